```python
import jax, jax.numpy as jnp
from jax import lax
import numpy as np

D_MODEL = 1024
BATCH = 4
SEQ = 8192
DEPTH = 2

MEM_LEN = 256
H_MEM = 4
MEM_HD = 128
MEM_WIDTH = H_MEM * MEM_HD
Q_BLOCK = 128
H_MLA = 8
MLA_NOPE = 64
MLA_ROPE = 32
MLA_V = 64
Q_LORA = 384
KV_LORA = 256
ROPE_THETA = 10000.0
MLA_WIDTH = H_MLA * MLA_V
H_FOX = 8
FOX_HD = 64
FOX_WIDTH = H_FOX * FOX_HD
MIX_WIDTH = MLA_WIDTH + FOX_WIDTH
IN_SIZES = (Q_LORA, KV_LORA, MLA_ROPE, FOX_WIDTH, FOX_WIDTH, FOX_WIDTH, H_FOX)
IN_COLS = Q_LORA + KV_LORA + MLA_ROPE + 3 * FOX_WIDTH + H_FOX
D_FF = -(-(8 * D_MODEL) // (3 * 256)) * 256
EPS = 1e-6

kernel_name = "hymba_style_mla_fox_hybrid"


def _rmsnorm(x, g):
    xf = x.astype(jnp.float32)
    y = xf * lax.rsqrt(jnp.mean(xf * xf, axis=-1, keepdims=True) + EPS)
    return (y * g.astype(jnp.float32)).astype(x.dtype)


def _rope_tables(seq, dtype):
    inv = 1.0 / (ROPE_THETA ** (jnp.arange(0, MLA_ROPE, 2, dtype=jnp.float32) / MLA_ROPE))
    ang = jnp.arange(seq, dtype=jnp.float32)[:, None] * inv[None, :]
    return jnp.cos(ang).astype(dtype), jnp.sin(ang).astype(dtype)


def _rope(x, cos, sin):
    x1, x2 = jnp.split(x, 2, axis=-1)
    return jnp.concatenate([x1 * cos - x2 * sin, x1 * sin + x2 * cos], axis=-1)


def _to_blocks(t):
    b, s = t.shape[:2]
    return t.reshape(b, s // Q_BLOCK, Q_BLOCK, *t.shape[2:]).swapaxes(0, 1)


def _from_blocks(t):
    nb, b, qb = t.shape[:3]
    return t.swapaxes(0, 1).reshape(b, nb * qb, *t.shape[3:])


def _causal_mask(i, seq):
    qpos = i * Q_BLOCK + jnp.arange(Q_BLOCK)
    return jnp.arange(seq)[None, :] <= qpos[:, None]


def _mla_attention(q_nope, q_rope, k_nope, k_rope, v):
    seq = k_nope.shape[1]
    scale = (MLA_NOPE + MLA_ROPE) ** -0.5

    def block(args):
        qn, qr, i = args
        s = (jnp.einsum('bqhd,bkhd->bhqk', qn, k_nope)
             + jnp.einsum('bqhr,bkr->bhqk', qr, k_rope)).astype(jnp.float32) * scale
        s = jnp.where(_causal_mask(i, seq), s, -jnp.inf)
        p = jax.nn.softmax(s, axis=-1).astype(v.dtype)
        return jnp.einsum('bhqk,bkhd->bqhd', p, v)

    out = lax.map(block, (_to_blocks(q_nope), _to_blocks(q_rope), jnp.arange(seq // Q_BLOCK)))
    return _from_blocks(out)


def _forgetting_attention(q, k, v, log_f):
    seq = k.shape[1]
    scale = FOX_HD ** -0.5
    cum = jnp.cumsum(log_f, axis=1)
    cum_k = cum.transpose(0, 2, 1)

    def block(args):
        qb, cum_q, i = args
        s = jnp.einsum('bqhd,bkhd->bhqk', qb, k).astype(jnp.float32) * scale
        s = s + (cum_q.transpose(0, 2, 1)[:, :, :, None] - cum_k[:, :, None, :])
        s = jnp.where(_causal_mask(i, seq), s, -jnp.inf)
        p = jax.nn.softmax(s, axis=-1).astype(v.dtype)
        return jnp.einsum('bhqk,bkhd->bqhd', p, v)

    out = lax.map(block, (_to_blocks(q), _to_blocks(cum), jnp.arange(seq // Q_BLOCK)))
    return _from_blocks(out)


def _hybrid_mixer(xn, w_in, cq_g, ckv_g, w_uq, w_ukv, f_bias, mla_g, fox_g, w_out, cos, sin):
    b, s, _ = xn.shape
    h = xn @ w_in
    offs = [int(o) for o in np.cumsum(IN_SIZES)[:-1]]
    c_q, c_kv, k_rope, q_f, k_f, v_f, f_logit = jnp.split(h, offs, axis=-1)

    q = (_rmsnorm(c_q, cq_g) @ w_uq).reshape(b, s, H_MLA, MLA_NOPE + MLA_ROPE)
    q_nope = q[..., :MLA_NOPE]
    q_rope = _rope(q[..., MLA_NOPE:], cos[:, None, :], sin[:, None, :])
    kv = (_rmsnorm(c_kv, ckv_g) @ w_ukv).reshape(b, s, H_MLA, MLA_NOPE + MLA_V)
    k_nope, v = kv[..., :MLA_NOPE], kv[..., MLA_NOPE:]
    k_rope = _rope(k_rope, cos, sin)
    o_mla = _mla_attention(q_nope, q_rope, k_nope, k_rope, v).reshape(b, s, MLA_WIDTH)

    heads = lambda t: t.reshape(b, s, H_FOX, FOX_HD)
    log_f = jax.nn.log_sigmoid((f_logit + f_bias).astype(jnp.float32))
    o_fox = _forgetting_attention(heads(q_f), heads(k_f), heads(v_f), log_f).reshape(b, s, FOX_WIDTH)

    merged = jnp.concatenate([_rmsnorm(o_mla, mla_g), _rmsnorm(o_fox, fox_g)], axis=-1)
    return merged @ w_out


def _memory_cross_attention(xn, mem, mem_g, w_mq, w_mkv, w_mo):
    b, s, _ = xn.shape
    m = mem.shape[1]
    q = (xn @ w_mq).reshape(b, s, H_MEM, MEM_HD)
    kv = (_rmsnorm(mem, mem_g) @ w_mkv).reshape(b, m, H_MEM, 2 * MEM_HD)
    k, v = kv[..., :MEM_HD], kv[..., MEM_HD:]
    sc = jnp.einsum('bqhd,bmhd->bhqm', q, k).astype(jnp.float32) * (MEM_HD ** -0.5)
    p = jax.nn.softmax(sc, axis=-1).astype(v.dtype)
    o = jnp.einsum('bhqm,bmhd->bqhd', p, v).reshape(b, s, MEM_WIDTH)
    return o @ w_mo


def _swiglu(xn, w_gate, w_up, w_down):
    return (jax.nn.silu(xn @ w_gate) * (xn @ w_up)) @ w_down


def setup_inputs(seed: int = 0) -> dict:
    key = jax.random.key(seed)
    ks = jax.random.split(key, 24)

    def dense(k, shape, fan_in):
        return jax.random.normal(k, shape, jnp.float32) * (fan_in ** -0.5)

    def gain(k, shape):
        return 1.0 + 0.02 * jax.random.normal(k, shape, jnp.float32)

    return {
        "x": jax.random.normal(ks[0], (BATCH, SEQ, D_MODEL), jnp.float32),
        "mem": jax.random.normal(ks[1], (BATCH, MEM_LEN, D_MODEL), jnp.float32),
        "mix_norm_g": gain(ks[2], (DEPTH, D_MODEL)),
        "w_in": dense(ks[3], (DEPTH, D_MODEL, IN_COLS), D_MODEL),
        "cq_norm_g": gain(ks[4], (DEPTH, Q_LORA)),
        "ckv_norm_g": gain(ks[5], (DEPTH, KV_LORA)),
        "w_uq": dense(ks[6], (DEPTH, Q_LORA, H_MLA * (MLA_NOPE + MLA_ROPE)), Q_LORA),
        "w_ukv": dense(ks[7], (DEPTH, KV_LORA, H_MLA * (MLA_NOPE + MLA_V)), KV_LORA),
        "forget_bias": jax.random.uniform(ks[8], (DEPTH, H_FOX), jnp.float32, 2.0, 6.0),
        "mla_out_g": gain(ks[9], (DEPTH, MLA_WIDTH)),
        "fox_out_g": gain(ks[10], (DEPTH, FOX_WIDTH)),
        "w_out": dense(ks[11], (DEPTH, MIX_WIDTH, D_MODEL), MIX_WIDTH),
        "mem_q_norm_g": gain(ks[12], (DEPTH, D_MODEL)),
        "mem_kv_norm_g": gain(ks[13], (DEPTH, D_MODEL)),
        "w_mq": dense(ks[14], (DEPTH, D_MODEL, MEM_WIDTH), D_MODEL),
        "w_mkv": dense(ks[15], (DEPTH, D_MODEL, 2 * MEM_WIDTH), D_MODEL),
        "w_mo": dense(ks[16], (DEPTH, MEM_WIDTH, D_MODEL), MEM_WIDTH),
        "ffn_norm_g": gain(ks[17], (DEPTH, D_MODEL)),
        "w_gate": dense(ks[18], (DEPTH, D_MODEL, D_FF), D_MODEL),
        "w_up": dense(ks[19], (DEPTH, D_MODEL, D_FF), D_MODEL),
        "w_down": dense(ks[20], (DEPTH, D_FF, D_MODEL), D_FF),
        "final_norm_g": gain(ks[21], (D_MODEL,)),
    }


def reference(x, mem, mix_norm_g, w_in, cq_norm_g, ckv_norm_g, w_uq, w_ukv, forget_bias,
              mla_out_g, fox_out_g, w_out, mem_q_norm_g, mem_kv_norm_g, w_mq, w_mkv, w_mo,
              ffn_norm_g, w_gate, w_up, w_down, final_norm_g):
    cos, sin = _rope_tables(x.shape[1], x.dtype)
    h = x
    for l in range(DEPTH):
        h = h + _hybrid_mixer(_rmsnorm(h, mix_norm_g[l]), w_in[l], cq_norm_g[l], ckv_norm_g[l],
                              w_uq[l], w_ukv[l], forget_bias[l], mla_out_g[l], fox_out_g[l],
                              w_out[l], cos, sin)
        h = h + _memory_cross_attention(_rmsnorm(h, mem_q_norm_g[l]), mem, mem_kv_norm_g[l],
                                        w_mq[l], w_mkv[l], w_mo[l])
        h = h + _swiglu(_rmsnorm(h, ffn_norm_g[l]), w_gate[l], w_up[l], w_down[l])
    return _rmsnorm(h, final_norm_g)
```

```python
import functools

import numpy as np
import jax
import jax.numpy as jnp
from jax import lax
from jax.experimental import pallas as pl
from jax.experimental.pallas import tpu as pltpu

F32 = jnp.float32
BF16 = jnp.bfloat16

EPS = 1e-6
ROPE_THETA = 10000.0
N_HEADS = 8
NOPE = 64
ROPE = 32
HEAD_V = 64
HEAD_PAD = 128
N_MEM_HEADS = 4
MEM_HD = 128
N_SPLIT = 3

LANE = 128
TM_PROJ = 256
TM_FFN = 256
BQ = 256
BK = 256
MASKED = -2e30
M_INIT = -1e30
VMEM_LIMIT = 56 * 1024 * 1024


def _mm(a, b):
    return jnp.dot(a, b, preferred_element_type=F32)


def _mm_nt(a, b):
    return lax.dot_general(a, b, (((1,), (1,)), ((), ())), preferred_element_type=F32)


def _rms(x, g):
    return x * lax.rsqrt(jnp.mean(x * x, axis=-1, keepdims=True) + EPS) * g


def _split3(x):
    hi = x.astype(BF16).astype(F32)
    r = x - hi
    mid = r.astype(BF16).astype(F32)
    lo = (r - mid).astype(BF16).astype(F32)
    return hi, mid, lo


def _proj_in_kernel(h_ref, rows_ref, tabs_ref, wx_ref, wq_ref, wkv_ref, sel_ref, tri_ref,
                    q_ref, k_ref, vt_ref, carry_ref, *, q_lora, kv_lora):
    tm = h_ref.shape[1]
    n_sub = vt_ref.shape[1]
    bk = vt_ref.shape[3]
    width = N_HEADS * HEAD_PAD

    @pl.when(pl.program_id(1) == 0)
    def _():
        carry_ref[...] = jnp.zeros_like(carry_ref)

    xn = _rms(h_ref[0], rows_ref[0:1, :]).astype(BF16)

    o_ckv = q_lora
    o_kr = o_ckv + kv_lora
    o_f3 = o_kr + 2 * LANE
    o_fq = o_f3 + LANE
    o_fk = o_fq + width
    o_fv = o_fk + width

    cos_q = tabs_ref[:, 0:LANE]
    sin_q = tabs_ref[:, LANE:2 * LANE]
    cos_k = tabs_ref[:, 2 * LANE:3 * LANE]
    sin_k = tabs_ref[:, 3 * LANE:4 * LANE]

    cq = _mm(xn, wx_ref[:, 0:q_lora])
    cqn = _rms(cq, rows_ref[1:2, 0:q_lora]).astype(BF16)
    qa = _mm(cqn, wq_ref[:, 0:width])
    qb = _mm(cqn, wq_ref[:, width:2 * width])
    for hd in range(N_HEADS):
        sl = slice(hd * HEAD_PAD, (hd + 1) * HEAD_PAD)
        q_ref[0, :, sl] = (qa[:, sl] * cos_q + qb[:, sl] * sin_q).astype(BF16)

    ckv = _mm(xn, wx_ref[:, o_ckv:o_kr])
    ckvn = _rms(ckv, rows_ref[1:2, q_lora:q_lora + kv_lora]).astype(BF16)
    kn = _mm(ckvn, wkv_ref[:, 0:width])
    v_mla = _mm(ckvn, wkv_ref[:, width:width + N_HEADS * HEAD_V])
    kr2 = _mm(xn, wx_ref[:, o_kr:o_f3])
    kr = kr2[:, 0:LANE] * cos_k + kr2[:, LANE:2 * LANE] * sin_k
    for hd in range(N_HEADS):
        sl = slice(hd * HEAD_PAD, (hd + 1) * HEAD_PAD)
        k_ref[0, :, sl] = (kn[:, sl] + kr).astype(BF16)

    lane = lax.broadcasted_iota(jnp.int32, (tm, LANE), 1)
    live = lane < N_SPLIT * N_HEADS
    f3 = _mm(xn, wx_ref[:, o_f3:o_fq]) + rows_ref[1:2, q_lora + kv_lora:q_lora + kv_lora + LANE]
    log_f = jnp.minimum(f3, 0.0) - jnp.log1p(jnp.exp(-jnp.abs(f3)))
    log_f = jnp.where(live, log_f, 0.0)
    pieces = jnp.concatenate([p.astype(BF16) for p in _split3(log_f)], axis=1)
    csum = _mm(tri_ref[...], pieces)
    cum = csum[:, 0:LANE] + csum[:, LANE:2 * LANE] + csum[:, 2 * LANE:3 * LANE] + carry_ref[...]
    carry_ref[...] = cum[tm - 1:tm, :]
    c_hi, c_mid, c_lo = _split3(cum)
    c_sel = jnp.where(lane < N_HEADS, c_hi, jnp.where(lane < 2 * N_HEADS, c_mid, c_lo))
    c_sel = jnp.where(live, c_sel, 0.0).astype(BF16)
    aug = _mm(c_sel, sel_ref[...])
    fq = _mm(xn, wx_ref[:, o_fq:o_fk])
    q_ref[0, :, width:2 * width] = (fq + aug[:, 0:width] + rows_ref[2:3, :]).astype(BF16)
    fk = _mm(xn, wx_ref[:, o_fk:o_fv])
    k_ref[0, :, width:2 * width] = (fk + aug[:, width:2 * width] + rows_ref[3:4, :]).astype(BF16)
    v_fox = _mm(xn, wx_ref[:, o_fv:o_fv + N_HEADS * HEAD_V])

    nv = N_HEADS * HEAD_V
    vt_mla = v_mla.T.astype(BF16)
    vt_fox = v_fox.T.astype(BF16)
    for c in range(n_sub):
        vt_ref[0, c, 0:nv, :] = vt_mla[:, c * bk:(c + 1) * bk]
        vt_ref[0, c, nv:2 * nv, :] = vt_fox[:, c * bk:(c + 1) * bk]


def _attn_kernel(q_ref, k_ref, vt_ref, o_ref):
    bq = q_ref.shape[1]
    bk = vt_ref.shape[3]
    qi = pl.program_id(2)
    q_all = q_ref[0]
    qs = [q_all[:, hh * HEAD_PAD:(hh + 1) * HEAD_PAD] for hh in range(2)]

    def tile(j, carry, masked):
        kb_all = k_ref[0, pl.ds(pl.multiple_of(j * bk, bk), bk), :]
        vb_all = vt_ref[0, j]
        out = []
        for hh in range(2):
            m, l, acc = carry[hh]
            s = _mm_nt(kb_all[:, hh * HEAD_PAD:(hh + 1) * HEAD_PAD], qs[hh])
            if masked:
                kpos = j * bk + lax.broadcasted_iota(jnp.int32, (bk, bq), 0)
                qpos = qi * bq + lax.broadcasted_iota(jnp.int32, (bk, bq), 1)
                s = jnp.where(kpos <= qpos, s, MASKED)
            m_new = jnp.maximum(m, jnp.max(s, axis=0, keepdims=True))
            alpha = jnp.exp(m - m_new)
            p = jnp.exp(s - m_new)
            l = alpha * l + jnp.sum(p, axis=0, keepdims=True)
            pv = _mm(vb_all[hh * HEAD_V:(hh + 1) * HEAD_V, :], p.astype(BF16))
            out.append((m_new, l, alpha * acc + pv))
        return tuple(out)

    init = tuple((jnp.full((1, bq), M_INIT, F32), jnp.zeros((1, bq), F32),
                  jnp.zeros((HEAD_V, bq), F32)) for _ in range(2))
    r = bq // bk
    carry = lax.fori_loop(0, qi * r, lambda j, c: tile(j, c, False), init)
    for u in range(r):
        carry = tile(qi * r + u, carry, True)
    ot = jnp.concatenate([acc * (1.0 / l) for (_, l, acc) in carry], axis=0)
    o_ref[0] = ot.T


def _post_attn_kernel(o_ref, h_ref, rows_ref, kv_ref, wo_ref, wmq_ref, wmo_ref, out_ref):
    half = o_ref.shape[2] // 2
    o = o_ref[0]
    merged = jnp.concatenate(
        [_rms(o[:, 0:half], rows_ref[0:1, 0:half]), _rms(o[:, half:], rows_ref[0:1, half:])],
        axis=1).astype(BF16)
    h1 = h_ref[0] + _mm(merged, wo_ref[...])
    xn = _rms(h1, rows_ref[1:2, :]).astype(BF16)
    q = (_mm(xn, wmq_ref[...]) * (MEM_HD ** -0.5)).astype(BF16)
    kv = kv_ref[0]
    heads = []
    for hd in range(N_MEM_HEADS):
        kh = kv[:, 2 * hd * MEM_HD:(2 * hd + 1) * MEM_HD]
        vh = kv[:, (2 * hd + 1) * MEM_HD:(2 * hd + 2) * MEM_HD]
        s = _mm_nt(q[:, hd * MEM_HD:(hd + 1) * MEM_HD], kh)
        e = jnp.exp(s - jnp.max(s, axis=-1, keepdims=True))
        l = jnp.sum(e, axis=-1, keepdims=True)
        heads.append(_mm(e.astype(BF16), vh) * (1.0 / l))
    om = jnp.concatenate(heads, axis=1).astype(BF16)
    out_ref[0] = h1 + _mm(om, wmo_ref[...])


def _mem_kv_kernel(mem_ref, g_ref, w_ref, kv_ref):
    kv_ref[0] = _mm(_rms(mem_ref[0], g_ref[...]).astype(BF16), w_ref[...]).astype(BF16)


def _ffn_kernel(h_ref, rows_ref, wg_ref, wu_ref, wd_ref, out_ref, *, final_norm):
    h = h_ref[0]
    xn = _rms(h, rows_ref[0:1, :]).astype(BF16)
    g = _mm(xn, wg_ref[...])
    u = _mm(xn, wu_ref[...])
    a = (g * (1.0 / (1.0 + jnp.exp(-g))) * u).astype(BF16)
    y = h + _mm(a, wd_ref[...])
    if final_norm:
        y = _rms(y, rows_ref[1:2, :])
    out_ref[0] = y


def _const_spec(shape):
    return pl.BlockSpec(shape, lambda *_: (0,) * len(shape), pipeline_mode=pl.Buffered(1))


def _pad_heads(w, d):
    kdim = w.shape[0]
    return jnp.pad(w.reshape(kdim, N_HEADS, d), ((0, 0), (0, 0), (0, HEAD_PAD - d))).reshape(
        kdim, N_HEADS * HEAD_PAD)


def _swap_halves(w):
    half = w.shape[-1] // 2
    return jnp.concatenate([w[..., half:], w[..., :half]], axis=-1)


def _decay_constants():
    width = N_HEADS * HEAD_PAD
    sel = np.zeros((LANE, 2 * width), np.float32)
    pat_q = np.zeros((width,), np.float32)
    pat_k = np.zeros((width,), np.float32)
    for hd in range(N_HEADS):
        for part in range(N_SPLIT):
            src = part * N_HEADS + hd
            sel[src, hd * HEAD_PAD + HEAD_V + part] = 1.0
            sel[src, width + hd * HEAD_PAD + HEAD_V + N_SPLIT + part] = -1.0
            pat_q[hd * HEAD_PAD + HEAD_V + N_SPLIT + part] = 1.0
            pat_k[hd * HEAD_PAD + HEAD_V + part] = 1.0
    return sel, pat_q, pat_k


def _rope_tables(seq):
    inv = 1.0 / (ROPE_THETA ** (jnp.arange(0, ROPE, 2, dtype=F32) / ROPE))
    ang = jnp.arange(seq, dtype=F32)[:, None] * inv[None, :]
    cos, sin = jnp.cos(ang), jnp.sin(ang)
    cc = jnp.concatenate([cos, cos], axis=1)
    ss = jnp.concatenate([-sin, sin], axis=1)
    ones = jnp.ones((seq, NOPE), F32)
    z_lo = jnp.zeros((seq, NOPE), F32)
    z_hi = jnp.zeros((seq, HEAD_PAD - NOPE - ROPE), F32)
    scale = (NOPE + ROPE) ** -0.5
    return jnp.concatenate([
        scale * jnp.concatenate([ones, cc, z_hi], axis=1),
        scale * jnp.concatenate([z_lo, ss, z_hi], axis=1),
        jnp.concatenate([z_lo, cc, z_hi], axis=1),
        jnp.concatenate([z_lo, ss, z_hi], axis=1)], axis=1)


def _layer_weights(l, w_in, w_uq, w_ukv, q_lora, kv_lora):
    d_model = w_in.shape[1]
    w = w_in[l]
    fox = N_HEADS * HEAD_V
    o = q_lora + kv_lora
    kr = w[:, o:o + ROPE]
    o += ROPE
    wfq, wfk, wfv = w[:, o:o + fox], w[:, o + fox:o + 2 * fox], w[:, o + 2 * fox:o + 3 * fox]
    wfl = w[:, o + 3 * fox:o + 3 * fox + N_HEADS]

    def place_rope(c):
        return jnp.pad(c, ((0, 0), (NOPE, HEAD_PAD - NOPE - ROPE)))

    gate3 = jnp.pad(jnp.concatenate([wfl] * N_SPLIT, axis=1), ((0, 0), (0, LANE - N_SPLIT * N_HEADS)))
    wx = jnp.concatenate([
        w[:, 0:q_lora + kv_lora], place_rope(kr), place_rope(_swap_halves(kr)), gate3,
        _pad_heads(wfq * (HEAD_V ** -0.5), HEAD_V), _pad_heads(wfk, HEAD_V), wfv], axis=1).astype(BF16)

    uq = w_uq[l].reshape(q_lora, N_HEADS, NOPE + ROPE)
    rope_cols = uq[:, :, NOPE:]
    pad_hi = jnp.zeros((q_lora, N_HEADS, HEAD_PAD - NOPE - ROPE), F32)
    wq_a = jnp.concatenate([uq[:, :, :NOPE], rope_cols, pad_hi], axis=2)
    wq_b = jnp.concatenate([jnp.zeros((q_lora, N_HEADS, NOPE), F32), _swap_halves(rope_cols), pad_hi], axis=2)
    wq = jnp.concatenate([wq_a.reshape(q_lora, -1), wq_b.reshape(q_lora, -1)], axis=1).astype(BF16)

    ukv = w_ukv[l].reshape(kv_lora, N_HEADS, NOPE + HEAD_V)
    wkv = jnp.concatenate([_pad_heads(ukv[:, :, :NOPE].reshape(kv_lora, -1), NOPE),
                           ukv[:, :, NOPE:].reshape(kv_lora, -1)], axis=1).astype(BF16)
    del d_model
    return wx, wq, wkv


def kernel(x, mem, mix_norm_g, w_in, cq_norm_g, ckv_norm_g, w_uq, w_ukv, forget_bias, mla_out_g, fox_out_g, w_out, mem_q_norm_g, mem_kv_norm_g, w_mq, w_mkv, w_mo, ffn_norm_g, w_gate, w_up, w_down, final_norm_g):
    bsz, seq, d_model = x.shape
    depth = w_in.shape[0]
    q_lora = cq_norm_g.shape[1]
    kv_lora = ckv_norm_g.shape[1]
    mem_len = mem.shape[1]
    d_ff = w_gate.shape[2]
    width = N_HEADS * HEAD_PAD
    n_v = 2 * N_HEADS * HEAD_V
    assert d_model == width == n_v, "layout assumes d_model = 8 heads * 128"
    assert seq % BQ == 0 and seq % TM_PROJ == 0 and TM_PROJ % BK == 0 and BQ % BK == 0

    tabs = _rope_tables(seq)
    sel_np, pat_q, pat_k = _decay_constants()
    sel = jnp.asarray(sel_np, BF16)
    tri = jnp.asarray(np.tril(np.ones((TM_PROJ, TM_PROJ), np.float32)), BF16)
    params = pltpu.CompilerParams

    h = x
    for l in range(depth):
        wx, wq, wkv = _layer_weights(l, w_in, w_uq, w_ukv, q_lora, kv_lora)
        bias3 = jnp.pad(jnp.concatenate([forget_bias[l]] * N_SPLIT), (0, LANE - N_SPLIT * N_HEADS))
        row1 = jnp.pad(jnp.concatenate([cq_norm_g[l], ckv_norm_g[l], bias3]),
                       (0, d_model - q_lora - kv_lora - LANE))
        rows1 = jnp.zeros((8, d_model), F32).at[0].set(mix_norm_g[l]).at[1].set(row1)
        rows1 = rows1.at[2].set(pat_q).at[3].set(pat_k)

        q_all, k_all, vt_all = pl.pallas_call(
            functools.partial(_proj_in_kernel, q_lora=q_lora, kv_lora=kv_lora),
            grid=(bsz, seq // TM_PROJ),
            in_specs=[
                pl.BlockSpec((1, TM_PROJ, d_model), lambda b, t: (b, t, 0)),
                _const_spec((8, d_model)),
                pl.BlockSpec((TM_PROJ, 4 * LANE), lambda b, t: (t, 0)),
                _const_spec(wx.shape), _const_spec(wq.shape), _const_spec(wkv.shape),
                _const_spec(sel.shape), _const_spec(tri.shape),
            ],
            out_specs=[
                pl.BlockSpec((1, TM_PROJ, 2 * width), lambda b, t: (b, t, 0)),
                pl.BlockSpec((1, TM_PROJ, 2 * width), lambda b, t: (b, t, 0)),
                pl.BlockSpec((1, TM_PROJ // BK, n_v, BK), lambda b, t: (b, t, 0, 0)),
            ],
            out_shape=[
                jax.ShapeDtypeStruct((bsz, seq, 2 * width), BF16),
                jax.ShapeDtypeStruct((bsz, seq, 2 * width), BF16),
                jax.ShapeDtypeStruct((bsz, seq // BK, n_v, BK), BF16),
            ],
            scratch_shapes=[pltpu.VMEM((1, LANE), F32)],
            compiler_params=params(dimension_semantics=("arbitrary", "arbitrary"),
                                   vmem_limit_bytes=VMEM_LIMIT),
            name=f"proj_in_{l}",
        )(h, rows1, tabs, wx, wq, wkv, sel, tri)

        o_all = pl.pallas_call(
            _attn_kernel,
            grid=(bsz, N_HEADS, seq // BQ),
            in_specs=[
                pl.BlockSpec((1, BQ, 2 * HEAD_PAD), lambda b, p, i: (b, i, p)),
                pl.BlockSpec((1, seq, 2 * HEAD_PAD), lambda b, p, i: (b, 0, p)),
                pl.BlockSpec((1, seq // BK, 2 * HEAD_V, BK), lambda b, p, i: (b, 0, p, 0)),
            ],
            out_specs=pl.BlockSpec((1, BQ, 2 * HEAD_V), lambda b, p, i: (b, i, p)),
            out_shape=jax.ShapeDtypeStruct((bsz, seq, n_v), F32),
            compiler_params=params(dimension_semantics=("parallel", "parallel", "parallel"),
                                   vmem_limit_bytes=VMEM_LIMIT),
            name=f"attn_{l}",
        )(q_all, k_all, vt_all)

        kv_mem = pl.pallas_call(
            _mem_kv_kernel,
            grid=(bsz,),
            in_specs=[
                pl.BlockSpec((1, mem_len, d_model), lambda b: (b, 0, 0)),
                _const_spec((1, d_model)),
                _const_spec(w_mkv.shape[1:]),
            ],
            out_specs=pl.BlockSpec((1, mem_len, w_mkv.shape[2]), lambda b: (b, 0, 0)),
            out_shape=jax.ShapeDtypeStruct((bsz, mem_len, w_mkv.shape[2]), BF16),
            compiler_params=params(dimension_semantics=("parallel",), vmem_limit_bytes=VMEM_LIMIT),
            name=f"mem_kv_{l}",
        )(mem, mem_kv_norm_g[l][None, :], w_mkv[l].astype(BF16))

        rows3 = jnp.zeros((8, d_model), F32).at[0].set(jnp.concatenate([mla_out_g[l], fox_out_g[l]]))
        rows3 = rows3.at[1].set(mem_q_norm_g[l])
        h = pl.pallas_call(
            _post_attn_kernel,
            grid=(bsz, seq // TM_PROJ),
            in_specs=[
                pl.BlockSpec((1, TM_PROJ, n_v), lambda b, t: (b, t, 0)),
                pl.BlockSpec((1, TM_PROJ, d_model), lambda b, t: (b, t, 0)),
                _const_spec((8, d_model)),
                pl.BlockSpec((1, mem_len, w_mkv.shape[2]), lambda b, t: (b, 0, 0)),
                _const_spec(w_out.shape[1:]), _const_spec(w_mq.shape[1:]), _const_spec(w_mo.shape[1:]),
            ],
            out_specs=pl.BlockSpec((1, TM_PROJ, d_model), lambda b, t: (b, t, 0)),
            out_shape=jax.ShapeDtypeStruct((bsz, seq, d_model), F32),
            compiler_params=params(dimension_semantics=("parallel", "parallel"),
                                   vmem_limit_bytes=VMEM_LIMIT),
            name=f"post_attn_{l}",
        )(o_all, h, rows3, kv_mem, w_out[l].astype(BF16), w_mq[l].astype(BF16), w_mo[l].astype(BF16))

        last = l == depth - 1
        rows4 = jnp.zeros((8, d_model), F32).at[0].set(ffn_norm_g[l]).at[1].set(final_norm_g)
        h = pl.pallas_call(
            functools.partial(_ffn_kernel, final_norm=last),
            grid=(bsz, seq // TM_FFN),
            in_specs=[
                pl.BlockSpec((1, TM_FFN, d_model), lambda b, t: (b, t, 0)),
                _const_spec((8, d_model)),
                _const_spec((d_model, d_ff)), _const_spec((d_model, d_ff)), _const_spec((d_ff, d_model)),
            ],
            out_specs=pl.BlockSpec((1, TM_FFN, d_model), lambda b, t: (b, t, 0)),
            out_shape=jax.ShapeDtypeStruct((bsz, seq, d_model), F32),
            compiler_params=params(dimension_semantics=("parallel", "parallel"),
                                   vmem_limit_bytes=VMEM_LIMIT),
            name=f"ffn_{l}",
        )(h, rows4, w_gate[l].astype(BF16), w_up[l].astype(BF16), w_down[l].astype(BF16))
    return h
```

```python
import functools

import numpy as np
import jax
import jax.numpy as jnp
from jax import lax
from jax.experimental import pallas as pl
from jax.experimental.pallas import tpu as pltpu

F32 = jnp.float32
BF16 = jnp.bfloat16

EPS = 1e-6
ROPE_THETA = 10000.0
N_HEADS = 8
NOPE = 64
ROPE = 32
HEAD_V = 64
HEAD_PAD = 128
N_MEM_HEADS = 4
MEM_HD = 128
N_SPLIT = 3

LANE = 128
TM_PROJ = 256
TM_FFN = 256
BQ = 512
BK = 256
MASKED = -2e30
M_INIT = -1e30
LOG2E = 1.4426950408889634
VMEM_LIMIT = 56 * 1024 * 1024


def _mm(a, b):
    return jnp.dot(a, b, preferred_element_type=F32)


def _mm_nt(a, b):
    return lax.dot_general(a, b, (((1,), (1,)), ((), ())), preferred_element_type=F32)


def _rms(x, g):
    return x * lax.rsqrt(jnp.mean(x * x, axis=-1, keepdims=True) + EPS) * g


def _split3(x):
    hi = x.astype(BF16).astype(F32)
    r = x - hi
    mid = r.astype(BF16).astype(F32)
    lo = (r - mid).astype(BF16).astype(F32)
    return hi, mid, lo


def _proj_in_kernel(h_ref, rows_ref, tabs_ref, wx_ref, wq_ref, wkv_ref, sel_ref, tri_ref,
                    q_ref, k_ref, vt_ref, carry_ref, *, q_lora, kv_lora):
    tm = h_ref.shape[1]
    n_sub = vt_ref.shape[1]
    bk = vt_ref.shape[3]
    width = N_HEADS * HEAD_PAD

    @pl.when(pl.program_id(1) == 0)
    def _():
        carry_ref[...] = jnp.zeros_like(carry_ref)

    xn = _rms(h_ref[0], rows_ref[0:1, :]).astype(BF16)

    o_ckv = q_lora
    o_kr = o_ckv + kv_lora
    o_f3 = o_kr + 2 * LANE
    o_fq = o_f3 + LANE
    o_fk = o_fq + width
    o_fv = o_fk + width

    cos_q = tabs_ref[:, 0:LANE]
    sin_q = tabs_ref[:, LANE:2 * LANE]
    cos_k = tabs_ref[:, 2 * LANE:3 * LANE]
    sin_k = tabs_ref[:, 3 * LANE:4 * LANE]

    cq = _mm(xn, wx_ref[:, 0:q_lora])
    cqn = _rms(cq, rows_ref[1:2, 0:q_lora]).astype(BF16)
    qa = _mm(cqn, wq_ref[:, 0:width])
    qb = _mm(cqn, wq_ref[:, width:2 * width])
    for hd in range(N_HEADS):
        sl = slice(hd * HEAD_PAD, (hd + 1) * HEAD_PAD)
        q_ref[0, :, sl] = (qa[:, sl] * cos_q + qb[:, sl] * sin_q).astype(BF16)

    ckv = _mm(xn, wx_ref[:, o_ckv:o_kr])
    ckvn = _rms(ckv, rows_ref[1:2, q_lora:q_lora + kv_lora]).astype(BF16)
    kn = _mm(ckvn, wkv_ref[:, 0:width])
    v_mla = _mm(ckvn, wkv_ref[:, width:width + N_HEADS * HEAD_V])
    kr2 = _mm(xn, wx_ref[:, o_kr:o_f3])
    kr = kr2[:, 0:LANE] * cos_k + kr2[:, LANE:2 * LANE] * sin_k
    for hd in range(N_HEADS):
        sl = slice(hd * HEAD_PAD, (hd + 1) * HEAD_PAD)
        k_ref[0, :, sl] = (kn[:, sl] + kr).astype(BF16)

    lane = lax.broadcasted_iota(jnp.int32, (tm, LANE), 1)
    live = lane < N_SPLIT * N_HEADS
    f3 = _mm(xn, wx_ref[:, o_f3:o_fq]) + rows_ref[1:2, q_lora + kv_lora:q_lora + kv_lora + LANE]
    log_f = jnp.minimum(f3, 0.0) - jnp.log1p(jnp.exp(-jnp.abs(f3)))
    log_f = jnp.where(live, log_f, 0.0)
    pieces = jnp.concatenate([p.astype(BF16) for p in _split3(log_f)], axis=1)
    csum = _mm(tri_ref[...], pieces)
    cum = csum[:, 0:LANE] + csum[:, LANE:2 * LANE] + csum[:, 2 * LANE:3 * LANE] + carry_ref[...]
    carry_ref[...] = cum[tm - 1:tm, :]
    c_hi, c_mid, c_lo = _split3(cum * LOG2E)
    c_sel = jnp.where(lane < N_HEADS, c_hi, jnp.where(lane < 2 * N_HEADS, c_mid, c_lo))
    c_sel = jnp.where(live, c_sel, 0.0).astype(BF16)
    aug = _mm(c_sel, sel_ref[...])
    fq = _mm(xn, wx_ref[:, o_fq:o_fk])
    q_ref[0, :, width:2 * width] = (fq + aug[:, 0:width] + rows_ref[2:3, :]).astype(BF16)
    fk = _mm(xn, wx_ref[:, o_fk:o_fv])
    k_ref[0, :, width:2 * width] = (fk + aug[:, width:2 * width] + rows_ref[3:4, :]).astype(BF16)
    v_fox = _mm(xn, wx_ref[:, o_fv:o_fv + N_HEADS * HEAD_V])

    nv = N_HEADS * HEAD_V
    vt_mla = v_mla.T.astype(BF16)
    vt_fox = v_fox.T.astype(BF16)
    for c in range(n_sub):
        vt_ref[0, c, 0:nv, :] = vt_mla[:, c * bk:(c + 1) * bk]
        vt_ref[0, c, nv:2 * nv, :] = vt_fox[:, c * bk:(c + 1) * bk]


def _attn_kernel(q_ref, k_ref, vt_ref, o_ref, s_scr, p_scr, mt_scr, al_scr, m_scr, l_scr, acc_scr):
    bq = q_ref.shape[1]
    bk = vt_ref.shape[3]
    assert bq == 2 * bk, "two key tiles straddle the diagonal; the loop is unrolled by two"
    qi = pl.program_id(2)
    n_full = 2 * qi

    def scores(j, slot, masked):
        kb = k_ref[0, pl.ds(pl.multiple_of(j * bk, bk), bk), :]
        for hh in range(2):
            sl = slice(hh * HEAD_PAD, (hh + 1) * HEAD_PAD)
            s = _mm_nt(kb[:, sl], q_ref[0, :, sl])
            if masked:
                kpos = j * bk + lax.broadcasted_iota(jnp.int32, (bk, bq), 0)
                qpos = qi * bq + lax.broadcasted_iota(jnp.int32, (bk, bq), 1)
                s = jnp.where(kpos <= qpos, s, MASKED)
            s_scr[slot, hh] = s
            mt_scr[slot, hh] = jnp.max(s, axis=0, keepdims=True)

    def softmax(slot):
        for hh in range(2):
            m_new = jnp.maximum(m_scr[hh], mt_scr[slot, hh])
            alpha = jnp.exp2(m_scr[hh] - m_new)
            p = jnp.exp2(s_scr[slot, hh] - m_new)
            l_scr[hh] = alpha * l_scr[hh] + jnp.sum(p, axis=0, keepdims=True)
            m_scr[hh] = m_new
            al_scr[slot, hh] = alpha
            p_scr[slot, hh] = p.astype(BF16)

    def values(j, slot):
        vb = vt_ref[0, j]
        for hh in range(2):
            pv = _mm(vb[hh * HEAD_V:(hh + 1) * HEAD_V, :], p_scr[slot, hh])
            acc_scr[hh] = al_scr[slot, hh] * acc_scr[hh] + pv

    def tile_at(pos):
        return jnp.where(pos < 2, n_full + pos, pos - 2)

    m_scr[...] = jnp.full_like(m_scr, M_INIT)
    l_scr[...] = jnp.zeros_like(l_scr)
    acc_scr[...] = jnp.zeros_like(acc_scr)

    scores(n_full, 0, True)
    scores(n_full + 1, 1, True)
    softmax(0)

    def body(t, carry):
        scores(2 * t, 0, False)
        values(tile_at(2 * t), 0)
        softmax(1)
        scores(2 * t + 1, 1, False)
        values(tile_at(2 * t + 1), 1)
        softmax(0)
        return carry

    lax.fori_loop(0, qi, body, 0)
    values(tile_at(n_full), 0)
    softmax(1)
    values(tile_at(n_full + 1), 1)

    ot = jnp.concatenate([acc_scr[hh] * (1.0 / l_scr[hh]) for hh in range(2)], axis=0)
    o_ref[0] = ot.T


def _post_attn_kernel(o_ref, h_ref, rows_ref, kv_ref, wo_ref, wmq_ref, wmo_ref, out_ref):
    half = o_ref.shape[2] // 2
    o = o_ref[0]
    merged = jnp.concatenate(
        [_rms(o[:, 0:half], rows_ref[0:1, 0:half]), _rms(o[:, half:], rows_ref[0:1, half:])],
        axis=1).astype(BF16)
    h1 = h_ref[0] + _mm(merged, wo_ref[...])
    xn = _rms(h1, rows_ref[1:2, :]).astype(BF16)
    q = (_mm(xn, wmq_ref[...]) * (MEM_HD ** -0.5)).astype(BF16)
    kv = kv_ref[0]
    heads = []
    for hd in range(N_MEM_HEADS):
        kh = kv[:, 2 * hd * MEM_HD:(2 * hd + 1) * MEM_HD]
        vh = kv[:, (2 * hd + 1) * MEM_HD:(2 * hd + 2) * MEM_HD]
        s = _mm_nt(q[:, hd * MEM_HD:(hd + 1) * MEM_HD], kh)
        e = jnp.exp(s - jnp.max(s, axis=-1, keepdims=True))
        l = jnp.sum(e, axis=-1, keepdims=True)
        heads.append(_mm(e.astype(BF16), vh) * (1.0 / l))
    om = jnp.concatenate(heads, axis=1).astype(BF16)
    out_ref[0] = h1 + _mm(om, wmo_ref[...])


def _mem_kv_kernel(mem_ref, g_ref, w_ref, kv_ref):
    kv_ref[0] = _mm(_rms(mem_ref[0], g_ref[...]).astype(BF16), w_ref[...]).astype(BF16)


def _ffn_kernel(h_ref, rows_ref, wg_ref, wu_ref, wd_ref, out_ref, *, final_norm):
    h = h_ref[0]
    xn = _rms(h, rows_ref[0:1, :]).astype(BF16)
    g = _mm(xn, wg_ref[...])
    u = _mm(xn, wu_ref[...])
    a = (g * (1.0 / (1.0 + jnp.exp(-g))) * u).astype(BF16)
    y = h + _mm(a, wd_ref[...])
    if final_norm:
        y = _rms(y, rows_ref[1:2, :])
    out_ref[0] = y


def _const_spec(shape):
    return pl.BlockSpec(shape, lambda *_: (0,) * len(shape), pipeline_mode=pl.Buffered(1))


def _pad_heads(w, d):
    kdim = w.shape[0]
    return jnp.pad(w.reshape(kdim, N_HEADS, d), ((0, 0), (0, 0), (0, HEAD_PAD - d))).reshape(
        kdim, N_HEADS * HEAD_PAD)


def _swap_halves(w):
    half = w.shape[-1] // 2
    return jnp.concatenate([w[..., half:], w[..., :half]], axis=-1)


def _decay_constants():
    width = N_HEADS * HEAD_PAD
    sel = np.zeros((LANE, 2 * width), np.float32)
    pat_q = np.zeros((width,), np.float32)
    pat_k = np.zeros((width,), np.float32)
    for hd in range(N_HEADS):
        for part in range(N_SPLIT):
            src = part * N_HEADS + hd
            sel[src, hd * HEAD_PAD + HEAD_V + part] = 1.0
            sel[src, width + hd * HEAD_PAD + HEAD_V + N_SPLIT + part] = -1.0
            pat_q[hd * HEAD_PAD + HEAD_V + N_SPLIT + part] = 1.0
            pat_k[hd * HEAD_PAD + HEAD_V + part] = 1.0
    return sel, pat_q, pat_k


def _rope_tables(seq):
    inv = 1.0 / (ROPE_THETA ** (jnp.arange(0, ROPE, 2, dtype=F32) / ROPE))
    ang = jnp.arange(seq, dtype=F32)[:, None] * inv[None, :]
    cos, sin = jnp.cos(ang), jnp.sin(ang)
    cc = jnp.concatenate([cos, cos], axis=1)
    ss = jnp.concatenate([-sin, sin], axis=1)
    ones = jnp.ones((seq, NOPE), F32)
    z_lo = jnp.zeros((seq, NOPE), F32)
    z_hi = jnp.zeros((seq, HEAD_PAD - NOPE - ROPE), F32)
    scale = LOG2E * (NOPE + ROPE) ** -0.5
    return jnp.concatenate([
        scale * jnp.concatenate([ones, cc, z_hi], axis=1),
        scale * jnp.concatenate([z_lo, ss, z_hi], axis=1),
        jnp.concatenate([z_lo, cc, z_hi], axis=1),
        jnp.concatenate([z_lo, ss, z_hi], axis=1)], axis=1)


def _layer_weights(l, w_in, w_uq, w_ukv, q_lora, kv_lora):
    d_model = w_in.shape[1]
    w = w_in[l]
    fox = N_HEADS * HEAD_V
    o = q_lora + kv_lora
    kr = w[:, o:o + ROPE]
    o += ROPE
    wfq, wfk, wfv = w[:, o:o + fox], w[:, o + fox:o + 2 * fox], w[:, o + 2 * fox:o + 3 * fox]
    wfl = w[:, o + 3 * fox:o + 3 * fox + N_HEADS]

    def place_rope(c):
        return jnp.pad(c, ((0, 0), (NOPE, HEAD_PAD - NOPE - ROPE)))

    gate3 = jnp.pad(jnp.concatenate([wfl] * N_SPLIT, axis=1), ((0, 0), (0, LANE - N_SPLIT * N_HEADS)))
    wx = jnp.concatenate([
        w[:, 0:q_lora + kv_lora], place_rope(kr), place_rope(_swap_halves(kr)), gate3,
        _pad_heads(wfq * (LOG2E * HEAD_V ** -0.5), HEAD_V), _pad_heads(wfk, HEAD_V), wfv], axis=1).astype(BF16)

    uq = w_uq[l].reshape(q_lora, N_HEADS, NOPE + ROPE)
    rope_cols = uq[:, :, NOPE:]
    pad_hi = jnp.zeros((q_lora, N_HEADS, HEAD_PAD - NOPE - ROPE), F32)
    wq_a = jnp.concatenate([uq[:, :, :NOPE], rope_cols, pad_hi], axis=2)
    wq_b = jnp.concatenate([jnp.zeros((q_lora, N_HEADS, NOPE), F32), _swap_halves(rope_cols), pad_hi], axis=2)
    wq = jnp.concatenate([wq_a.reshape(q_lora, -1), wq_b.reshape(q_lora, -1)], axis=1).astype(BF16)

    ukv = w_ukv[l].reshape(kv_lora, N_HEADS, NOPE + HEAD_V)
    wkv = jnp.concatenate([_pad_heads(ukv[:, :, :NOPE].reshape(kv_lora, -1), NOPE),
                           ukv[:, :, NOPE:].reshape(kv_lora, -1)], axis=1).astype(BF16)
    del d_model
    return wx, wq, wkv


def kernel(x, mem, mix_norm_g, w_in, cq_norm_g, ckv_norm_g, w_uq, w_ukv, forget_bias, mla_out_g, fox_out_g, w_out, mem_q_norm_g, mem_kv_norm_g, w_mq, w_mkv, w_mo, ffn_norm_g, w_gate, w_up, w_down, final_norm_g):
    bsz, seq, d_model = x.shape
    depth = w_in.shape[0]
    q_lora = cq_norm_g.shape[1]
    kv_lora = ckv_norm_g.shape[1]
    mem_len = mem.shape[1]
    d_ff = w_gate.shape[2]
    width = N_HEADS * HEAD_PAD
    n_v = 2 * N_HEADS * HEAD_V
    assert d_model == width == n_v, "layout assumes d_model = 8 heads * 128"
    assert seq % BQ == 0 and seq % TM_PROJ == 0 and TM_PROJ % BK == 0 and BQ == 2 * BK

    tabs = _rope_tables(seq)
    sel_np, pat_q, pat_k = _decay_constants()
    sel = jnp.asarray(sel_np, BF16)
    tri = jnp.asarray(np.tril(np.ones((TM_PROJ, TM_PROJ), np.float32)), BF16)
    params = pltpu.CompilerParams

    h = x
    for l in range(depth):
        wx, wq, wkv = _layer_weights(l, w_in, w_uq, w_ukv, q_lora, kv_lora)
        bias3 = jnp.pad(jnp.concatenate([forget_bias[l]] * N_SPLIT), (0, LANE - N_SPLIT * N_HEADS))
        row1 = jnp.pad(jnp.concatenate([cq_norm_g[l], ckv_norm_g[l], bias3]),
                       (0, d_model - q_lora - kv_lora - LANE))
        rows1 = jnp.zeros((8, d_model), F32).at[0].set(mix_norm_g[l]).at[1].set(row1)
        rows1 = rows1.at[2].set(pat_q).at[3].set(pat_k)

        q_all, k_all, vt_all = pl.pallas_call(
            functools.partial(_proj_in_kernel, q_lora=q_lora, kv_lora=kv_lora),
            grid=(bsz, seq // TM_PROJ),
            in_specs=[
                pl.BlockSpec((1, TM_PROJ, d_model), lambda b, t: (b, t, 0)),
                _const_spec((8, d_model)),
                pl.BlockSpec((TM_PROJ, 4 * LANE), lambda b, t: (t, 0)),
                _const_spec(wx.shape), _const_spec(wq.shape), _const_spec(wkv.shape),
                _const_spec(sel.shape), _const_spec(tri.shape),
            ],
            out_specs=[
                pl.BlockSpec((1, TM_PROJ, 2 * width), lambda b, t: (b, t, 0)),
                pl.BlockSpec((1, TM_PROJ, 2 * width), lambda b, t: (b, t, 0)),
                pl.BlockSpec((1, TM_PROJ // BK, n_v, BK), lambda b, t: (b, t, 0, 0)),
            ],
            out_shape=[
                jax.ShapeDtypeStruct((bsz, seq, 2 * width), BF16),
                jax.ShapeDtypeStruct((bsz, seq, 2 * width), BF16),
                jax.ShapeDtypeStruct((bsz, seq // BK, n_v, BK), BF16),
            ],
            scratch_shapes=[pltpu.VMEM((1, LANE), F32)],
            compiler_params=params(dimension_semantics=("arbitrary", "arbitrary"),
                                   vmem_limit_bytes=VMEM_LIMIT),
            name=f"proj_in_{l}",
        )(h, rows1, tabs, wx, wq, wkv, sel, tri)

        o_all = pl.pallas_call(
            _attn_kernel,
            grid=(bsz, N_HEADS, seq // BQ),
            in_specs=[
                pl.BlockSpec((1, BQ, 2 * HEAD_PAD), lambda b, p, i: (b, i, p)),
                pl.BlockSpec((1, seq, 2 * HEAD_PAD), lambda b, p, i: (b, 0, p)),
                pl.BlockSpec((1, seq // BK, 2 * HEAD_V, BK), lambda b, p, i: (b, 0, p, 0)),
            ],
            out_specs=pl.BlockSpec((1, BQ, 2 * HEAD_V), lambda b, p, i: (b, i, p)),
            out_shape=jax.ShapeDtypeStruct((bsz, seq, n_v), F32),
            scratch_shapes=[
                pltpu.VMEM((2, 2, BK, BQ), F32),
                pltpu.VMEM((2, 2, BK, BQ), BF16),
                pltpu.VMEM((2, 2, 1, BQ), F32),
                pltpu.VMEM((2, 2, 1, BQ), F32),
                pltpu.VMEM((2, 1, BQ), F32),
                pltpu.VMEM((2, 1, BQ), F32),
                pltpu.VMEM((2, HEAD_V, BQ), F32),
            ],
            compiler_params=params(dimension_semantics=("parallel", "parallel", "parallel"),
                                   vmem_limit_bytes=VMEM_LIMIT),
            name=f"attn_{l}",
        )(q_all, k_all, vt_all)

        kv_mem = pl.pallas_call(
            _mem_kv_kernel,
            grid=(bsz,),
            in_specs=[
                pl.BlockSpec((1, mem_len, d_model), lambda b: (b, 0, 0)),
                _const_spec((1, d_model)),
                _const_spec(w_mkv.shape[1:]),
            ],
            out_specs=pl.BlockSpec((1, mem_len, w_mkv.shape[2]), lambda b: (b, 0, 0)),
            out_shape=jax.ShapeDtypeStruct((bsz, mem_len, w_mkv.shape[2]), BF16),
            compiler_params=params(dimension_semantics=("parallel",), vmem_limit_bytes=VMEM_LIMIT),
            name=f"mem_kv_{l}",
        )(mem, mem_kv_norm_g[l][None, :], w_mkv[l].astype(BF16))

        rows3 = jnp.zeros((8, d_model), F32).at[0].set(jnp.concatenate([mla_out_g[l], fox_out_g[l]]))
        rows3 = rows3.at[1].set(mem_q_norm_g[l])
        h = pl.pallas_call(
            _post_attn_kernel,
            grid=(bsz, seq // TM_PROJ),
            in_specs=[
                pl.BlockSpec((1, TM_PROJ, n_v), lambda b, t: (b, t, 0)),
                pl.BlockSpec((1, TM_PROJ, d_model), lambda b, t: (b, t, 0)),
                _const_spec((8, d_model)),
                pl.BlockSpec((1, mem_len, w_mkv.shape[2]), lambda b, t: (b, 0, 0)),
                _const_spec(w_out.shape[1:]), _const_spec(w_mq.shape[1:]), _const_spec(w_mo.shape[1:]),
            ],
            out_specs=pl.BlockSpec((1, TM_PROJ, d_model), lambda b, t: (b, t, 0)),
            out_shape=jax.ShapeDtypeStruct((bsz, seq, d_model), F32),
            compiler_params=params(dimension_semantics=("parallel", "parallel"),
                                   vmem_limit_bytes=VMEM_LIMIT),
            name=f"post_attn_{l}",
        )(o_all, h, rows3, kv_mem, w_out[l].astype(BF16), w_mq[l].astype(BF16), w_mo[l].astype(BF16))

        last = l == depth - 1
        rows4 = jnp.zeros((8, d_model), F32).at[0].set(ffn_norm_g[l]).at[1].set(final_norm_g)
        h = pl.pallas_call(
            functools.partial(_ffn_kernel, final_norm=last),
            grid=(bsz, seq // TM_FFN),
            in_specs=[
                pl.BlockSpec((1, TM_FFN, d_model), lambda b, t: (b, t, 0)),
                _const_spec((8, d_model)),
                _const_spec((d_model, d_ff)), _const_spec((d_model, d_ff)), _const_spec((d_ff, d_model)),
            ],
            out_specs=pl.BlockSpec((1, TM_FFN, d_model), lambda b, t: (b, t, 0)),
            out_shape=jax.ShapeDtypeStruct((bsz, seq, d_model), F32),
            compiler_params=params(dimension_semantics=("parallel", "parallel"),
                                   vmem_limit_bytes=VMEM_LIMIT),
            name=f"ffn_{l}",
        )(h, rows4, w_gate[l].astype(BF16), w_up[l].astype(BF16), w_down[l].astype(BF16))
    return h
```

```python
import functools

import numpy as np
import jax
import jax.numpy as jnp
from jax import lax
from jax.experimental import pallas as pl
from jax.experimental.pallas import tpu as pltpu

F32 = jnp.float32
BF16 = jnp.bfloat16

EPS = 1e-6
ROPE_THETA = 10000.0
N_HEADS = 8
NOPE = 64
ROPE = 32
HEAD_V = 64
HEAD_PAD = 128
N_MEM_HEADS = 4
MEM_HD = 128
N_SPLIT = 3
ONES_ROWS = 16

LANE = 128
TM_PROJ = 256
TM_FFN = 256
BQ = 1024
BK = 256
MASKED = -2e30
M_INIT = -1e30
LOG2E = 1.4426950408889634
VMEM_LIMIT = 56 * 1024 * 1024


def _mm(a, b):
    return jnp.dot(a, b, preferred_element_type=F32)


def _mm_nt(a, b):
    return lax.dot_general(a, b, (((1,), (1,)), ((), ())), preferred_element_type=F32)


def _rms(x, g):
    return x * lax.rsqrt(jnp.mean(x * x, axis=-1, keepdims=True) + EPS) * g


def _split3(x):
    hi = x.astype(BF16).astype(F32)
    r = x - hi
    mid = r.astype(BF16).astype(F32)
    lo = (r - mid).astype(BF16).astype(F32)
    return hi, mid, lo


def _proj_in_kernel(h_ref, rows_ref, tabs_ref, wx_ref, wq_ref, wkv_ref, sel_ref, tri_ref,
                    q_ref, k_ref, vt_ref, carry_ref, *, q_lora, kv_lora):
    tm = h_ref.shape[1]
    n_sub = vt_ref.shape[1]
    bk = vt_ref.shape[3]
    width = N_HEADS * HEAD_PAD

    @pl.when(pl.program_id(1) == 0)
    def _():
        carry_ref[...] = jnp.zeros_like(carry_ref)

    xn = _rms(h_ref[0], rows_ref[0:1, :]).astype(BF16)

    o_ckv = q_lora
    o_kr = o_ckv + kv_lora
    o_f3 = o_kr + 2 * LANE
    o_fq = o_f3 + LANE
    o_fk = o_fq + width
    o_fv = o_fk + width

    cos_q = tabs_ref[:, 0:LANE]
    sin_q = tabs_ref[:, LANE:2 * LANE]
    cos_k = tabs_ref[:, 2 * LANE:3 * LANE]
    sin_k = tabs_ref[:, 3 * LANE:4 * LANE]

    cq = _mm(xn, wx_ref[:, 0:q_lora])
    cqn = _rms(cq, rows_ref[1:2, 0:q_lora]).astype(BF16)
    qa = _mm(cqn, wq_ref[:, 0:width])
    qb = _mm(cqn, wq_ref[:, width:2 * width])
    for hd in range(N_HEADS):
        sl = slice(hd * HEAD_PAD, (hd + 1) * HEAD_PAD)
        q_ref[0, :, sl] = (qa[:, sl] * cos_q + qb[:, sl] * sin_q).astype(BF16)

    ckv = _mm(xn, wx_ref[:, o_ckv:o_kr])
    ckvn = _rms(ckv, rows_ref[1:2, q_lora:q_lora + kv_lora]).astype(BF16)
    kn = _mm(ckvn, wkv_ref[:, 0:width])
    v_mla = _mm(ckvn, wkv_ref[:, width:width + N_HEADS * HEAD_V])
    kr2 = _mm(xn, wx_ref[:, o_kr:o_f3])
    kr = kr2[:, 0:LANE] * cos_k + kr2[:, LANE:2 * LANE] * sin_k
    for hd in range(N_HEADS):
        sl = slice(hd * HEAD_PAD, (hd + 1) * HEAD_PAD)
        k_ref[0, :, sl] = (kn[:, sl] + kr).astype(BF16)

    lane = lax.broadcasted_iota(jnp.int32, (tm, LANE), 1)
    live = lane < N_SPLIT * N_HEADS
    f3 = _mm(xn, wx_ref[:, o_f3:o_fq]) + rows_ref[1:2, q_lora + kv_lora:q_lora + kv_lora + LANE]
    log_f = jnp.minimum(f3, 0.0) - jnp.log1p(jnp.exp(-jnp.abs(f3)))
    log_f = jnp.where(live, log_f, 0.0)
    pieces = jnp.concatenate([p.astype(BF16) for p in _split3(log_f)], axis=1)
    csum = _mm(tri_ref[...], pieces)
    cum = csum[:, 0:LANE] + csum[:, LANE:2 * LANE] + csum[:, 2 * LANE:3 * LANE] + carry_ref[...]
    carry_ref[...] = cum[tm - 1:tm, :]
    c_hi, c_mid, c_lo = _split3(cum * LOG2E)
    c_sel = jnp.where(lane < N_HEADS, c_hi, jnp.where(lane < 2 * N_HEADS, c_mid, c_lo))
    c_sel = jnp.where(live, c_sel, 0.0).astype(BF16)
    aug = _mm(c_sel, sel_ref[...])
    fq = _mm(xn, wx_ref[:, o_fq:o_fk])
    q_ref[0, :, width:2 * width] = (fq + aug[:, 0:width] + rows_ref[2:3, :]).astype(BF16)
    fk = _mm(xn, wx_ref[:, o_fk:o_fv])
    k_ref[0, :, width:2 * width] = (fk + aug[:, width:2 * width] + rows_ref[3:4, :]).astype(BF16)
    v_fox = _mm(xn, wx_ref[:, o_fv:o_fv + N_HEADS * HEAD_V])

    nv = N_HEADS * HEAD_V
    vt_mla = v_mla.T.astype(BF16)
    vt_fox = v_fox.T.astype(BF16)
    for c in range(n_sub):
        vt_ref[0, c, 0:nv, :] = vt_mla[:, c * bk:(c + 1) * bk]
        vt_ref[0, c, nv:2 * nv, :] = vt_fox[:, c * bk:(c + 1) * bk]


def _attn_kernel(q_ref, k_ref, vt_ref, o_ref, s_scr, p_scr, mt_scr, al_scr, m_scr, acc_scr):
    bq = q_ref.shape[1]
    bk = vt_ref.shape[3]
    n_sub = bq // bk
    qi = pl.program_id(2)
    heads = [slice(hh * HEAD_PAD, (hh + 1) * HEAD_PAD) for hh in range(2)]

    def k_tile(j):
        return k_ref[0, pl.ds(pl.multiple_of(j * bk, bk), bk), :]

    def put_scores(u, hh, s):
        s_scr[u, hh] = s
        mt_scr[u, hh] = jnp.max(s, axis=0, keepdims=True)

    def scores_group(g):
        for u in range(n_sub):
            kb = k_tile(n_sub * g + u)
            for hh in range(2):
                put_scores(u, hh, _mm_nt(kb[:, heads[hh]], q_ref[0, :, heads[hh]]))

    def scores_diagonal():
        tri = (lax.broadcasted_iota(jnp.int32, (bk, bk), 0)
               <= lax.broadcasted_iota(jnp.int32, (bk, bk), 1))
        for u in range(n_sub):
            kb = k_tile(n_sub * qi + u)
            for hh in range(2):
                s = _mm_nt(kb[:, heads[hh]], q_ref[0, u * bk:, heads[hh]])
                parts = [jnp.full((bk, u * bk), MASKED, F32)] if u > 0 else []
                parts.append(jnp.where(tri, s[:, :bk], MASKED))
                if u < n_sub - 1:
                    parts.append(s[:, bk:])
                put_scores(u, hh, jnp.concatenate(parts, axis=1) if len(parts) > 1 else parts[0])

    def softmax_group():
        for hh in range(2):
            mt = mt_scr[0, hh]
            for u in range(1, n_sub):
                mt = jnp.maximum(mt, mt_scr[u, hh])
            m_new = jnp.maximum(m_scr[hh], mt)
            al_scr[hh] = jnp.exp2(m_scr[hh] - m_new)
            m_scr[hh] = m_new
            for u in range(n_sub):
                p_scr[u, hh] = jnp.exp2(s_scr[u, hh] - m_new).astype(BF16)

    def values_group(pos):
        j0 = jnp.where(pos == 0, n_sub * qi, n_sub * (pos - 1))
        ones = jnp.ones((ONES_ROWS, bk), BF16)
        for hh in range(2):
            pv = None
            for u in range(n_sub):
                vb = vt_ref[0, j0 + u]
                v = jnp.concatenate([vb[hh * HEAD_V:(hh + 1) * HEAD_V, :], ones], axis=0)
                d = _mm(v, p_scr[u, hh])
                pv = d if pv is None else pv + d
            acc_scr[hh] = al_scr[hh] * acc_scr[hh] + pv

    m_scr[...] = jnp.full_like(m_scr, M_INIT)
    acc_scr[...] = jnp.zeros_like(acc_scr)
    scores_diagonal()

    def body(g, carry):
        softmax_group()
        scores_group(g)
        values_group(g)
        return carry

    lax.fori_loop(0, qi, body, 0)
    softmax_group()
    values_group(qi)
    ot = jnp.concatenate(
        [acc_scr[hh, 0:HEAD_V] * (1.0 / acc_scr[hh, HEAD_V:HEAD_V + 1]) for hh in range(2)], axis=0)
    o_ref[0] = ot.T


def _post_attn_kernel(o_ref, h_ref, rows_ref, kv_ref, wo_ref, wmq_ref, wmo_ref, out_ref):
    half = o_ref.shape[2] // 2
    o = o_ref[0]
    merged = jnp.concatenate(
        [_rms(o[:, 0:half], rows_ref[0:1, 0:half]), _rms(o[:, half:], rows_ref[0:1, half:])],
        axis=1).astype(BF16)
    h1 = h_ref[0] + _mm(merged, wo_ref[...])
    xn = _rms(h1, rows_ref[1:2, :]).astype(BF16)
    q = (_mm(xn, wmq_ref[...]) * (MEM_HD ** -0.5)).astype(BF16)
    kv = kv_ref[0]
    heads = []
    for hd in range(N_MEM_HEADS):
        kh = kv[:, 2 * hd * MEM_HD:(2 * hd + 1) * MEM_HD]
        vh = kv[:, (2 * hd + 1) * MEM_HD:(2 * hd + 2) * MEM_HD]
        s = _mm_nt(q[:, hd * MEM_HD:(hd + 1) * MEM_HD], kh)
        e = jnp.exp(s - jnp.max(s, axis=-1, keepdims=True))
        l = jnp.sum(e, axis=-1, keepdims=True)
        heads.append(_mm(e.astype(BF16), vh) * (1.0 / l))
    om = jnp.concatenate(heads, axis=1).astype(BF16)
    out_ref[0] = h1 + _mm(om, wmo_ref[...])


def _mem_kv_kernel(mem_ref, g_ref, w_ref, kv_ref):
    kv_ref[0] = _mm(_rms(mem_ref[0], g_ref[...]).astype(BF16), w_ref[...]).astype(BF16)


def _ffn_kernel(h_ref, rows_ref, wg_ref, wu_ref, wd_ref, out_ref, *, final_norm):
    h = h_ref[0]
    xn = _rms(h, rows_ref[0:1, :]).astype(BF16)
    g = _mm(xn, wg_ref[...])
    u = _mm(xn, wu_ref[...])
    a = (g * (1.0 / (1.0 + jnp.exp(-g))) * u).astype(BF16)
    y = h + _mm(a, wd_ref[...])
    if final_norm:
        y = _rms(y, rows_ref[1:2, :])
    out_ref[0] = y


def _const_spec(shape):
    return pl.BlockSpec(shape, lambda *_: (0,) * len(shape), pipeline_mode=pl.Buffered(1))


def _pad_heads(w, d):
    kdim = w.shape[0]
    return jnp.pad(w.reshape(kdim, N_HEADS, d), ((0, 0), (0, 0), (0, HEAD_PAD - d))).reshape(
        kdim, N_HEADS * HEAD_PAD)


def _swap_halves(w):
    half = w.shape[-1] // 2
    return jnp.concatenate([w[..., half:], w[..., :half]], axis=-1)


def _decay_constants():
    width = N_HEADS * HEAD_PAD
    sel = np.zeros((LANE, 2 * width), np.float32)
    pat_q = np.zeros((width,), np.float32)
    pat_k = np.zeros((width,), np.float32)
    for hd in range(N_HEADS):
        for part in range(N_SPLIT):
            src = part * N_HEADS + hd
            sel[src, hd * HEAD_PAD + HEAD_V + part] = 1.0
            sel[src, width + hd * HEAD_PAD + HEAD_V + N_SPLIT + part] = -1.0
            pat_q[hd * HEAD_PAD + HEAD_V + N_SPLIT + part] = 1.0
            pat_k[hd * HEAD_PAD + HEAD_V + part] = 1.0
    return sel, pat_q, pat_k


def _rope_tables(seq):
    inv = 1.0 / (ROPE_THETA ** (jnp.arange(0, ROPE, 2, dtype=F32) / ROPE))
    ang = jnp.arange(seq, dtype=F32)[:, None] * inv[None, :]
    cos, sin = jnp.cos(ang), jnp.sin(ang)
    cc = jnp.concatenate([cos, cos], axis=1)
    ss = jnp.concatenate([-sin, sin], axis=1)
    ones = jnp.ones((seq, NOPE), F32)
    z_lo = jnp.zeros((seq, NOPE), F32)
    z_hi = jnp.zeros((seq, HEAD_PAD - NOPE - ROPE), F32)
    scale = LOG2E * (NOPE + ROPE) ** -0.5
    return jnp.concatenate([
        scale * jnp.concatenate([ones, cc, z_hi], axis=1),
        scale * jnp.concatenate([z_lo, ss, z_hi], axis=1),
        jnp.concatenate([z_lo, cc, z_hi], axis=1),
        jnp.concatenate([z_lo, ss, z_hi], axis=1)], axis=1)


def _layer_weights(l, w_in, w_uq, w_ukv, q_lora, kv_lora):
    d_model = w_in.shape[1]
    w = w_in[l]
    fox = N_HEADS * HEAD_V
    o = q_lora + kv_lora
    kr = w[:, o:o + ROPE]
    o += ROPE
    wfq, wfk, wfv = w[:, o:o + fox], w[:, o + fox:o + 2 * fox], w[:, o + 2 * fox:o + 3 * fox]
    wfl = w[:, o + 3 * fox:o + 3 * fox + N_HEADS]

    def place_rope(c):
        return jnp.pad(c, ((0, 0), (NOPE, HEAD_PAD - NOPE - ROPE)))

    gate3 = jnp.pad(jnp.concatenate([wfl] * N_SPLIT, axis=1), ((0, 0), (0, LANE - N_SPLIT * N_HEADS)))
    wx = jnp.concatenate([
        w[:, 0:q_lora + kv_lora], place_rope(kr), place_rope(_swap_halves(kr)), gate3,
        _pad_heads(wfq * (LOG2E * HEAD_V ** -0.5), HEAD_V), _pad_heads(wfk, HEAD_V), wfv], axis=1).astype(BF16)

    uq = w_uq[l].reshape(q_lora, N_HEADS, NOPE + ROPE)
    rope_cols = uq[:, :, NOPE:]
    pad_hi = jnp.zeros((q_lora, N_HEADS, HEAD_PAD - NOPE - ROPE), F32)
    wq_a = jnp.concatenate([uq[:, :, :NOPE], rope_cols, pad_hi], axis=2)
    wq_b = jnp.concatenate([jnp.zeros((q_lora, N_HEADS, NOPE), F32), _swap_halves(rope_cols), pad_hi], axis=2)
    wq = jnp.concatenate([wq_a.reshape(q_lora, -1), wq_b.reshape(q_lora, -1)], axis=1).astype(BF16)

    ukv = w_ukv[l].reshape(kv_lora, N_HEADS, NOPE + HEAD_V)
    wkv = jnp.concatenate([_pad_heads(ukv[:, :, :NOPE].reshape(kv_lora, -1), NOPE),
                           ukv[:, :, NOPE:].reshape(kv_lora, -1)], axis=1).astype(BF16)
    del d_model
    return wx, wq, wkv


def kernel(x, mem, mix_norm_g, w_in, cq_norm_g, ckv_norm_g, w_uq, w_ukv, forget_bias, mla_out_g, fox_out_g, w_out, mem_q_norm_g, mem_kv_norm_g, w_mq, w_mkv, w_mo, ffn_norm_g, w_gate, w_up, w_down, final_norm_g):
    bsz, seq, d_model = x.shape
    depth = w_in.shape[0]
    q_lora = cq_norm_g.shape[1]
    kv_lora = ckv_norm_g.shape[1]
    mem_len = mem.shape[1]
    d_ff = w_gate.shape[2]
    width = N_HEADS * HEAD_PAD
    n_v = 2 * N_HEADS * HEAD_V
    assert d_model == width == n_v, "layout assumes d_model = 8 heads * 128"
    assert seq % BQ == 0 and seq % TM_PROJ == 0 and TM_PROJ % BK == 0 and BQ % BK == 0

    tabs = _rope_tables(seq)
    sel_np, pat_q, pat_k = _decay_constants()
    sel = jnp.asarray(sel_np, BF16)
    tri = jnp.asarray(np.tril(np.ones((TM_PROJ, TM_PROJ), np.float32)), BF16)
    params = pltpu.CompilerParams

    h = x
    for l in range(depth):
        wx, wq, wkv = _layer_weights(l, w_in, w_uq, w_ukv, q_lora, kv_lora)
        bias3 = jnp.pad(jnp.concatenate([forget_bias[l]] * N_SPLIT), (0, LANE - N_SPLIT * N_HEADS))
        row1 = jnp.pad(jnp.concatenate([cq_norm_g[l], ckv_norm_g[l], bias3]),
                       (0, d_model - q_lora - kv_lora - LANE))
        rows1 = jnp.zeros((8, d_model), F32).at[0].set(mix_norm_g[l]).at[1].set(row1)
        rows1 = rows1.at[2].set(pat_q).at[3].set(pat_k)

        q_all, k_all, vt_all = pl.pallas_call(
            functools.partial(_proj_in_kernel, q_lora=q_lora, kv_lora=kv_lora),
            grid=(bsz, seq // TM_PROJ),
            in_specs=[
                pl.BlockSpec((1, TM_PROJ, d_model), lambda b, t: (b, t, 0)),
                _const_spec((8, d_model)),
                pl.BlockSpec((TM_PROJ, 4 * LANE), lambda b, t: (t, 0)),
                _const_spec(wx.shape), _const_spec(wq.shape), _const_spec(wkv.shape),
                _const_spec(sel.shape), _const_spec(tri.shape),
            ],
            out_specs=[
                pl.BlockSpec((1, TM_PROJ, 2 * width), lambda b, t: (b, t, 0)),
                pl.BlockSpec((1, TM_PROJ, 2 * width), lambda b, t: (b, t, 0)),
                pl.BlockSpec((1, TM_PROJ // BK, n_v, BK), lambda b, t: (b, t, 0, 0)),
            ],
            out_shape=[
                jax.ShapeDtypeStruct((bsz, seq, 2 * width), BF16),
                jax.ShapeDtypeStruct((bsz, seq, 2 * width), BF16),
                jax.ShapeDtypeStruct((bsz, seq // BK, n_v, BK), BF16),
            ],
            scratch_shapes=[pltpu.VMEM((1, LANE), F32)],
            compiler_params=params(dimension_semantics=("arbitrary", "arbitrary"),
                                   vmem_limit_bytes=VMEM_LIMIT),
            name=f"proj_in_{l}",
        )(h, rows1, tabs, wx, wq, wkv, sel, tri)

        o_all = pl.pallas_call(
            _attn_kernel,
            grid=(bsz, N_HEADS, seq // BQ),
            in_specs=[
                pl.BlockSpec((1, BQ, 2 * HEAD_PAD), lambda b, p, i: (b, i, p)),
                pl.BlockSpec((1, seq, 2 * HEAD_PAD), lambda b, p, i: (b, 0, p)),
                pl.BlockSpec((1, seq // BK, 2 * HEAD_V, BK), lambda b, p, i: (b, 0, p, 0)),
            ],
            out_specs=pl.BlockSpec((1, BQ, 2 * HEAD_V), lambda b, p, i: (b, i, p)),
            out_shape=jax.ShapeDtypeStruct((bsz, seq, n_v), F32),
            scratch_shapes=[
                pltpu.VMEM((BQ // BK, 2, BK, BQ), F32),
                pltpu.VMEM((BQ // BK, 2, BK, BQ), BF16),
                pltpu.VMEM((BQ // BK, 2, 1, BQ), F32),
                pltpu.VMEM((2, 1, BQ), F32),
                pltpu.VMEM((2, 1, BQ), F32),
                pltpu.VMEM((2, HEAD_V + ONES_ROWS, BQ), F32),
            ],
            compiler_params=params(dimension_semantics=("parallel", "parallel", "parallel"),
                                   vmem_limit_bytes=VMEM_LIMIT),
            name=f"attn_{l}",
        )(q_all, k_all, vt_all)

        kv_mem = pl.pallas_call(
            _mem_kv_kernel,
            grid=(bsz,),
            in_specs=[
                pl.BlockSpec((1, mem_len, d_model), lambda b: (b, 0, 0)),
                _const_spec((1, d_model)),
                _const_spec(w_mkv.shape[1:]),
            ],
            out_specs=pl.BlockSpec((1, mem_len, w_mkv.shape[2]), lambda b: (b, 0, 0)),
            out_shape=jax.ShapeDtypeStruct((bsz, mem_len, w_mkv.shape[2]), BF16),
            compiler_params=params(dimension_semantics=("parallel",), vmem_limit_bytes=VMEM_LIMIT),
            name=f"mem_kv_{l}",
        )(mem, mem_kv_norm_g[l][None, :], w_mkv[l].astype(BF16))

        rows3 = jnp.zeros((8, d_model), F32).at[0].set(jnp.concatenate([mla_out_g[l], fox_out_g[l]]))
        rows3 = rows3.at[1].set(mem_q_norm_g[l])
        h = pl.pallas_call(
            _post_attn_kernel,
            grid=(bsz, seq // TM_PROJ),
            in_specs=[
                pl.BlockSpec((1, TM_PROJ, n_v), lambda b, t: (b, t, 0)),
                pl.BlockSpec((1, TM_PROJ, d_model), lambda b, t: (b, t, 0)),
                _const_spec((8, d_model)),
                pl.BlockSpec((1, mem_len, w_mkv.shape[2]), lambda b, t: (b, 0, 0)),
                _const_spec(w_out.shape[1:]), _const_spec(w_mq.shape[1:]), _const_spec(w_mo.shape[1:]),
            ],
            out_specs=pl.BlockSpec((1, TM_PROJ, d_model), lambda b, t: (b, t, 0)),
            out_shape=jax.ShapeDtypeStruct((bsz, seq, d_model), F32),
            compiler_params=params(dimension_semantics=("parallel", "parallel"),
                                   vmem_limit_bytes=VMEM_LIMIT),
            name=f"post_attn_{l}",
        )(o_all, h, rows3, kv_mem, w_out[l].astype(BF16), w_mq[l].astype(BF16), w_mo[l].astype(BF16))

        last = l == depth - 1
        rows4 = jnp.zeros((8, d_model), F32).at[0].set(ffn_norm_g[l]).at[1].set(final_norm_g)
        h = pl.pallas_call(
            functools.partial(_ffn_kernel, final_norm=last),
            grid=(bsz, seq // TM_FFN),
            in_specs=[
                pl.BlockSpec((1, TM_FFN, d_model), lambda b, t: (b, t, 0)),
                _const_spec((8, d_model)),
                _const_spec((d_model, d_ff)), _const_spec((d_model, d_ff)), _const_spec((d_ff, d_model)),
            ],
            out_specs=pl.BlockSpec((1, TM_FFN, d_model), lambda b, t: (b, t, 0)),
            out_shape=jax.ShapeDtypeStruct((bsz, seq, d_model), F32),
            compiler_params=params(dimension_semantics=("parallel", "parallel"),
                                   vmem_limit_bytes=VMEM_LIMIT),
            name=f"ffn_{l}",
        )(h, rows4, w_gate[l].astype(BF16), w_up[l].astype(BF16), w_down[l].astype(BF16))
    return h
```

```python
import functools

import numpy as np
import jax
import jax.numpy as jnp
from jax import lax
from jax.experimental import pallas as pl
from jax.experimental.pallas import tpu as pltpu

F32 = jnp.float32
BF16 = jnp.bfloat16

EPS = 1e-6
ROPE_THETA = 10000.0
N_HEADS = 8
NOPE = 64
ROPE = 32
HEAD_V = 64
HEAD_PAD = 128
N_MEM_HEADS = 4
MEM_HD = 128
N_SPLIT = 3
ONES_ROWS = 16

LANE = 128
TM_PROJ = 512
TM_FFN = 512
BQ = 1024
BK = 256
MASKED = -2e30
M_INIT = -1e30
LOG2E = 1.4426950408889634
VMEM_LIMIT = 56 * 1024 * 1024


def _mm(a, b):
    return jnp.dot(a, b, preferred_element_type=F32)


def _mm_nt(a, b):
    return lax.dot_general(a, b, (((1,), (1,)), ((), ())), preferred_element_type=F32)


def _rms(x, g):
    return x * lax.rsqrt(jnp.mean(x * x, axis=-1, keepdims=True) + EPS) * g


def _split3(x):
    hi = x.astype(BF16).astype(F32)
    r = x - hi
    mid = r.astype(BF16).astype(F32)
    lo = (r - mid).astype(BF16).astype(F32)
    return hi, mid, lo


def _proj_in_kernel(h_ref, rows_ref, tabs_ref, wx_ref, wq_ref, wkv_ref, sel_ref, tri_ref,
                    q_ref, k_ref, vt_ref, carry_ref, *, q_lora, kv_lora):
    tm = h_ref.shape[1]
    n_sub = vt_ref.shape[1]
    bk = vt_ref.shape[3]
    width = N_HEADS * HEAD_PAD
    nv = N_HEADS * HEAD_V

    @pl.when(pl.program_id(1) == 0)
    def _():
        carry_ref[...] = jnp.zeros_like(carry_ref)

    xn = _rms(h_ref[0], rows_ref[0:1, :]).astype(BF16)

    o_ckv = q_lora
    o_kr = o_ckv + kv_lora
    o_f3 = o_kr + 2 * LANE
    o_fq = o_f3 + LANE
    o_fk = o_fq + nv
    o_fv = o_fk + nv

    cos_q = tabs_ref[:, 0:LANE]
    sin_q = tabs_ref[:, LANE:2 * LANE]
    cos_k = tabs_ref[:, 2 * LANE:3 * LANE]
    sin_k = tabs_ref[:, 3 * LANE:4 * LANE]

    def put_values(v, row0):
        vt = v.T.astype(BF16)
        for c in range(n_sub):
            vt_ref[0, c, row0:row0 + nv, :] = vt[:, c * bk:(c + 1) * bk]

    lane = lax.broadcasted_iota(jnp.int32, (tm, LANE), 1)
    low = lane < HEAD_V

    def put_head_pairs(out_ref, col0, narrow, fill):
        for pair in range(N_HEADS // 2):
            blk = narrow[:, pair * LANE:(pair + 1) * LANE]
            other = fill(pair)
            c = col0 + 2 * pair * HEAD_PAD
            out_ref[0, :, c:c + HEAD_PAD] = jnp.where(low, blk, other).astype(BF16)
            odd = pltpu.roll(jnp.where(low, other, blk), HEAD_V, axis=1)
            out_ref[0, :, c + HEAD_PAD:c + 2 * HEAD_PAD] = odd.astype(BF16)

    cq = _mm(xn, wx_ref[:, 0:q_lora])
    ckv = _mm(xn, wx_ref[:, o_ckv:o_kr])
    f3 = _mm(xn, wx_ref[:, o_f3:o_fq]) + rows_ref[1:2, q_lora + kv_lora:q_lora + kv_lora + LANE]
    kr2 = _mm(xn, wx_ref[:, o_kr:o_f3])
    put_values(_mm(xn, wx_ref[:, o_fv:o_fv + nv]), nv)

    cqn = _rms(cq, rows_ref[1:2, 0:q_lora]).astype(BF16)
    qa = _mm(cqn, wq_ref[:, 0:width])
    qb = _mm(cqn, wq_ref[:, width:2 * width])
    for hd in range(N_HEADS):
        sl = slice(hd * HEAD_PAD, (hd + 1) * HEAD_PAD)
        q_ref[0, :, sl] = (qa[:, sl] * cos_q + qb[:, sl] * sin_q).astype(BF16)

    ckvn = _rms(ckv, rows_ref[1:2, q_lora:q_lora + kv_lora]).astype(BF16)
    kn = _mm(ckvn, wkv_ref[:, 0:nv])
    kr = kr2[:, 0:LANE] * cos_k + kr2[:, LANE:2 * LANE] * sin_k
    kr_both = kr + pltpu.roll(kr, HEAD_V, axis=1)
    put_head_pairs(k_ref, 0, kn, lambda pair: kr_both)
    put_values(_mm(ckvn, wkv_ref[:, nv:2 * nv]), 0)

    live = lane < N_SPLIT * N_HEADS
    log_f = jnp.minimum(f3, 0.0) - jnp.log1p(jnp.exp(-jnp.abs(f3)))
    log_f = jnp.where(live, log_f, 0.0)
    pieces = jnp.concatenate([p.astype(BF16) for p in _split3(log_f)], axis=1)
    csum = _mm(tri_ref[...], pieces)
    fq = _mm(xn, wx_ref[:, o_fq:o_fk])
    cum = csum[:, 0:LANE] + csum[:, LANE:2 * LANE] + csum[:, 2 * LANE:3 * LANE] + carry_ref[...]
    carry_ref[...] = cum[tm - 1:tm, :]
    c_hi, c_mid, c_lo = _split3(cum * LOG2E)
    c_sel = jnp.where(lane < N_HEADS, c_hi, jnp.where(lane < 2 * N_HEADS, c_mid, c_lo))
    c_sel = jnp.where(live, c_sel, 0.0).astype(BF16)
    fk = _mm(xn, wx_ref[:, o_fk:o_fv])
    aug = _mm(c_sel, sel_ref[...])
    aug = aug + rows_ref[2:3, :]
    put_head_pairs(q_ref, width, fq, lambda pair: aug[:, pair * LANE:(pair + 1) * LANE])
    put_head_pairs(k_ref, width, fk, lambda pair: aug[:, nv + pair * LANE:nv + (pair + 1) * LANE])


def _attn_kernel(q_ref, k_ref, vt_ref, o_ref, s_scr, p_scr, mt_scr, al_scr, m_scr, acc_scr):
    bq = q_ref.shape[1]
    bk = vt_ref.shape[3]
    n_sub = bq // bk
    qi = pl.program_id(2)
    heads = [slice(hh * HEAD_PAD, (hh + 1) * HEAD_PAD) for hh in range(2)]

    def k_tile(j):
        return k_ref[0, pl.ds(pl.multiple_of(j * bk, bk), bk), :]

    def put_scores(u, hh, s):
        s_scr[u, hh] = s
        mt_scr[u, hh] = jnp.max(s, axis=0, keepdims=True)

    def scores_group(g):
        for u in range(n_sub):
            kb = k_tile(n_sub * g + u)
            for hh in range(2):
                put_scores(u, hh, _mm_nt(kb[:, heads[hh]], q_ref[0, :, heads[hh]]))

    def scores_diagonal():
        tri = (lax.broadcasted_iota(jnp.int32, (bk, bk), 0)
               <= lax.broadcasted_iota(jnp.int32, (bk, bk), 1))
        for u in range(n_sub):
            kb = k_tile(n_sub * qi + u)
            for hh in range(2):
                s = _mm_nt(kb[:, heads[hh]], q_ref[0, u * bk:, heads[hh]])
                parts = [jnp.full((bk, u * bk), MASKED, F32)] if u > 0 else []
                parts.append(jnp.where(tri, s[:, :bk], MASKED))
                if u < n_sub - 1:
                    parts.append(s[:, bk:])
                put_scores(u, hh, jnp.concatenate(parts, axis=1) if len(parts) > 1 else parts[0])

    def softmax_group():
        for hh in range(2):
            mt = mt_scr[0, hh]
            for u in range(1, n_sub):
                mt = jnp.maximum(mt, mt_scr[u, hh])
            m_new = jnp.maximum(m_scr[hh], mt)
            al_scr[hh] = jnp.exp2(m_scr[hh] - m_new)
            m_scr[hh] = m_new
            for u in range(n_sub):
                p_scr[u, hh] = jnp.exp2(s_scr[u, hh] - m_new).astype(BF16)

    def values_group(pos):
        j0 = jnp.where(pos == 0, n_sub * qi, n_sub * (pos - 1))
        ones = jnp.ones((ONES_ROWS, bk), BF16)
        for hh in range(2):
            pv = None
            for u in range(n_sub):
                vb = vt_ref[0, j0 + u]
                v = jnp.concatenate([vb[hh * HEAD_V:(hh + 1) * HEAD_V, :], ones], axis=0)
                d = _mm(v, p_scr[u, hh])
                pv = d if pv is None else pv + d
            acc_scr[hh] = al_scr[hh] * acc_scr[hh] + pv

    m_scr[...] = jnp.full_like(m_scr, M_INIT)
    acc_scr[...] = jnp.zeros_like(acc_scr)
    scores_diagonal()

    def body(g, carry):
        softmax_group()
        scores_group(g)
        values_group(g)
        return carry

    lax.fori_loop(0, qi, body, 0)
    softmax_group()
    values_group(qi)
    ot = jnp.concatenate(
        [acc_scr[hh, 0:HEAD_V] * (1.0 / acc_scr[hh, HEAD_V:HEAD_V + 1]) for hh in range(2)], axis=0)
    o_ref[0] = ot.T


def _post_attn_kernel(o_ref, h_ref, rows_ref, kv_ref, wo_ref, wmq_ref, wmo_ref, out_ref):
    half = o_ref.shape[2] // 2
    o = o_ref[0]
    merged = jnp.concatenate(
        [_rms(o[:, 0:half], rows_ref[0:1, 0:half]), _rms(o[:, half:], rows_ref[0:1, half:])],
        axis=1).astype(BF16)
    h1 = h_ref[0] + _mm(merged, wo_ref[...])
    xn = _rms(h1, rows_ref[1:2, :]).astype(BF16)
    q = (_mm(xn, wmq_ref[...]) * (MEM_HD ** -0.5)).astype(BF16)
    kv = kv_ref[0]
    heads = []
    for hd in range(N_MEM_HEADS):
        kh = kv[:, 2 * hd * MEM_HD:(2 * hd + 1) * MEM_HD]
        vh = kv[:, (2 * hd + 1) * MEM_HD:(2 * hd + 2) * MEM_HD]
        s = _mm_nt(q[:, hd * MEM_HD:(hd + 1) * MEM_HD], kh)
        e = jnp.exp(s - jnp.max(s, axis=-1, keepdims=True))
        l = jnp.sum(e, axis=-1, keepdims=True)
        heads.append(_mm(e.astype(BF16), vh) * (1.0 / l))
    om = jnp.concatenate(heads, axis=1).astype(BF16)
    out_ref[0] = h1 + _mm(om, wmo_ref[...])


def _mem_kv_kernel(mem_ref, g_ref, w_ref, kv_ref):
    kv_ref[0] = _mm(_rms(mem_ref[0], g_ref[...]).astype(BF16), w_ref[...]).astype(BF16)


def _ffn_kernel(h_ref, rows_ref, wg_ref, wu_ref, wd_ref, out_ref, *, final_norm):
    h = h_ref[0]
    xn = _rms(h, rows_ref[0:1, :]).astype(BF16)
    g = _mm(xn, wg_ref[...])
    u = _mm(xn, wu_ref[...])
    a = (g * (1.0 / (1.0 + jnp.exp(-g))) * u).astype(BF16)
    y = h + _mm(a, wd_ref[...])
    if final_norm:
        y = _rms(y, rows_ref[1:2, :])
    out_ref[0] = y


def _const_spec(shape):
    return pl.BlockSpec(shape, lambda *_: (0,) * len(shape), pipeline_mode=pl.Buffered(1))


def _swap_halves(w):
    half = w.shape[-1] // 2
    return jnp.concatenate([w[..., half:], w[..., :half]], axis=-1)


def _decay_constants():
    n_half = N_HEADS * HEAD_V
    sel = np.zeros((LANE, 2 * n_half), np.float32)
    ones = np.zeros((2 * n_half,), np.float32)
    for hd in range(N_HEADS):
        base = (hd // 2) * LANE + (HEAD_V if hd % 2 == 0 else 0)
        for part in range(N_SPLIT):
            src = part * N_HEADS + hd
            sel[src, base + part] = 1.0
            ones[base + N_SPLIT + part] = 1.0
            ones[n_half + base + part] = 1.0
            sel[src, n_half + base + N_SPLIT + part] = -1.0
    return sel, ones


def _rope_tables(seq):
    inv = 1.0 / (ROPE_THETA ** (jnp.arange(0, ROPE, 2, dtype=F32) / ROPE))
    ang = jnp.arange(seq, dtype=F32)[:, None] * inv[None, :]
    cos, sin = jnp.cos(ang), jnp.sin(ang)
    cc = jnp.concatenate([cos, cos], axis=1)
    ss = jnp.concatenate([-sin, sin], axis=1)
    ones = jnp.ones((seq, NOPE), F32)
    z_lo = jnp.zeros((seq, NOPE), F32)
    z_hi = jnp.zeros((seq, HEAD_PAD - NOPE - ROPE), F32)
    scale = LOG2E * (NOPE + ROPE) ** -0.5
    return jnp.concatenate([
        scale * jnp.concatenate([ones, cc, z_hi], axis=1),
        scale * jnp.concatenate([z_lo, ss, z_hi], axis=1),
        jnp.concatenate([z_lo, cc, z_hi], axis=1),
        jnp.concatenate([z_lo, ss, z_hi], axis=1)], axis=1)


def _layer_weights(l, w_in, w_uq, w_ukv, q_lora, kv_lora):
    d_model = w_in.shape[1]
    w = w_in[l]
    fox = N_HEADS * HEAD_V
    o = q_lora + kv_lora
    kr = w[:, o:o + ROPE]
    o += ROPE
    wfq, wfk, wfv = w[:, o:o + fox], w[:, o + fox:o + 2 * fox], w[:, o + 2 * fox:o + 3 * fox]
    wfl = w[:, o + 3 * fox:o + 3 * fox + N_HEADS]

    def place_rope(c):
        return jnp.pad(c, ((0, 0), (NOPE, HEAD_PAD - NOPE - ROPE)))

    gate3 = jnp.pad(jnp.concatenate([wfl] * N_SPLIT, axis=1), ((0, 0), (0, LANE - N_SPLIT * N_HEADS)))
    wx = jnp.concatenate([
        w[:, 0:q_lora + kv_lora], place_rope(kr), place_rope(_swap_halves(kr)), gate3,
        wfq * (LOG2E * HEAD_V ** -0.5), wfk, wfv], axis=1).astype(BF16)

    uq = w_uq[l].reshape(q_lora, N_HEADS, NOPE + ROPE)
    rope_cols = uq[:, :, NOPE:]
    pad_hi = jnp.zeros((q_lora, N_HEADS, HEAD_PAD - NOPE - ROPE), F32)
    wq_a = jnp.concatenate([uq[:, :, :NOPE], rope_cols, pad_hi], axis=2)
    wq_b = jnp.concatenate([jnp.zeros((q_lora, N_HEADS, NOPE), F32), _swap_halves(rope_cols), pad_hi], axis=2)
    wq = jnp.concatenate([wq_a.reshape(q_lora, -1), wq_b.reshape(q_lora, -1)], axis=1).astype(BF16)

    ukv = w_ukv[l].reshape(kv_lora, N_HEADS, NOPE + HEAD_V)
    wkv = jnp.concatenate([ukv[:, :, :NOPE].reshape(kv_lora, -1),
                           ukv[:, :, NOPE:].reshape(kv_lora, -1)], axis=1).astype(BF16)
    del d_model
    return wx, wq, wkv


def kernel(x, mem, mix_norm_g, w_in, cq_norm_g, ckv_norm_g, w_uq, w_ukv, forget_bias, mla_out_g, fox_out_g, w_out, mem_q_norm_g, mem_kv_norm_g, w_mq, w_mkv, w_mo, ffn_norm_g, w_gate, w_up, w_down, final_norm_g):
    bsz, seq, d_model = x.shape
    depth = w_in.shape[0]
    q_lora = cq_norm_g.shape[1]
    kv_lora = ckv_norm_g.shape[1]
    mem_len = mem.shape[1]
    d_ff = w_gate.shape[2]
    width = N_HEADS * HEAD_PAD
    n_v = 2 * N_HEADS * HEAD_V
    assert d_model == width == n_v, "layout assumes d_model = 8 heads * 128"
    assert seq % BQ == 0 and seq % TM_PROJ == 0 and TM_PROJ % BK == 0 and BQ % BK == 0

    tabs = _rope_tables(seq)
    sel_np, decay_ones = _decay_constants()
    sel = jnp.asarray(sel_np, BF16)
    tri = jnp.asarray(np.tril(np.ones((TM_PROJ, TM_PROJ), np.float32)), BF16)
    params = pltpu.CompilerParams

    h = x
    for l in range(depth):
        wx, wq, wkv = _layer_weights(l, w_in, w_uq, w_ukv, q_lora, kv_lora)
        bias3 = jnp.pad(jnp.concatenate([forget_bias[l]] * N_SPLIT), (0, LANE - N_SPLIT * N_HEADS))
        row1 = jnp.pad(jnp.concatenate([cq_norm_g[l], ckv_norm_g[l], bias3]),
                       (0, d_model - q_lora - kv_lora - LANE))
        rows1 = jnp.zeros((8, d_model), F32).at[0].set(mix_norm_g[l]).at[1].set(row1)
        rows1 = rows1.at[2].set(decay_ones)

        q_all, k_all, vt_all = pl.pallas_call(
            functools.partial(_proj_in_kernel, q_lora=q_lora, kv_lora=kv_lora),
            grid=(bsz, seq // TM_PROJ),
            in_specs=[
                pl.BlockSpec((1, TM_PROJ, d_model), lambda b, t: (b, t, 0)),
                _const_spec((8, d_model)),
                pl.BlockSpec((TM_PROJ, 4 * LANE), lambda b, t: (t, 0)),
                _const_spec(wx.shape), _const_spec(wq.shape), _const_spec(wkv.shape),
                _const_spec(sel.shape), _const_spec(tri.shape),
            ],
            out_specs=[
                pl.BlockSpec((1, TM_PROJ, 2 * width), lambda b, t: (b, t, 0)),
                pl.BlockSpec((1, TM_PROJ, 2 * width), lambda b, t: (b, t, 0)),
                pl.BlockSpec((1, TM_PROJ // BK, n_v, BK), lambda b, t: (b, t, 0, 0)),
            ],
            out_shape=[
                jax.ShapeDtypeStruct((bsz, seq, 2 * width), BF16),
                jax.ShapeDtypeStruct((bsz, seq, 2 * width), BF16),
                jax.ShapeDtypeStruct((bsz, seq // BK, n_v, BK), BF16),
            ],
            scratch_shapes=[pltpu.VMEM((1, LANE), F32)],
            compiler_params=params(dimension_semantics=("arbitrary", "arbitrary"),
                                   vmem_limit_bytes=VMEM_LIMIT),
            name=f"proj_in_{l}",
        )(h, rows1, tabs, wx, wq, wkv, sel, tri)

        o_all = pl.pallas_call(
            _attn_kernel,
            grid=(bsz, N_HEADS, seq // BQ),
            in_specs=[
                pl.BlockSpec((1, BQ, 2 * HEAD_PAD), lambda b, p, i: (b, i, p)),
                pl.BlockSpec((1, seq, 2 * HEAD_PAD), lambda b, p, i: (b, 0, p)),
                pl.BlockSpec((1, seq // BK, 2 * HEAD_V, BK), lambda b, p, i: (b, 0, p, 0)),
            ],
            out_specs=pl.BlockSpec((1, BQ, 2 * HEAD_V), lambda b, p, i: (b, i, p)),
            out_shape=jax.ShapeDtypeStruct((bsz, seq, n_v), F32),
            scratch_shapes=[
                pltpu.VMEM((BQ // BK, 2, BK, BQ), F32),
                pltpu.VMEM((BQ // BK, 2, BK, BQ), BF16),
                pltpu.VMEM((BQ // BK, 2, 1, BQ), F32),
                pltpu.VMEM((2, 1, BQ), F32),
                pltpu.VMEM((2, 1, BQ), F32),
                pltpu.VMEM((2, HEAD_V + ONES_ROWS, BQ), F32),
            ],
            compiler_params=params(dimension_semantics=("parallel", "parallel", "parallel"),
                                   vmem_limit_bytes=VMEM_LIMIT),
            name=f"attn_{l}",
        )(q_all, k_all, vt_all)

        kv_mem = pl.pallas_call(
            _mem_kv_kernel,
            grid=(bsz,),
            in_specs=[
                pl.BlockSpec((1, mem_len, d_model), lambda b: (b, 0, 0)),
                _const_spec((1, d_model)),
                _const_spec(w_mkv.shape[1:]),
            ],
            out_specs=pl.BlockSpec((1, mem_len, w_mkv.shape[2]), lambda b: (b, 0, 0)),
            out_shape=jax.ShapeDtypeStruct((bsz, mem_len, w_mkv.shape[2]), BF16),
            compiler_params=params(dimension_semantics=("parallel",), vmem_limit_bytes=VMEM_LIMIT),
            name=f"mem_kv_{l}",
        )(mem, mem_kv_norm_g[l][None, :], w_mkv[l].astype(BF16))

        rows3 = jnp.zeros((8, d_model), F32).at[0].set(jnp.concatenate([mla_out_g[l], fox_out_g[l]]))
        rows3 = rows3.at[1].set(mem_q_norm_g[l])
        h = pl.pallas_call(
            _post_attn_kernel,
            grid=(bsz, seq // TM_PROJ),
            in_specs=[
                pl.BlockSpec((1, TM_PROJ, n_v), lambda b, t: (b, t, 0)),
                pl.BlockSpec((1, TM_PROJ, d_model), lambda b, t: (b, t, 0)),
                _const_spec((8, d_model)),
                pl.BlockSpec((1, mem_len, w_mkv.shape[2]), lambda b, t: (b, 0, 0)),
                _const_spec(w_out.shape[1:]), _const_spec(w_mq.shape[1:]), _const_spec(w_mo.shape[1:]),
            ],
            out_specs=pl.BlockSpec((1, TM_PROJ, d_model), lambda b, t: (b, t, 0)),
            out_shape=jax.ShapeDtypeStruct((bsz, seq, d_model), F32),
            compiler_params=params(dimension_semantics=("parallel", "parallel"),
                                   vmem_limit_bytes=VMEM_LIMIT),
            name=f"post_attn_{l}",
        )(o_all, h, rows3, kv_mem, w_out[l].astype(BF16), w_mq[l].astype(BF16), w_mo[l].astype(BF16))

        last = l == depth - 1
        rows4 = jnp.zeros((8, d_model), F32).at[0].set(ffn_norm_g[l]).at[1].set(final_norm_g)
        h = pl.pallas_call(
            functools.partial(_ffn_kernel, final_norm=last),
            grid=(bsz, seq // TM_FFN),
            in_specs=[
                pl.BlockSpec((1, TM_FFN, d_model), lambda b, t: (b, t, 0)),
                _const_spec((8, d_model)),
                _const_spec((d_model, d_ff)), _const_spec((d_model, d_ff)), _const_spec((d_ff, d_model)),
            ],
            out_specs=pl.BlockSpec((1, TM_FFN, d_model), lambda b, t: (b, t, 0)),
            out_shape=jax.ShapeDtypeStruct((bsz, seq, d_model), F32),
            compiler_params=params(dimension_semantics=("parallel", "parallel"),
                                   vmem_limit_bytes=VMEM_LIMIT),
            name=f"ffn_{l}",
        )(h, rows4, w_gate[l].astype(BF16), w_up[l].astype(BF16), w_down[l].astype(BF16))
    return h
```

```python
import functools

import numpy as np
import jax
import jax.numpy as jnp
from jax import lax
from jax.experimental import pallas as pl
from jax.experimental.pallas import tpu as pltpu

F32 = jnp.float32
BF16 = jnp.bfloat16

EPS = 1e-6
ROPE_THETA = 10000.0
N_HEADS = 8
NOPE = 64
ROPE = 32
HEAD_V = 64
HEAD_PAD = 128
N_MEM_HEADS = 4
MEM_HD = 128
N_SPLIT = 3
ONES_ROWS = 16

LANE = 128
TM_PROJ = 512
TM_FFN = 512
BQ = 1024
BK = 256
MASKED = -2e30
M_INIT = -1e30
LOG2E = 1.4426950408889634
VMEM_LIMIT = 56 * 1024 * 1024

ROW_MIX_G, ROW_LATENT, ROW_DECAY_ONES, ROW_OUT_G, ROW_MEM_Q_G, ROW_FFN_G, ROW_FINAL_G, ROW_MEM_KV_G = range(8)


def _mm(a, b):
    return jnp.dot(a, b, preferred_element_type=F32)


def _mm_nt(a, b):
    return lax.dot_general(a, b, (((1,), (1,)), ((), ())), preferred_element_type=F32)


def _rms(x, g):
    return x * lax.rsqrt(jnp.mean(x * x, axis=-1, keepdims=True) + EPS) * g


def _split3(x):
    hi = x.astype(BF16).astype(F32)
    r = x - hi
    mid = r.astype(BF16).astype(F32)
    lo = (r - mid).astype(BF16).astype(F32)
    return hi, mid, lo


def _proj_in_kernel(h_ref, rows_ref, tabs_ref, wx_ref, wq_ref, wkv_ref, sel_ref, tri_ref,
                    q_ref, k_ref, vt_ref, carry_ref, *, q_lora, kv_lora):
    tm = h_ref.shape[1]
    n_sub = vt_ref.shape[1]
    bk = vt_ref.shape[3]
    width = N_HEADS * HEAD_PAD
    nv = N_HEADS * HEAD_V

    @pl.when(pl.program_id(1) == 0)
    def _():
        carry_ref[...] = jnp.zeros_like(carry_ref)

    xn = _rms(h_ref[0], rows_ref[ROW_MIX_G:ROW_MIX_G + 1, :]).astype(BF16)

    o_ckv = q_lora
    o_kr = o_ckv + kv_lora
    o_f3 = o_kr + 2 * LANE
    o_fq = o_f3 + LANE
    o_fk = o_fq + nv
    o_fv = o_fk + nv

    cos_q = tabs_ref[:, 0:LANE]
    sin_q = tabs_ref[:, LANE:2 * LANE]
    cos_k = tabs_ref[:, 2 * LANE:3 * LANE]
    sin_k = tabs_ref[:, 3 * LANE:4 * LANE]

    def put_values(v, row0):
        vt = v.T.astype(BF16)
        for c in range(n_sub):
            vt_ref[0, c, row0:row0 + nv, :] = vt[:, c * bk:(c + 1) * bk]

    lane = lax.broadcasted_iota(jnp.int32, (tm, LANE), 1)
    low = lane < HEAD_V

    def put_head_pairs(out_ref, col0, narrow, fill):
        for pair in range(N_HEADS // 2):
            blk = narrow[:, pair * LANE:(pair + 1) * LANE]
            other = fill(pair)
            c = col0 + 2 * pair * HEAD_PAD
            out_ref[0, :, c:c + HEAD_PAD] = jnp.where(low, blk, other).astype(BF16)
            odd = pltpu.roll(jnp.where(low, other, blk), HEAD_V, axis=1)
            out_ref[0, :, c + HEAD_PAD:c + 2 * HEAD_PAD] = odd.astype(BF16)

    cq = _mm(xn, wx_ref[:, 0:q_lora])
    ckv = _mm(xn, wx_ref[:, o_ckv:o_kr])
    o_bias = q_lora + kv_lora
    f3 = _mm(xn, wx_ref[:, o_f3:o_fq]) + rows_ref[ROW_LATENT:ROW_LATENT + 1, o_bias:o_bias + LANE]
    kr2 = _mm(xn, wx_ref[:, o_kr:o_f3])
    put_values(_mm(xn, wx_ref[:, o_fv:o_fv + nv]), nv)

    cqn = _rms(cq, rows_ref[ROW_LATENT:ROW_LATENT + 1, 0:q_lora]).astype(BF16)
    qa = _mm(cqn, wq_ref[...])
    first_half = lane < NOPE + ROPE // 2
    for hd in range(N_HEADS):
        sl = slice(hd * HEAD_PAD, (hd + 1) * HEAD_PAD)
        x = qa[:, sl]
        swapped = jnp.where(first_half, pltpu.roll(x, LANE - ROPE // 2, axis=1),
                            pltpu.roll(x, ROPE // 2, axis=1))
        q_ref[0, :, sl] = (x * cos_q + swapped * sin_q).astype(BF16)

    ckvn = _rms(ckv, rows_ref[ROW_LATENT:ROW_LATENT + 1, q_lora:q_lora + kv_lora]).astype(BF16)
    kn = _mm(ckvn, wkv_ref[:, 0:nv])
    kr = kr2[:, 0:LANE] * cos_k + kr2[:, LANE:2 * LANE] * sin_k
    kr_both = kr + pltpu.roll(kr, HEAD_V, axis=1)
    put_head_pairs(k_ref, 0, kn, lambda pair: kr_both)
    put_values(_mm(ckvn, wkv_ref[:, nv:2 * nv]), 0)

    live = lane < N_SPLIT * N_HEADS
    log_f = jnp.minimum(f3, 0.0) - jnp.log1p(jnp.exp(-jnp.abs(f3)))
    log_f = jnp.where(live, log_f, 0.0)
    pieces = jnp.concatenate([p.astype(BF16) for p in _split3(log_f)], axis=1)
    csum = _mm(tri_ref[...], pieces)
    fq = _mm(xn, wx_ref[:, o_fq:o_fk])
    cum = csum[:, 0:LANE] + csum[:, LANE:2 * LANE] + csum[:, 2 * LANE:3 * LANE] + carry_ref[...]
    carry_ref[...] = cum[tm - 1:tm, :]
    c_hi, c_mid, c_lo = _split3(cum * LOG2E)
    c_sel = jnp.where(lane < N_HEADS, c_hi, jnp.where(lane < 2 * N_HEADS, c_mid, c_lo))
    c_sel = jnp.where(live, c_sel, 0.0).astype(BF16)
    fk = _mm(xn, wx_ref[:, o_fk:o_fv])
    aug = _mm(c_sel, sel_ref[...])
    aug = aug + rows_ref[ROW_DECAY_ONES:ROW_DECAY_ONES + 1, :]
    put_head_pairs(q_ref, width, fq, lambda pair: aug[:, pair * LANE:(pair + 1) * LANE])
    put_head_pairs(k_ref, width, fk, lambda pair: aug[:, nv + pair * LANE:nv + (pair + 1) * LANE])


def _attn_kernel(q_ref, k_ref, vt_ref, o_ref, s_scr, p_scr, mt_scr, al_scr, m_scr, acc_scr, *, bq):
    bk = vt_ref.shape[3]
    n_sub = bq // bk
    nq = q_ref.shape[1] // bq
    heads = [slice(hh * HEAD_PAD, (hh + 1) * HEAD_PAD) for hh in range(2)]

    def k_tile(j):
        return k_ref[0, pl.ds(pl.multiple_of(j * bk, bk), bk), :]

    def q_rows(qi, first, hh):
        return q_ref[0, pl.ds(pl.multiple_of(qi * bq + first, bk), bq - first), heads[hh]]

    def put_scores(u, hh, s):
        s_scr[u, hh] = s
        mt_scr[u, hh] = jnp.max(s, axis=0, keepdims=True)

    def scores_group(g, qi):
        for u in range(n_sub):
            kb = k_tile(n_sub * g + u)
            for hh in range(2):
                put_scores(u, hh, _mm_nt(kb[:, heads[hh]], q_rows(qi, 0, hh)))

    def scores_diagonal(qi):
        tri = (lax.broadcasted_iota(jnp.int32, (bk, bk), 0)
               <= lax.broadcasted_iota(jnp.int32, (bk, bk), 1))
        for u in range(n_sub):
            kb = k_tile(n_sub * qi + u)
            for hh in range(2):
                s = _mm_nt(kb[:, heads[hh]], q_rows(qi, u * bk, hh))
                parts = [jnp.full((bk, u * bk), MASKED, F32)] if u > 0 else []
                parts.append(jnp.where(tri, s[:, :bk], MASKED))
                if u < n_sub - 1:
                    parts.append(s[:, bk:])
                put_scores(u, hh, jnp.concatenate(parts, axis=1) if len(parts) > 1 else parts[0])

    def softmax_group():
        for hh in range(2):
            mt = mt_scr[0, hh]
            for u in range(1, n_sub):
                mt = jnp.maximum(mt, mt_scr[u, hh])
            m_new = jnp.maximum(m_scr[hh], mt)
            al_scr[hh] = jnp.exp2(m_scr[hh] - m_new)
            m_scr[hh] = m_new
            for u in range(n_sub):
                p_scr[u, hh] = jnp.exp2(s_scr[u, hh] - m_new).astype(BF16)

    def values_at(j0):
        ones = jnp.ones((ONES_ROWS, bk), BF16)
        for hh in range(2):
            pv = None
            for u in range(n_sub):
                vb = vt_ref[0, j0 + u]
                v = jnp.concatenate([vb[hh * HEAD_V:(hh + 1) * HEAD_V, :], ones], axis=0)
                d = _mm(v, p_scr[u, hh])
                pv = d if pv is None else pv + d
            acc_scr[hh] = al_scr[hh] * acc_scr[hh] + pv

    def last_group_start(qi):
        return jnp.where(qi == 0, 0, n_sub * (qi - 1))

    def reset():
        m_scr[...] = jnp.full_like(m_scr, M_INIT)
        acc_scr[...] = jnp.zeros_like(acc_scr)

    def finish(qi):
        ot = jnp.concatenate(
            [acc_scr[hh, 0:HEAD_V] * (1.0 / acc_scr[hh, HEAD_V:HEAD_V + 1]) for hh in range(2)], axis=0)
        o_ref[0, pl.ds(pl.multiple_of(qi * bq, bq), bq), :] = ot.T
        reset()

    def below_diagonal(qi):
        def body(g, carry):
            softmax_group()
            scores_group(g, qi)
            values_at(jnp.where(g == 0, n_sub * qi, n_sub * (g - 1)))
            return carry

        lax.fori_loop(0, qi, body, 0)

    reset()
    scores_diagonal(0)

    def block(qi, carry):
        softmax_group()
        scores_diagonal(qi)
        values_at(last_group_start(qi - 1))
        finish(qi - 1)
        below_diagonal(qi)
        return carry

    lax.fori_loop(1, nq, block, 0)
    softmax_group()
    values_at(last_group_start(nq - 1))
    finish(nq - 1)


def _post_attn_kernel(o_ref, h_ref, rows_ref, kv_ref, wo_ref, wmq_ref, wmo_ref, out_ref):
    half = o_ref.shape[2] // 2
    o = o_ref[0]
    merged = jnp.concatenate(
        [_rms(o[:, 0:half], rows_ref[ROW_OUT_G:ROW_OUT_G + 1, 0:half]),
         _rms(o[:, half:], rows_ref[ROW_OUT_G:ROW_OUT_G + 1, half:])],
        axis=1).astype(BF16)
    h1 = h_ref[0] + _mm(merged, wo_ref[...])
    xn = _rms(h1, rows_ref[ROW_MEM_Q_G:ROW_MEM_Q_G + 1, :]).astype(BF16)
    q = (_mm(xn, wmq_ref[...]) * (MEM_HD ** -0.5)).astype(BF16)
    kv = kv_ref[0]
    heads = []
    for hd in range(N_MEM_HEADS):
        kh = kv[:, 2 * hd * MEM_HD:(2 * hd + 1) * MEM_HD]
        vh = kv[:, (2 * hd + 1) * MEM_HD:(2 * hd + 2) * MEM_HD]
        s = _mm_nt(q[:, hd * MEM_HD:(hd + 1) * MEM_HD], kh)
        e = jnp.exp(s - jnp.max(s, axis=-1, keepdims=True))
        l = jnp.sum(e, axis=-1, keepdims=True)
        heads.append(_mm(e.astype(BF16), vh) * (1.0 / l))
    om = jnp.concatenate(heads, axis=1).astype(BF16)
    out_ref[0] = h1 + _mm(om, wmo_ref[...])


def _mem_kv_kernel(mem_ref, rows_ref, w_ref, kv_ref):
    g = rows_ref[ROW_MEM_KV_G:ROW_MEM_KV_G + 1, :]
    kv_ref[0] = _mm(_rms(mem_ref[0], g).astype(BF16), w_ref[...]).astype(BF16)


def _ffn_kernel(h_ref, rows_ref, wg_ref, wu_ref, wd_ref, out_ref, *, final_norm):
    h = h_ref[0]
    xn = _rms(h, rows_ref[ROW_FFN_G:ROW_FFN_G + 1, :]).astype(BF16)
    g = _mm(xn, wg_ref[...])
    u = _mm(xn, wu_ref[...])
    a = (g * (1.0 / (1.0 + jnp.exp(-g))) * u).astype(BF16)
    y = h + _mm(a, wd_ref[...])
    if final_norm:
        y = _rms(y, rows_ref[ROW_FINAL_G:ROW_FINAL_G + 1, :])
    out_ref[0] = y


def _const_spec(shape):
    return pl.BlockSpec(shape, lambda *_: (0,) * len(shape), pipeline_mode=pl.Buffered(1))


def _layer_spec(shape, l):
    return pl.BlockSpec((None,) + tuple(shape), lambda *_: (l,) + (0,) * len(shape),
                        pipeline_mode=pl.Buffered(1))


def _swap_halves(w):
    half = w.shape[-1] // 2
    return jnp.concatenate([w[..., half:], w[..., :half]], axis=-1)


def _decay_constants():
    n_half = N_HEADS * HEAD_V
    sel = np.zeros((LANE, 2 * n_half), np.float32)
    ones = np.zeros((2 * n_half,), np.float32)
    for hd in range(N_HEADS):
        base = (hd // 2) * LANE + (HEAD_V if hd % 2 == 0 else 0)
        for part in range(N_SPLIT):
            src = part * N_HEADS + hd
            sel[src, base + part] = 1.0
            ones[base + N_SPLIT + part] = 1.0
            ones[n_half + base + part] = 1.0
            sel[src, n_half + base + N_SPLIT + part] = -1.0
    return sel, ones


def _rope_tables(seq):
    inv = 1.0 / (ROPE_THETA ** (jnp.arange(0, ROPE, 2, dtype=F32) / ROPE))
    ang = jnp.arange(seq, dtype=F32)[:, None] * inv[None, :]
    cos, sin = jnp.cos(ang), jnp.sin(ang)
    cc = jnp.concatenate([cos, cos], axis=1)
    ss = jnp.concatenate([-sin, sin], axis=1)
    ones = jnp.ones((seq, NOPE), F32)
    z_lo = jnp.zeros((seq, NOPE), F32)
    z_hi = jnp.zeros((seq, HEAD_PAD - NOPE - ROPE), F32)
    scale = LOG2E * (NOPE + ROPE) ** -0.5
    return jnp.concatenate([
        scale * jnp.concatenate([ones, cc, z_hi], axis=1),
        scale * jnp.concatenate([z_lo, ss, z_hi], axis=1),
        jnp.concatenate([z_lo, cc, z_hi], axis=1),
        jnp.concatenate([z_lo, ss, z_hi], axis=1)], axis=1)


def _mixer_weights(w_in, w_uq, w_ukv, q_lora, kv_lora):
    depth = w_in.shape[0]
    fox = N_HEADS * HEAD_V
    o = q_lora + kv_lora
    kr = w_in[:, :, o:o + ROPE]
    o += ROPE
    wfq, wfk, wfv = w_in[:, :, o:o + fox], w_in[:, :, o + fox:o + 2 * fox], w_in[:, :, o + 2 * fox:o + 3 * fox]
    wfl = w_in[:, :, o + 3 * fox:o + 3 * fox + N_HEADS]

    def place_rope(c):
        return jnp.pad(c, ((0, 0), (0, 0), (NOPE, HEAD_PAD - NOPE - ROPE)))

    gate3 = jnp.pad(jnp.concatenate([wfl] * N_SPLIT, axis=2), ((0, 0), (0, 0), (0, LANE - N_SPLIT * N_HEADS)))
    wx = jnp.concatenate([
        w_in[:, :, 0:q_lora + kv_lora], place_rope(kr), place_rope(_swap_halves(kr)), gate3,
        wfq * (LOG2E * HEAD_V ** -0.5), wfk, wfv], axis=2).astype(BF16)

    uq = w_uq.reshape(depth, q_lora, N_HEADS, NOPE + ROPE)
    pad_hi = jnp.zeros((depth, q_lora, N_HEADS, HEAD_PAD - NOPE - ROPE), F32)
    wq = jnp.concatenate([uq, pad_hi], axis=3).reshape(depth, q_lora, -1).astype(BF16)

    ukv = w_ukv.reshape(depth, kv_lora, N_HEADS, NOPE + HEAD_V)
    wkv = jnp.concatenate([ukv[..., :NOPE].reshape(depth, kv_lora, -1),
                           ukv[..., NOPE:].reshape(depth, kv_lora, -1)], axis=2).astype(BF16)
    return wx, wq, wkv


def kernel(x, mem, mix_norm_g, w_in, cq_norm_g, ckv_norm_g, w_uq, w_ukv, forget_bias, mla_out_g, fox_out_g, w_out, mem_q_norm_g, mem_kv_norm_g, w_mq, w_mkv, w_mo, ffn_norm_g, w_gate, w_up, w_down, final_norm_g):
    bsz, seq, d_model = x.shape
    depth = w_in.shape[0]
    q_lora = cq_norm_g.shape[1]
    kv_lora = ckv_norm_g.shape[1]
    mem_len = mem.shape[1]
    d_ff = w_gate.shape[2]
    width = N_HEADS * HEAD_PAD
    n_v = 2 * N_HEADS * HEAD_V
    assert d_model == width == n_v, "layout assumes d_model = 8 heads * 128"
    assert seq % BQ == 0 and seq % TM_PROJ == 0 and TM_PROJ % BK == 0 and BQ % BK == 0

    tabs = _rope_tables(seq)
    sel_np, decay_ones = _decay_constants()
    sel = jnp.asarray(sel_np, BF16)
    tri = jnp.asarray(np.tril(np.ones((TM_PROJ, TM_PROJ), np.float32)), BF16)
    params = pltpu.CompilerParams

    wx, wq, wkv = _mixer_weights(w_in, w_uq, w_ukv, q_lora, kv_lora)
    w_out_b, w_mq_b, w_mkv_b, w_mo_b = (w.astype(BF16) for w in (w_out, w_mq, w_mkv, w_mo))
    w_gate_b, w_up_b, w_down_b = (w.astype(BF16) for w in (w_gate, w_up, w_down))
    latent = jnp.pad(jnp.concatenate([cq_norm_g, ckv_norm_g] + [forget_bias] * N_SPLIT, axis=1),
                     ((0, 0), (0, d_model - q_lora - kv_lora - N_SPLIT * N_HEADS)))
    per_layer = lambda v: jnp.broadcast_to(v, (depth, d_model))
    rows = jnp.stack([mix_norm_g, latent, per_layer(jnp.asarray(decay_ones)),
                      jnp.concatenate([mla_out_g, fox_out_g], axis=1), mem_q_norm_g, ffn_norm_g,
                      per_layer(final_norm_g), mem_kv_norm_g], axis=1)
    rows_spec = lambda l: _layer_spec((8, d_model), l)

    h = x
    for l in range(depth):
        q_all, k_all, vt_all = pl.pallas_call(
            functools.partial(_proj_in_kernel, q_lora=q_lora, kv_lora=kv_lora),
            grid=(bsz, seq // TM_PROJ),
            in_specs=[
                pl.BlockSpec((1, TM_PROJ, d_model), lambda b, t: (b, t, 0)),
                rows_spec(l),
                pl.BlockSpec((TM_PROJ, 4 * LANE), lambda b, t: (t, 0)),
                _layer_spec(wx.shape[1:], l), _layer_spec(wq.shape[1:], l), _layer_spec(wkv.shape[1:], l),
                _const_spec(sel.shape), _const_spec(tri.shape),
            ],
            out_specs=[
                pl.BlockSpec((1, TM_PROJ, 2 * width), lambda b, t: (b, t, 0)),
                pl.BlockSpec((1, TM_PROJ, 2 * width), lambda b, t: (b, t, 0)),
                pl.BlockSpec((1, TM_PROJ // BK, n_v, BK), lambda b, t: (b, t, 0, 0)),
            ],
            out_shape=[
                jax.ShapeDtypeStruct((bsz, seq, 2 * width), BF16),
                jax.ShapeDtypeStruct((bsz, seq, 2 * width), BF16),
                jax.ShapeDtypeStruct((bsz, seq // BK, n_v, BK), BF16),
            ],
            scratch_shapes=[pltpu.VMEM((1, LANE), F32)],
            compiler_params=params(dimension_semantics=("arbitrary", "arbitrary"),
                                   vmem_limit_bytes=VMEM_LIMIT),
            name=f"proj_in_{l}",
        )(h, rows, tabs, wx, wq, wkv, sel, tri)

        o_all = pl.pallas_call(
            functools.partial(_attn_kernel, bq=BQ),
            grid=(bsz, N_HEADS),
            in_specs=[
                pl.BlockSpec((1, seq, 2 * HEAD_PAD), lambda b, p: (b, 0, p)),
                pl.BlockSpec((1, seq, 2 * HEAD_PAD), lambda b, p: (b, 0, p)),
                pl.BlockSpec((1, seq // BK, 2 * HEAD_V, BK), lambda b, p: (b, 0, p, 0)),
            ],
            out_specs=pl.BlockSpec((1, seq, 2 * HEAD_V), lambda b, p: (b, 0, p)),
            out_shape=jax.ShapeDtypeStruct((bsz, seq, n_v), F32),
            scratch_shapes=[
                pltpu.VMEM((BQ // BK, 2, BK, BQ), F32),
                pltpu.VMEM((BQ // BK, 2, BK, BQ), BF16),
                pltpu.VMEM((BQ // BK, 2, 1, BQ), F32),
                pltpu.VMEM((2, 1, BQ), F32),
                pltpu.VMEM((2, 1, BQ), F32),
                pltpu.VMEM((2, HEAD_V + ONES_ROWS, BQ), F32),
            ],
            compiler_params=params(dimension_semantics=("parallel", "parallel"),
                                   vmem_limit_bytes=VMEM_LIMIT),
            name=f"attn_{l}",
        )(q_all, k_all, vt_all)

        kv_mem = pl.pallas_call(
            _mem_kv_kernel,
            grid=(bsz,),
            in_specs=[
                pl.BlockSpec((1, mem_len, d_model), lambda b: (b, 0, 0)),
                rows_spec(l),
                _layer_spec(w_mkv.shape[1:], l),
            ],
            out_specs=pl.BlockSpec((1, mem_len, w_mkv.shape[2]), lambda b: (b, 0, 0)),
            out_shape=jax.ShapeDtypeStruct((bsz, mem_len, w_mkv.shape[2]), BF16),
            compiler_params=params(dimension_semantics=("parallel",), vmem_limit_bytes=VMEM_LIMIT),
            name=f"mem_kv_{l}",
        )(mem, rows, w_mkv_b)

        h = pl.pallas_call(
            _post_attn_kernel,
            grid=(bsz, seq // TM_PROJ),
            in_specs=[
                pl.BlockSpec((1, TM_PROJ, n_v), lambda b, t: (b, t, 0)),
                pl.BlockSpec((1, TM_PROJ, d_model), lambda b, t: (b, t, 0)),
                rows_spec(l),
                pl.BlockSpec((1, mem_len, w_mkv.shape[2]), lambda b, t: (b, 0, 0)),
                _layer_spec(w_out.shape[1:], l), _layer_spec(w_mq.shape[1:], l), _layer_spec(w_mo.shape[1:], l),
            ],
            out_specs=pl.BlockSpec((1, TM_PROJ, d_model), lambda b, t: (b, t, 0)),
            out_shape=jax.ShapeDtypeStruct((bsz, seq, d_model), F32),
            compiler_params=params(dimension_semantics=("parallel", "parallel"),
                                   vmem_limit_bytes=VMEM_LIMIT),
            name=f"post_attn_{l}",
        )(o_all, h, rows, kv_mem, w_out_b, w_mq_b, w_mo_b)

        last = l == depth - 1
        h = pl.pallas_call(
            functools.partial(_ffn_kernel, final_norm=last),
            grid=(bsz, seq // TM_FFN),
            in_specs=[
                pl.BlockSpec((1, TM_FFN, d_model), lambda b, t: (b, t, 0)),
                rows_spec(l),
                _layer_spec((d_model, d_ff), l), _layer_spec((d_model, d_ff), l), _layer_spec((d_ff, d_model), l),
            ],
            out_specs=pl.BlockSpec((1, TM_FFN, d_model), lambda b, t: (b, t, 0)),
            out_shape=jax.ShapeDtypeStruct((bsz, seq, d_model), F32),
            compiler_params=params(dimension_semantics=("parallel", "parallel"),
                                   vmem_limit_bytes=VMEM_LIMIT),
            name=f"ffn_{l}",
        )(h, rows, w_gate_b, w_up_b, w_down_b)
    return h
```

```python
import functools

import numpy as np
import jax
import jax.numpy as jnp
from jax import lax
from jax.experimental import pallas as pl
from jax.experimental.pallas import tpu as pltpu

F32 = jnp.float32
BF16 = jnp.bfloat16

EPS = 1e-6
ROPE_THETA = 10000.0
N_HEADS = 8
NOPE = 64
ROPE = 32
HEAD_V = 64
HEAD_PAD = 128
N_MEM_HEADS = 4
MEM_HD = 128
N_SPLIT = 3
ONES_ROWS = 16

LANE = 128
TM_PROJ = 512
TM_FFN = 512
BQ = 1024
BK = 256
MASKED = -2e30
M_INIT = -1e30
LOG2E = 1.4426950408889634
VMEM_LIMIT = 56 * 1024 * 1024

ROW_MIX_G, ROW_LATENT, ROW_DECAY_ONES, ROW_OUT_G, ROW_MEM_Q_G, ROW_FFN_G, ROW_FINAL_G, ROW_MEM_KV_G = range(8)


def _mm(a, b):
    return jnp.dot(a, b, preferred_element_type=F32)


def _mm_nt(a, b):
    return lax.dot_general(a, b, (((1,), (1,)), ((), ())), preferred_element_type=F32)


def _rms(x, g):
    return x * lax.rsqrt(jnp.mean(x * x, axis=-1, keepdims=True) + EPS) * g


def _split3(x):
    hi = x.astype(BF16).astype(F32)
    r = x - hi
    mid = r.astype(BF16).astype(F32)
    lo = (r - mid).astype(BF16).astype(F32)
    return hi, mid, lo


def _proj_in_kernel(h_ref, rows_ref, tabs_ref, wx_ref, wq_ref, wkv_ref, sel_ref, tri_ref,
                    q_ref, k_ref, vt_ref, carry_ref, *, q_lora, kv_lora):
    tm = h_ref.shape[1]
    n_sub = vt_ref.shape[1]
    bk = vt_ref.shape[3]
    width = N_HEADS * HEAD_PAD
    nv = N_HEADS * HEAD_V

    @pl.when(pl.program_id(1) == 0)
    def _():
        carry_ref[...] = jnp.zeros_like(carry_ref)

    xn = _rms(h_ref[0], rows_ref[ROW_MIX_G:ROW_MIX_G + 1, :]).astype(BF16)

    o_ckv = q_lora
    o_kr = o_ckv + kv_lora
    o_f3 = o_kr + 2 * LANE
    o_fq = o_f3 + LANE
    o_fk = o_fq + nv
    o_fv = o_fk + nv

    cos_q = tabs_ref[:, 0:LANE]
    sin_q = tabs_ref[:, LANE:2 * LANE]
    cos_k = tabs_ref[:, 2 * LANE:3 * LANE]
    sin_k = tabs_ref[:, 3 * LANE:4 * LANE]

    def put_values(v, row0):
        vt = v.T.astype(BF16)
        for c in range(n_sub):
            vt_ref[0, c, row0:row0 + nv, :] = vt[:, c * bk:(c + 1) * bk]

    lane = lax.broadcasted_iota(jnp.int32, (tm, LANE), 1)
    low = lane < HEAD_V

    def put_head_pairs(out_ref, col0, narrow, fill):
        for pair in range(N_HEADS // 2):
            blk = narrow[:, pair * LANE:(pair + 1) * LANE]
            other = fill(pair)
            c = col0 + 2 * pair * HEAD_PAD
            out_ref[0, :, c:c + HEAD_PAD] = jnp.where(low, blk, other).astype(BF16)
            odd = pltpu.roll(jnp.where(low, other, blk), HEAD_V, axis=1)
            out_ref[0, :, c + HEAD_PAD:c + 2 * HEAD_PAD] = odd.astype(BF16)

    cq = _mm(xn, wx_ref[:, 0:q_lora])
    ckv = _mm(xn, wx_ref[:, o_ckv:o_kr])
    o_bias = q_lora + kv_lora
    f3 = _mm(xn, wx_ref[:, o_f3:o_fq]) + rows_ref[ROW_LATENT:ROW_LATENT + 1, o_bias:o_bias + LANE]
    kr2 = _mm(xn, wx_ref[:, o_kr:o_f3])
    put_values(_mm(xn, wx_ref[:, o_fv:o_fv + nv]), nv)

    cqn = _rms(cq, rows_ref[ROW_LATENT:ROW_LATENT + 1, 0:q_lora]).astype(BF16)
    qa = _mm(cqn, wq_ref[...])
    first_half = lane < NOPE + ROPE // 2
    for hd in range(N_HEADS):
        sl = slice(hd * HEAD_PAD, (hd + 1) * HEAD_PAD)
        x = qa[:, sl]
        swapped = jnp.where(first_half, pltpu.roll(x, LANE - ROPE // 2, axis=1),
                            pltpu.roll(x, ROPE // 2, axis=1))
        q_ref[0, :, sl] = (x * cos_q + swapped * sin_q).astype(BF16)

    ckvn = _rms(ckv, rows_ref[ROW_LATENT:ROW_LATENT + 1, q_lora:q_lora + kv_lora]).astype(BF16)
    kn = _mm(ckvn, wkv_ref[:, 0:nv])
    kr = kr2[:, 0:LANE] * cos_k + kr2[:, LANE:2 * LANE] * sin_k
    kr_both = kr + pltpu.roll(kr, HEAD_V, axis=1)
    put_head_pairs(k_ref, 0, kn, lambda pair: kr_both)
    put_values(_mm(ckvn, wkv_ref[:, nv:2 * nv]), 0)

    live = lane < N_SPLIT * N_HEADS
    log_f = jnp.minimum(f3, 0.0) - jnp.log1p(jnp.exp(-jnp.abs(f3)))
    log_f = jnp.where(live, log_f, 0.0)
    pieces = jnp.concatenate([p.astype(BF16) for p in _split3(log_f)], axis=1)
    csum = _mm(tri_ref[...], pieces)
    fq = _mm(xn, wx_ref[:, o_fq:o_fk])
    cum = csum[:, 0:LANE] + csum[:, LANE:2 * LANE] + csum[:, 2 * LANE:3 * LANE] + carry_ref[...]
    carry_ref[...] = cum[tm - 1:tm, :]
    c_hi, c_mid, c_lo = _split3(cum * LOG2E)
    c_sel = jnp.where(lane < N_HEADS, c_hi, jnp.where(lane < 2 * N_HEADS, c_mid, c_lo))
    c_sel = jnp.where(live, c_sel, 0.0).astype(BF16)
    fk = _mm(xn, wx_ref[:, o_fk:o_fv])
    aug = _mm(c_sel, sel_ref[...])
    aug = aug + rows_ref[ROW_DECAY_ONES:ROW_DECAY_ONES + 1, :]
    put_head_pairs(q_ref, width, fq, lambda pair: aug[:, pair * LANE:(pair + 1) * LANE])
    put_head_pairs(k_ref, width, fk, lambda pair: aug[:, nv + pair * LANE:nv + (pair + 1) * LANE])


def _attn_kernel(q_ref, k_ref, vt_ref, o_ref, s_scr, p_scr, mt_scr, al_scr, m_scr, acc_scr, *, bq):
    bk = vt_ref.shape[3]
    n_sub = bq // bk
    nq = q_ref.shape[1] // bq
    heads = [slice(hh * HEAD_PAD, (hh + 1) * HEAD_PAD) for hh in range(2)]

    def k_tile(j):
        return k_ref[0, pl.ds(pl.multiple_of(j * bk, bk), bk), :]

    def q_rows(qi, first, hh):
        return q_ref[0, pl.ds(pl.multiple_of(qi * bq + first, bk), bq - first), heads[hh]]

    def put_scores(u, hh, s):
        s_scr[u, hh, :, 0:bq] = s
        mt_scr[u, hh] = jnp.max(s, axis=0, keepdims=True)

    def scores_group(g, qi):
        for u in range(n_sub):
            kb = k_tile(n_sub * g + u)
            for hh in range(2):
                put_scores(u, hh, _mm_nt(kb[:, heads[hh]], q_rows(qi, 0, hh)))

    def scores_diagonal(qi):
        tri = (lax.broadcasted_iota(jnp.int32, (bk, bk), 0)
               <= lax.broadcasted_iota(jnp.int32, (bk, bk), 1))
        for u in range(n_sub):
            kb = k_tile(n_sub * qi + u)
            for hh in range(2):
                s = _mm_nt(kb[:, heads[hh]], q_rows(qi, u * bk, hh))
                parts = [jnp.full((bk, u * bk), MASKED, F32)] if u > 0 else []
                parts.append(jnp.where(tri, s[:, :bk], MASKED))
                if u < n_sub - 1:
                    parts.append(s[:, bk:])
                put_scores(u, hh, jnp.concatenate(parts, axis=1) if len(parts) > 1 else parts[0])

    def softmax_group():
        for hh in range(2):
            mt = mt_scr[0, hh]
            for u in range(1, n_sub):
                mt = jnp.maximum(mt, mt_scr[u, hh])
            m_new = jnp.maximum(m_scr[hh], mt)
            al_scr[hh] = jnp.exp2(m_scr[hh] - m_new)
            m_scr[hh] = m_new
            for u in range(n_sub):
                p_scr[u, hh, :, 0:bq] = jnp.exp2(s_scr[u, hh, :, 0:bq] - m_new).astype(BF16)

    def values_at(j0):
        ones = jnp.ones((ONES_ROWS, bk), BF16)
        for hh in range(2):
            pv = None
            for u in range(n_sub):
                vb = vt_ref[0, j0 + u]
                v = jnp.concatenate([vb[hh * HEAD_V:(hh + 1) * HEAD_V, :], ones], axis=0)
                d = _mm(v, p_scr[u, hh, :, 0:bq])
                pv = d if pv is None else pv + d
            acc_scr[hh] = al_scr[hh] * acc_scr[hh] + pv

    def last_group_start(qi):
        return jnp.where(qi == 0, 0, n_sub * (qi - 1))

    def reset():
        m_scr[...] = jnp.full_like(m_scr, M_INIT)
        acc_scr[...] = jnp.zeros_like(acc_scr)

    def finish(qi):
        ot = jnp.concatenate(
            [acc_scr[hh, 0:HEAD_V] * (1.0 / acc_scr[hh, HEAD_V:HEAD_V + 1]) for hh in range(2)], axis=0)
        o_ref[0, pl.ds(pl.multiple_of(qi * bq, bq), bq), :] = ot.T
        reset()

    def below_diagonal(qi):
        def body(g, carry):
            softmax_group()
            scores_group(g, qi)
            values_at(jnp.where(g == 0, n_sub * qi, n_sub * (g - 1)))
            return carry

        lax.fori_loop(0, qi, body, 0)

    reset()
    scores_diagonal(0)

    def block(qi, carry):
        softmax_group()
        scores_diagonal(qi)
        values_at(last_group_start(qi - 1))
        finish(qi - 1)
        below_diagonal(qi)
        return carry

    lax.fori_loop(1, nq, block, 0)
    softmax_group()
    values_at(last_group_start(nq - 1))
    finish(nq - 1)


def _post_attn_kernel(o_ref, h_ref, rows_ref, kv_ref, wo_ref, wmq_ref, wmo_ref, out_ref):
    half = o_ref.shape[2] // 2
    o = o_ref[0]
    merged = jnp.concatenate(
        [_rms(o[:, 0:half], rows_ref[ROW_OUT_G:ROW_OUT_G + 1, 0:half]),
         _rms(o[:, half:], rows_ref[ROW_OUT_G:ROW_OUT_G + 1, half:])],
        axis=1).astype(BF16)
    h1 = h_ref[0] + _mm(merged, wo_ref[...])
    xn = _rms(h1, rows_ref[ROW_MEM_Q_G:ROW_MEM_Q_G + 1, :]).astype(BF16)
    q = (_mm(xn, wmq_ref[...]) * (MEM_HD ** -0.5)).astype(BF16)
    kv = kv_ref[0]
    heads = []
    for hd in range(N_MEM_HEADS):
        kh = kv[:, 2 * hd * MEM_HD:(2 * hd + 1) * MEM_HD]
        vh = kv[:, (2 * hd + 1) * MEM_HD:(2 * hd + 2) * MEM_HD]
        s = _mm_nt(q[:, hd * MEM_HD:(hd + 1) * MEM_HD], kh)
        e = jnp.exp(s - jnp.max(s, axis=-1, keepdims=True))
        l = jnp.sum(e, axis=-1, keepdims=True)
        heads.append(_mm(e.astype(BF16), vh) * (1.0 / l))
    om = jnp.concatenate(heads, axis=1).astype(BF16)
    out_ref[0] = h1 + _mm(om, wmo_ref[...])


def _mem_kv_kernel(mem_ref, rows_ref, w_ref, kv_ref):
    g = rows_ref[ROW_MEM_KV_G:ROW_MEM_KV_G + 1, :]
    kv_ref[0] = _mm(_rms(mem_ref[0], g).astype(BF16), w_ref[...]).astype(BF16)


def _ffn_kernel(h_ref, rows_ref, wg_ref, wu_ref, wd_ref, out_ref, *, final_norm):
    h = h_ref[0]
    xn = _rms(h, rows_ref[ROW_FFN_G:ROW_FFN_G + 1, :]).astype(BF16)
    g = _mm(xn, wg_ref[...])
    u = _mm(xn, wu_ref[...])
    a = (g * (1.0 / (1.0 + jnp.exp(-g))) * u).astype(BF16)
    y = h + _mm(a, wd_ref[...])
    if final_norm:
        y = _rms(y, rows_ref[ROW_FINAL_G:ROW_FINAL_G + 1, :])
    out_ref[0] = y


def _const_spec(shape):
    return pl.BlockSpec(shape, lambda *_: (0,) * len(shape), pipeline_mode=pl.Buffered(1))


def _layer_spec(shape, l):
    return pl.BlockSpec((None,) + tuple(shape), lambda *_: (l,) + (0,) * len(shape),
                        pipeline_mode=pl.Buffered(1))


def _swap_halves(w):
    half = w.shape[-1] // 2
    return jnp.concatenate([w[..., half:], w[..., :half]], axis=-1)


def _decay_constants():
    n_half = N_HEADS * HEAD_V
    sel = np.zeros((LANE, 2 * n_half), np.float32)
    ones = np.zeros((2 * n_half,), np.float32)
    for hd in range(N_HEADS):
        base = (hd // 2) * LANE + (HEAD_V if hd % 2 == 0 else 0)
        for part in range(N_SPLIT):
            src = part * N_HEADS + hd
            sel[src, base + part] = 1.0
            ones[base + N_SPLIT + part] = 1.0
            ones[n_half + base + part] = 1.0
            sel[src, n_half + base + N_SPLIT + part] = -1.0
    return sel, ones


def _rope_tables(seq):
    inv = 1.0 / (ROPE_THETA ** (jnp.arange(0, ROPE, 2, dtype=F32) / ROPE))
    ang = jnp.arange(seq, dtype=F32)[:, None] * inv[None, :]
    cos, sin = jnp.cos(ang), jnp.sin(ang)
    cc = jnp.concatenate([cos, cos], axis=1)
    ss = jnp.concatenate([-sin, sin], axis=1)
    ones = jnp.ones((seq, NOPE), F32)
    z_lo = jnp.zeros((seq, NOPE), F32)
    z_hi = jnp.zeros((seq, HEAD_PAD - NOPE - ROPE), F32)
    scale = LOG2E * (NOPE + ROPE) ** -0.5
    return jnp.concatenate([
        scale * jnp.concatenate([ones, cc, z_hi], axis=1),
        scale * jnp.concatenate([z_lo, ss, z_hi], axis=1),
        jnp.concatenate([z_lo, cc, z_hi], axis=1),
        jnp.concatenate([z_lo, ss, z_hi], axis=1)], axis=1)


def _mixer_weights(w_in, w_uq, w_ukv, q_lora, kv_lora):
    depth = w_in.shape[0]
    fox = N_HEADS * HEAD_V
    o = q_lora + kv_lora
    kr = w_in[:, :, o:o + ROPE]
    o += ROPE
    wfq, wfk, wfv = w_in[:, :, o:o + fox], w_in[:, :, o + fox:o + 2 * fox], w_in[:, :, o + 2 * fox:o + 3 * fox]
    wfl = w_in[:, :, o + 3 * fox:o + 3 * fox + N_HEADS]

    def place_rope(c):
        return jnp.pad(c, ((0, 0), (0, 0), (NOPE, HEAD_PAD - NOPE - ROPE)))

    gate3 = jnp.pad(jnp.concatenate([wfl] * N_SPLIT, axis=2), ((0, 0), (0, 0), (0, LANE - N_SPLIT * N_HEADS)))
    wx = jnp.concatenate([
        w_in[:, :, 0:q_lora + kv_lora], place_rope(kr), place_rope(_swap_halves(kr)), gate3,
        wfq * (LOG2E * HEAD_V ** -0.5), wfk, wfv], axis=2).astype(BF16)

    uq = w_uq.reshape(depth, q_lora, N_HEADS, NOPE + ROPE)
    pad_hi = jnp.zeros((depth, q_lora, N_HEADS, HEAD_PAD - NOPE - ROPE), F32)
    wq = jnp.concatenate([uq, pad_hi], axis=3).reshape(depth, q_lora, -1).astype(BF16)

    ukv = w_ukv.reshape(depth, kv_lora, N_HEADS, NOPE + HEAD_V)
    wkv = jnp.concatenate([ukv[..., :NOPE].reshape(depth, kv_lora, -1),
                           ukv[..., NOPE:].reshape(depth, kv_lora, -1)], axis=2).astype(BF16)
    return wx, wq, wkv


def kernel(x, mem, mix_norm_g, w_in, cq_norm_g, ckv_norm_g, w_uq, w_ukv, forget_bias, mla_out_g, fox_out_g, w_out, mem_q_norm_g, mem_kv_norm_g, w_mq, w_mkv, w_mo, ffn_norm_g, w_gate, w_up, w_down, final_norm_g):
    bsz, seq, d_model = x.shape
    depth = w_in.shape[0]
    q_lora = cq_norm_g.shape[1]
    kv_lora = ckv_norm_g.shape[1]
    mem_len = mem.shape[1]
    d_ff = w_gate.shape[2]
    width = N_HEADS * HEAD_PAD
    n_v = 2 * N_HEADS * HEAD_V
    assert d_model == width == n_v, "layout assumes d_model = 8 heads * 128"
    assert seq % BQ == 0 and seq % TM_PROJ == 0 and TM_PROJ % BK == 0 and BQ % BK == 0

    tabs = _rope_tables(seq)
    sel_np, decay_ones = _decay_constants()
    sel = jnp.asarray(sel_np, BF16)
    tri = jnp.asarray(np.tril(np.ones((TM_PROJ, TM_PROJ), np.float32)), BF16)
    params = pltpu.CompilerParams

    wx, wq, wkv = _mixer_weights(w_in, w_uq, w_ukv, q_lora, kv_lora)
    w_out_b, w_mq_b, w_mkv_b, w_mo_b = (w.astype(BF16) for w in (w_out, w_mq, w_mkv, w_mo))
    w_gate_b, w_up_b, w_down_b = (w.astype(BF16) for w in (w_gate, w_up, w_down))
    latent = jnp.pad(jnp.concatenate([cq_norm_g, ckv_norm_g] + [forget_bias] * N_SPLIT, axis=1),
                     ((0, 0), (0, d_model - q_lora - kv_lora - N_SPLIT * N_HEADS)))
    per_layer = lambda v: jnp.broadcast_to(v, (depth, d_model))
    rows = jnp.stack([mix_norm_g, latent, per_layer(jnp.asarray(decay_ones)),
                      jnp.concatenate([mla_out_g, fox_out_g], axis=1), mem_q_norm_g, ffn_norm_g,
                      per_layer(final_norm_g), mem_kv_norm_g], axis=1)
    rows_spec = lambda l: _layer_spec((8, d_model), l)

    h = x
    for l in range(depth):
        q_all, k_all, vt_all = pl.pallas_call(
            functools.partial(_proj_in_kernel, q_lora=q_lora, kv_lora=kv_lora),
            grid=(bsz, seq // TM_PROJ),
            in_specs=[
                pl.BlockSpec((1, TM_PROJ, d_model), lambda b, t: (b, t, 0)),
                rows_spec(l),
                pl.BlockSpec((TM_PROJ, 4 * LANE), lambda b, t: (t, 0)),
                _layer_spec(wx.shape[1:], l), _layer_spec(wq.shape[1:], l), _layer_spec(wkv.shape[1:], l),
                _const_spec(sel.shape), _const_spec(tri.shape),
            ],
            out_specs=[
                pl.BlockSpec((1, TM_PROJ, 2 * width), lambda b, t: (b, t, 0)),
                pl.BlockSpec((1, TM_PROJ, 2 * width), lambda b, t: (b, t, 0)),
                pl.BlockSpec((1, TM_PROJ // BK, n_v, BK), lambda b, t: (b, t, 0, 0)),
            ],
            out_shape=[
                jax.ShapeDtypeStruct((bsz, seq, 2 * width), BF16),
                jax.ShapeDtypeStruct((bsz, seq, 2 * width), BF16),
                jax.ShapeDtypeStruct((bsz, seq // BK, n_v, BK), BF16),
            ],
            scratch_shapes=[pltpu.VMEM((1, LANE), F32)],
            compiler_params=params(dimension_semantics=("arbitrary", "arbitrary"),
                                   vmem_limit_bytes=VMEM_LIMIT),
            name=f"proj_in_{l}",
        )(h, rows, tabs, wx, wq, wkv, sel, tri)

        o_all = pl.pallas_call(
            functools.partial(_attn_kernel, bq=BQ),
            grid=(bsz, N_HEADS),
            in_specs=[
                pl.BlockSpec((1, seq, 2 * HEAD_PAD), lambda b, p: (b, 0, p)),
                pl.BlockSpec((1, seq, 2 * HEAD_PAD), lambda b, p: (b, 0, p)),
                pl.BlockSpec((1, seq // BK, 2 * HEAD_V, BK), lambda b, p: (b, 0, p, 0)),
            ],
            out_specs=pl.BlockSpec((1, seq, 2 * HEAD_V), lambda b, p: (b, 0, p)),
            out_shape=jax.ShapeDtypeStruct((bsz, seq, n_v), F32),
            scratch_shapes=[
                pltpu.VMEM((BQ // BK, 2, BK, BQ + LANE), F32),
                pltpu.VMEM((BQ // BK, 2, BK, BQ + LANE), BF16),
                pltpu.VMEM((BQ // BK, 2, 1, BQ), F32),
                pltpu.VMEM((2, 1, BQ), F32),
                pltpu.VMEM((2, 1, BQ), F32),
                pltpu.VMEM((2, HEAD_V + ONES_ROWS, BQ), F32),
            ],
            compiler_params=params(dimension_semantics=("parallel", "parallel"),
                                   vmem_limit_bytes=VMEM_LIMIT),
            name=f"attn_{l}",
        )(q_all, k_all, vt_all)

        kv_mem = pl.pallas_call(
            _mem_kv_kernel,
            grid=(bsz,),
            in_specs=[
                pl.BlockSpec((1, mem_len, d_model), lambda b: (b, 0, 0)),
                rows_spec(l),
                _layer_spec(w_mkv.shape[1:], l),
            ],
            out_specs=pl.BlockSpec((1, mem_len, w_mkv.shape[2]), lambda b: (b, 0, 0)),
            out_shape=jax.ShapeDtypeStruct((bsz, mem_len, w_mkv.shape[2]), BF16),
            compiler_params=params(dimension_semantics=("parallel",), vmem_limit_bytes=VMEM_LIMIT),
            name=f"mem_kv_{l}",
        )(mem, rows, w_mkv_b)

        h = pl.pallas_call(
            _post_attn_kernel,
            grid=(bsz, seq // TM_PROJ),
            in_specs=[
                pl.BlockSpec((1, TM_PROJ, n_v), lambda b, t: (b, t, 0)),
                pl.BlockSpec((1, TM_PROJ, d_model), lambda b, t: (b, t, 0)),
                rows_spec(l),
                pl.BlockSpec((1, mem_len, w_mkv.shape[2]), lambda b, t: (b, 0, 0)),
                _layer_spec(w_out.shape[1:], l), _layer_spec(w_mq.shape[1:], l), _layer_spec(w_mo.shape[1:], l),
            ],
            out_specs=pl.BlockSpec((1, TM_PROJ, d_model), lambda b, t: (b, t, 0)),
            out_shape=jax.ShapeDtypeStruct((bsz, seq, d_model), F32),
            compiler_params=params(dimension_semantics=("parallel", "parallel"),
                                   vmem_limit_bytes=VMEM_LIMIT),
            name=f"post_attn_{l}",
        )(o_all, h, rows, kv_mem, w_out_b, w_mq_b, w_mo_b)

        last = l == depth - 1
        h = pl.pallas_call(
            functools.partial(_ffn_kernel, final_norm=last),
            grid=(bsz, seq // TM_FFN),
            in_specs=[
                pl.BlockSpec((1, TM_FFN, d_model), lambda b, t: (b, t, 0)),
                rows_spec(l),
                _layer_spec((d_model, d_ff), l), _layer_spec((d_model, d_ff), l), _layer_spec((d_ff, d_model), l),
            ],
            out_specs=pl.BlockSpec((1, TM_FFN, d_model), lambda b, t: (b, t, 0)),
            out_shape=jax.ShapeDtypeStruct((bsz, seq, d_model), F32),
            compiler_params=params(dimension_semantics=("parallel", "parallel"),
                                   vmem_limit_bytes=VMEM_LIMIT),
            name=f"ffn_{l}",
        )(h, rows, w_gate_b, w_up_b, w_down_b)
    return h
```

```python
import functools

import numpy as np
import jax
import jax.numpy as jnp
from jax import lax
from jax.experimental import pallas as pl
from jax.experimental.pallas import tpu as pltpu

F32 = jnp.float32
BF16 = jnp.bfloat16

EPS = 1e-6
ROPE_THETA = 10000.0
N_HEADS = 8
NOPE = 64
ROPE = 32
HEAD_V = 64
HEAD_PAD = 128
N_MEM_HEADS = 4
MEM_HD = 128
N_SPLIT = 3
ONES_ROWS = 16

LANE = 128
TM_PROJ = 512
TM_FFN = 512
BQ = 1024
BK = 256
MASKED = -2e30
M_INIT = -1e30
LOG2E = 1.4426950408889634
VMEM_LIMIT = 56 * 1024 * 1024

ROW_MIX_G, ROW_LATENT, ROW_DECAY_ONES, ROW_OUT_G, ROW_MEM_Q_G, ROW_FFN_G, ROW_FINAL_G, ROW_MEM_KV_G = range(8)


def _mm(a, b):
    return jnp.dot(a, b, preferred_element_type=F32)


def _mm_nt(a, b):
    return lax.dot_general(a, b, (((1,), (1,)), ((), ())), preferred_element_type=F32)


def _rms(x, g):
    return x * lax.rsqrt(jnp.mean(x * x, axis=-1, keepdims=True) + EPS) * g


def _split3(x):
    hi = x.astype(BF16).astype(F32)
    r = x - hi
    mid = r.astype(BF16).astype(F32)
    lo = (r - mid).astype(BF16).astype(F32)
    return hi, mid, lo


def _proj_in_kernel(h_ref, rows_ref, tabs_ref, wx_ref, wq_ref, wkv_ref, sel_ref, tri_ref,
                    q_ref, k_ref, vt_ref, carry_ref, *, q_lora, kv_lora):
    tm = h_ref.shape[1]
    n_sub = vt_ref.shape[1]
    bk = vt_ref.shape[3]
    width = N_HEADS * HEAD_PAD
    nv = N_HEADS * HEAD_V

    @pl.when(pl.program_id(1) == 0)
    def _():
        carry_ref[...] = jnp.zeros_like(carry_ref)

    xn = _rms(h_ref[0], rows_ref[ROW_MIX_G:ROW_MIX_G + 1, :]).astype(BF16)

    o_ckv = q_lora
    o_kr = o_ckv + kv_lora
    o_f3 = o_kr + 2 * LANE
    o_fq = o_f3 + LANE
    o_fk = o_fq + nv
    o_fv = o_fk + nv

    cos_q = tabs_ref[:, 0:LANE]
    sin_q = tabs_ref[:, LANE:2 * LANE]
    cos_k = tabs_ref[:, 2 * LANE:3 * LANE]
    sin_k = tabs_ref[:, 3 * LANE:4 * LANE]

    def put_values(v, row0):
        vt = v.T.astype(BF16)
        for c in range(n_sub):
            vt_ref[0, c, row0:row0 + nv, :] = vt[:, c * bk:(c + 1) * bk]

    lane = lax.broadcasted_iota(jnp.int32, (tm, LANE), 1)
    low = lane < HEAD_V

    def put_head_pairs(out_ref, col0, narrow, fill):
        for pair in range(N_HEADS // 2):
            blk = narrow[:, pair * LANE:(pair + 1) * LANE]
            other = fill(pair)
            c = col0 + 2 * pair * HEAD_PAD
            out_ref[0, :, c:c + HEAD_PAD] = jnp.where(low, blk, other).astype(BF16)
            odd = pltpu.roll(jnp.where(low, other, blk), HEAD_V, axis=1)
            out_ref[0, :, c + HEAD_PAD:c + 2 * HEAD_PAD] = odd.astype(BF16)

    cq = _mm(xn, wx_ref[:, 0:q_lora])
    ckv = _mm(xn, wx_ref[:, o_ckv:o_kr])
    o_bias = q_lora + kv_lora
    f3 = _mm(xn, wx_ref[:, o_f3:o_fq]) + rows_ref[ROW_LATENT:ROW_LATENT + 1, o_bias:o_bias + LANE]
    kr2 = _mm(xn, wx_ref[:, o_kr:o_f3])
    put_values(_mm(xn, wx_ref[:, o_fv:o_fv + nv]), nv)

    cqn = _rms(cq, rows_ref[ROW_LATENT:ROW_LATENT + 1, 0:q_lora]).astype(BF16)
    qa = _mm(cqn, wq_ref[...])
    first_half = lane < NOPE + ROPE // 2
    for hd in range(N_HEADS):
        sl = slice(hd * HEAD_PAD, (hd + 1) * HEAD_PAD)
        x = qa[:, sl]
        swapped = jnp.where(first_half, pltpu.roll(x, LANE - ROPE // 2, axis=1),
                            pltpu.roll(x, ROPE // 2, axis=1))
        q_ref[0, :, sl] = (x * cos_q + swapped * sin_q).astype(BF16)

    ckvn = _rms(ckv, rows_ref[ROW_LATENT:ROW_LATENT + 1, q_lora:q_lora + kv_lora]).astype(BF16)
    kn = _mm(ckvn, wkv_ref[:, 0:nv])
    kr = kr2[:, 0:LANE] * cos_k + kr2[:, LANE:2 * LANE] * sin_k
    kr_both = kr + pltpu.roll(kr, HEAD_V, axis=1)
    put_head_pairs(k_ref, 0, kn, lambda pair: kr_both)
    put_values(_mm(ckvn, wkv_ref[:, nv:2 * nv]), 0)

    live = lane < N_SPLIT * N_HEADS
    log_f = jnp.minimum(f3, 0.0) - jnp.log1p(jnp.exp(-jnp.abs(f3)))
    log_f = jnp.where(live, log_f, 0.0)
    pieces = jnp.concatenate([p.astype(BF16) for p in _split3(log_f)], axis=1)
    csum = _mm(tri_ref[...], pieces)
    fq = _mm(xn, wx_ref[:, o_fq:o_fk])
    cum = csum[:, 0:LANE] + csum[:, LANE:2 * LANE] + csum[:, 2 * LANE:3 * LANE] + carry_ref[...]
    carry_ref[...] = cum[tm - 1:tm, :]
    c_hi, c_mid, c_lo = _split3(cum * LOG2E)
    c_sel = jnp.where(lane < N_HEADS, c_hi, jnp.where(lane < 2 * N_HEADS, c_mid, c_lo))
    c_sel = jnp.where(live, c_sel, 0.0).astype(BF16)
    fk = _mm(xn, wx_ref[:, o_fk:o_fv])
    aug = _mm(c_sel, sel_ref[...])
    aug = aug + rows_ref[ROW_DECAY_ONES:ROW_DECAY_ONES + 1, :]
    put_head_pairs(q_ref, width, fq, lambda pair: aug[:, pair * LANE:(pair + 1) * LANE])
    put_head_pairs(k_ref, width, fk, lambda pair: aug[:, nv + pair * LANE:nv + (pair + 1) * LANE])


def _attn_kernel(qa_ref, qb_ref, ka_ref, kb_ref, vt_ref, o_ref,
                 s_scr, p_scr, mt_scr, al_scr, m_scr, acc_scr, *, bq):
    bk = vt_ref.shape[3]
    n_sub = bq // bk
    nq = qa_ref.shape[1] // bq
    q_refs = (qa_ref, qb_ref)
    k_refs = (ka_ref, kb_ref)

    def k_tile(j, hh):
        return k_refs[hh][0, pl.ds(pl.multiple_of(j * bk, bk), bk), :]

    def q_rows(qi, first, hh):
        return q_refs[hh][0, pl.ds(pl.multiple_of(qi * bq + first, bk), bq - first), :]

    def put_scores(u, hh, s):
        s_scr[u, hh, :, 0:bq] = s
        mt_scr[u, hh] = jnp.max(s, axis=0, keepdims=True)

    def scores_group(g, qi):
        for u in range(n_sub):
            for hh in range(2):
                put_scores(u, hh, _mm_nt(k_tile(n_sub * g + u, hh), q_rows(qi, 0, hh)))

    def scores_diagonal(qi):
        tri = (lax.broadcasted_iota(jnp.int32, (bk, bk), 0)
               <= lax.broadcasted_iota(jnp.int32, (bk, bk), 1))
        for u in range(n_sub):
            for hh in range(2):
                s = _mm_nt(k_tile(n_sub * qi + u, hh), q_rows(qi, u * bk, hh))
                parts = [jnp.full((bk, u * bk), MASKED, F32)] if u > 0 else []
                parts.append(jnp.where(tri, s[:, :bk], MASKED))
                if u < n_sub - 1:
                    parts.append(s[:, bk:])
                put_scores(u, hh, jnp.concatenate(parts, axis=1) if len(parts) > 1 else parts[0])

    def softmax_group():
        for hh in range(2):
            mt = mt_scr[0, hh]
            for u in range(1, n_sub):
                mt = jnp.maximum(mt, mt_scr[u, hh])
            m_new = jnp.maximum(m_scr[hh], mt)
            al_scr[hh] = jnp.exp2(m_scr[hh] - m_new)
            m_scr[hh] = m_new
            for u in range(n_sub):
                p_scr[u, hh, :, 0:bq] = jnp.exp2(s_scr[u, hh, :, 0:bq] - m_new).astype(BF16)

    def values_at(j0):
        ones = jnp.ones((ONES_ROWS, bk), BF16)
        for hh in range(2):
            pv = None
            for u in range(n_sub):
                vb = vt_ref[0, j0 + u]
                v = jnp.concatenate([vb[hh * HEAD_V:(hh + 1) * HEAD_V, :], ones], axis=0)
                d = _mm(v, p_scr[u, hh, :, 0:bq])
                pv = d if pv is None else pv + d
            acc_scr[hh, :, 0:bq] = al_scr[hh] * acc_scr[hh, :, 0:bq] + pv

    def last_group_start(qi):
        return jnp.where(qi == 0, 0, n_sub * (qi - 1))

    def reset():
        m_scr[...] = jnp.full_like(m_scr, M_INIT)
        acc_scr[...] = jnp.zeros_like(acc_scr)

    def finish(qi):
        ot = jnp.concatenate(
            [acc_scr[hh, 0:HEAD_V, 0:bq] * (1.0 / acc_scr[hh, HEAD_V:HEAD_V + 1, 0:bq]) for hh in range(2)],
            axis=0)
        o_ref[0, pl.ds(pl.multiple_of(qi * bq, bq), bq), :] = ot.T
        reset()

    def below_diagonal(qi):
        def body(g, carry):
            softmax_group()
            scores_group(g, qi)
            values_at(jnp.where(g == 0, n_sub * qi, n_sub * (g - 1)))
            return carry

        lax.fori_loop(0, qi, body, 0)

    reset()
    scores_diagonal(0)

    def block(qi, carry):
        softmax_group()
        scores_diagonal(qi)
        values_at(last_group_start(qi - 1))
        finish(qi - 1)
        below_diagonal(qi)
        return carry

    lax.fori_loop(1, nq, block, 0)
    softmax_group()
    values_at(last_group_start(nq - 1))
    finish(nq - 1)


def _post_attn_kernel(o_ref, h_ref, rows_ref, kv_ref, wo_ref, wmq_ref, wmo_ref, out_ref):
    half = o_ref.shape[2] // 2
    o = o_ref[0]
    merged = jnp.concatenate(
        [_rms(o[:, 0:half], rows_ref[ROW_OUT_G:ROW_OUT_G + 1, 0:half]),
         _rms(o[:, half:], rows_ref[ROW_OUT_G:ROW_OUT_G + 1, half:])],
        axis=1).astype(BF16)
    h1 = h_ref[0] + _mm(merged, wo_ref[...])
    xn = _rms(h1, rows_ref[ROW_MEM_Q_G:ROW_MEM_Q_G + 1, :]).astype(BF16)
    q = (_mm(xn, wmq_ref[...]) * (MEM_HD ** -0.5)).astype(BF16)
    kv = kv_ref[0]
    heads = []
    for hd in range(N_MEM_HEADS):
        kh = kv[:, 2 * hd * MEM_HD:(2 * hd + 1) * MEM_HD]
        vh = kv[:, (2 * hd + 1) * MEM_HD:(2 * hd + 2) * MEM_HD]
        s = _mm_nt(q[:, hd * MEM_HD:(hd + 1) * MEM_HD], kh)
        e = jnp.exp(s - jnp.max(s, axis=-1, keepdims=True))
        l = jnp.sum(e, axis=-1, keepdims=True)
        heads.append(_mm(e.astype(BF16), vh) * (1.0 / l))
    om = jnp.concatenate(heads, axis=1).astype(BF16)
    out_ref[0] = h1 + _mm(om, wmo_ref[...])


def _mem_kv_kernel(mem_ref, rows_ref, w_ref, kv_ref):
    g = rows_ref[ROW_MEM_KV_G:ROW_MEM_KV_G + 1, :]
    kv_ref[0] = _mm(_rms(mem_ref[0], g).astype(BF16), w_ref[...]).astype(BF16)


def _ffn_kernel(h_ref, rows_ref, wg_ref, wu_ref, wd_ref, out_ref, *, final_norm):
    h = h_ref[0]
    xn = _rms(h, rows_ref[ROW_FFN_G:ROW_FFN_G + 1, :]).astype(BF16)
    g = _mm(xn, wg_ref[...])
    u = _mm(xn, wu_ref[...])
    a = (g * (1.0 / (1.0 + jnp.exp(-g))) * u).astype(BF16)
    y = h + _mm(a, wd_ref[...])
    if final_norm:
        y = _rms(y, rows_ref[ROW_FINAL_G:ROW_FINAL_G + 1, :])
    out_ref[0] = y


def _const_spec(shape):
    return pl.BlockSpec(shape, lambda *_: (0,) * len(shape), pipeline_mode=pl.Buffered(1))


def _layer_spec(shape, l):
    return pl.BlockSpec((None,) + tuple(shape), lambda *_: (l,) + (0,) * len(shape),
                        pipeline_mode=pl.Buffered(1))


def _swap_halves(w):
    half = w.shape[-1] // 2
    return jnp.concatenate([w[..., half:], w[..., :half]], axis=-1)


def _decay_constants():
    n_half = N_HEADS * HEAD_V
    sel = np.zeros((LANE, 2 * n_half), np.float32)
    ones = np.zeros((2 * n_half,), np.float32)
    for hd in range(N_HEADS):
        base = (hd // 2) * LANE + (HEAD_V if hd % 2 == 0 else 0)
        for part in range(N_SPLIT):
            src = part * N_HEADS + hd
            sel[src, base + part] = 1.0
            ones[base + N_SPLIT + part] = 1.0
            ones[n_half + base + part] = 1.0
            sel[src, n_half + base + N_SPLIT + part] = -1.0
    return sel, ones


def _rope_tables(seq):
    inv = 1.0 / (ROPE_THETA ** (jnp.arange(0, ROPE, 2, dtype=F32) / ROPE))
    ang = jnp.arange(seq, dtype=F32)[:, None] * inv[None, :]
    cos, sin = jnp.cos(ang), jnp.sin(ang)
    cc = jnp.concatenate([cos, cos], axis=1)
    ss = jnp.concatenate([-sin, sin], axis=1)
    ones = jnp.ones((seq, NOPE), F32)
    z_lo = jnp.zeros((seq, NOPE), F32)
    z_hi = jnp.zeros((seq, HEAD_PAD - NOPE - ROPE), F32)
    scale = LOG2E * (NOPE + ROPE) ** -0.5
    return jnp.concatenate([
        scale * jnp.concatenate([ones, cc, z_hi], axis=1),
        scale * jnp.concatenate([z_lo, ss, z_hi], axis=1),
        jnp.concatenate([z_lo, cc, z_hi], axis=1),
        jnp.concatenate([z_lo, ss, z_hi], axis=1)], axis=1)


def _mixer_weights(w_in, w_uq, w_ukv, q_lora, kv_lora):
    depth = w_in.shape[0]
    fox = N_HEADS * HEAD_V
    o = q_lora + kv_lora
    kr = w_in[:, :, o:o + ROPE]
    o += ROPE
    wfq, wfk, wfv = w_in[:, :, o:o + fox], w_in[:, :, o + fox:o + 2 * fox], w_in[:, :, o + 2 * fox:o + 3 * fox]
    wfl = w_in[:, :, o + 3 * fox:o + 3 * fox + N_HEADS]

    def place_rope(c):
        return jnp.pad(c, ((0, 0), (0, 0), (NOPE, HEAD_PAD - NOPE - ROPE)))

    gate3 = jnp.pad(jnp.concatenate([wfl] * N_SPLIT, axis=2), ((0, 0), (0, 0), (0, LANE - N_SPLIT * N_HEADS)))
    wx = jnp.concatenate([
        w_in[:, :, 0:q_lora + kv_lora], place_rope(kr), place_rope(_swap_halves(kr)), gate3,
        wfq * (LOG2E * HEAD_V ** -0.5), wfk, wfv], axis=2).astype(BF16)

    uq = w_uq.reshape(depth, q_lora, N_HEADS, NOPE + ROPE)
    pad_hi = jnp.zeros((depth, q_lora, N_HEADS, HEAD_PAD - NOPE - ROPE), F32)
    wq = jnp.concatenate([uq, pad_hi], axis=3).reshape(depth, q_lora, -1).astype(BF16)

    ukv = w_ukv.reshape(depth, kv_lora, N_HEADS, NOPE + HEAD_V)
    wkv = jnp.concatenate([ukv[..., :NOPE].reshape(depth, kv_lora, -1),
                           ukv[..., NOPE:].reshape(depth, kv_lora, -1)], axis=2).astype(BF16)
    return wx, wq, wkv


def kernel(x, mem, mix_norm_g, w_in, cq_norm_g, ckv_norm_g, w_uq, w_ukv, forget_bias, mla_out_g, fox_out_g, w_out, mem_q_norm_g, mem_kv_norm_g, w_mq, w_mkv, w_mo, ffn_norm_g, w_gate, w_up, w_down, final_norm_g):
    bsz, seq, d_model = x.shape
    depth = w_in.shape[0]
    q_lora = cq_norm_g.shape[1]
    kv_lora = ckv_norm_g.shape[1]
    mem_len = mem.shape[1]
    d_ff = w_gate.shape[2]
    width = N_HEADS * HEAD_PAD
    n_v = 2 * N_HEADS * HEAD_V
    assert d_model == width == n_v, "layout assumes d_model = 8 heads * 128"
    assert seq % BQ == 0 and seq % TM_PROJ == 0 and TM_PROJ % BK == 0 and BQ % BK == 0

    tabs = _rope_tables(seq)
    sel_np, decay_ones = _decay_constants()
    sel = jnp.asarray(sel_np, BF16)
    tri = jnp.asarray(np.tril(np.ones((TM_PROJ, TM_PROJ), np.float32)), BF16)
    params = pltpu.CompilerParams

    wx, wq, wkv = _mixer_weights(w_in, w_uq, w_ukv, q_lora, kv_lora)
    w_out_b, w_mq_b, w_mkv_b, w_mo_b = (w.astype(BF16) for w in (w_out, w_mq, w_mkv, w_mo))
    w_gate_b, w_up_b, w_down_b = (w.astype(BF16) for w in (w_gate, w_up, w_down))
    latent = jnp.pad(jnp.concatenate([cq_norm_g, ckv_norm_g] + [forget_bias] * N_SPLIT, axis=1),
                     ((0, 0), (0, d_model - q_lora - kv_lora - N_SPLIT * N_HEADS)))
    per_layer = lambda v: jnp.broadcast_to(v, (depth, d_model))
    rows = jnp.stack([mix_norm_g, latent, per_layer(jnp.asarray(decay_ones)),
                      jnp.concatenate([mla_out_g, fox_out_g], axis=1), mem_q_norm_g, ffn_norm_g,
                      per_layer(final_norm_g), mem_kv_norm_g], axis=1)
    rows_spec = lambda l: _layer_spec((8, d_model), l)

    h = x
    for l in range(depth):
        q_all, k_all, vt_all = pl.pallas_call(
            functools.partial(_proj_in_kernel, q_lora=q_lora, kv_lora=kv_lora),
            grid=(bsz, seq // TM_PROJ),
            in_specs=[
                pl.BlockSpec((1, TM_PROJ, d_model), lambda b, t: (b, t, 0)),
                rows_spec(l),
                pl.BlockSpec((TM_PROJ, 4 * LANE), lambda b, t: (t, 0)),
                _layer_spec(wx.shape[1:], l), _layer_spec(wq.shape[1:], l), _layer_spec(wkv.shape[1:], l),
                _const_spec(sel.shape), _const_spec(tri.shape),
            ],
            out_specs=[
                pl.BlockSpec((1, TM_PROJ, 2 * width), lambda b, t: (b, t, 0)),
                pl.BlockSpec((1, TM_PROJ, 2 * width), lambda b, t: (b, t, 0)),
                pl.BlockSpec((1, TM_PROJ // BK, n_v, BK), lambda b, t: (b, t, 0, 0)),
            ],
            out_shape=[
                jax.ShapeDtypeStruct((bsz, seq, 2 * width), BF16),
                jax.ShapeDtypeStruct((bsz, seq, 2 * width), BF16),
                jax.ShapeDtypeStruct((bsz, seq // BK, n_v, BK), BF16),
            ],
            scratch_shapes=[pltpu.VMEM((1, LANE), F32)],
            compiler_params=params(dimension_semantics=("arbitrary", "arbitrary"),
                                   vmem_limit_bytes=VMEM_LIMIT),
            name=f"proj_in_{l}",
        )(h, rows, tabs, wx, wq, wkv, sel, tri)

        o_all = pl.pallas_call(
            functools.partial(_attn_kernel, bq=BQ),
            grid=(bsz, N_HEADS),
            in_specs=[
                pl.BlockSpec((1, seq, HEAD_PAD), lambda b, p: (b, 0, 2 * p)),
                pl.BlockSpec((1, seq, HEAD_PAD), lambda b, p: (b, 0, 2 * p + 1)),
                pl.BlockSpec((1, seq, HEAD_PAD), lambda b, p: (b, 0, 2 * p)),
                pl.BlockSpec((1, seq, HEAD_PAD), lambda b, p: (b, 0, 2 * p + 1)),
                pl.BlockSpec((1, seq // BK, 2 * HEAD_V, BK), lambda b, p: (b, 0, p, 0)),
            ],
            out_specs=pl.BlockSpec((1, seq, 2 * HEAD_V), lambda b, p: (b, 0, p)),
            out_shape=jax.ShapeDtypeStruct((bsz, seq, n_v), F32),
            scratch_shapes=[
                pltpu.VMEM((BQ // BK, 2, BK, BQ + LANE), F32),
                pltpu.VMEM((BQ // BK, 2, BK, BQ + LANE), BF16),
                pltpu.VMEM((BQ // BK, 2, 1, BQ), F32),
                pltpu.VMEM((2, 1, BQ), F32),
                pltpu.VMEM((2, 1, BQ), F32),
                pltpu.VMEM((2, HEAD_V + ONES_ROWS, BQ + LANE), F32),
            ],
            compiler_params=params(dimension_semantics=("parallel", "parallel"),
                                   vmem_limit_bytes=VMEM_LIMIT),
            name=f"attn_{l}",
        )(q_all, q_all, k_all, k_all, vt_all)

        kv_mem = pl.pallas_call(
            _mem_kv_kernel,
            grid=(bsz,),
            in_specs=[
                pl.BlockSpec((1, mem_len, d_model), lambda b: (b, 0, 0)),
                rows_spec(l),
                _layer_spec(w_mkv.shape[1:], l),
            ],
            out_specs=pl.BlockSpec((1, mem_len, w_mkv.shape[2]), lambda b: (b, 0, 0)),
            out_shape=jax.ShapeDtypeStruct((bsz, mem_len, w_mkv.shape[2]), BF16),
            compiler_params=params(dimension_semantics=("parallel",), vmem_limit_bytes=VMEM_LIMIT),
            name=f"mem_kv_{l}",
        )(mem, rows, w_mkv_b)

        h = pl.pallas_call(
            _post_attn_kernel,
            grid=(bsz, seq // TM_PROJ),
            in_specs=[
                pl.BlockSpec((1, TM_PROJ, n_v), lambda b, t: (b, t, 0)),
                pl.BlockSpec((1, TM_PROJ, d_model), lambda b, t: (b, t, 0)),
                rows_spec(l),
                pl.BlockSpec((1, mem_len, w_mkv.shape[2]), lambda b, t: (b, 0, 0)),
                _layer_spec(w_out.shape[1:], l), _layer_spec(w_mq.shape[1:], l), _layer_spec(w_mo.shape[1:], l),
            ],
            out_specs=pl.BlockSpec((1, TM_PROJ, d_model), lambda b, t: (b, t, 0)),
            out_shape=jax.ShapeDtypeStruct((bsz, seq, d_model), F32),
            compiler_params=params(dimension_semantics=("parallel", "parallel"),
                                   vmem_limit_bytes=VMEM_LIMIT),
            name=f"post_attn_{l}",
        )(o_all, h, rows, kv_mem, w_out_b, w_mq_b, w_mo_b)

        last = l == depth - 1
        h = pl.pallas_call(
            functools.partial(_ffn_kernel, final_norm=last),
            grid=(bsz, seq // TM_FFN),
            in_specs=[
                pl.BlockSpec((1, TM_FFN, d_model), lambda b, t: (b, t, 0)),
                rows_spec(l),
                _layer_spec((d_model, d_ff), l), _layer_spec((d_model, d_ff), l), _layer_spec((d_ff, d_model), l),
            ],
            out_specs=pl.BlockSpec((1, TM_FFN, d_model), lambda b, t: (b, t, 0)),
            out_shape=jax.ShapeDtypeStruct((bsz, seq, d_model), F32),
            compiler_params=params(dimension_semantics=("parallel", "parallel"),
                                   vmem_limit_bytes=VMEM_LIMIT),
            name=f"ffn_{l}",
        )(h, rows, w_gate_b, w_up_b, w_down_b)
    return h
```

```python
import functools

import numpy as np
import jax
import jax.numpy as jnp
from jax import lax
from jax.experimental import pallas as pl
from jax.experimental.pallas import tpu as pltpu

F32 = jnp.float32
BF16 = jnp.bfloat16

EPS = 1e-6
ROPE_THETA = 10000.0
N_HEADS = 8
NOPE = 64
ROPE = 32
HEAD_V = 64
HEAD_PAD = 128
N_MEM_HEADS = 4
MEM_HD = 128
N_SPLIT = 3
ONES_ROWS = 16

LANE = 128
TM_PROJ = 512
TM_POST = 1024
TM_FFN = 512
BQ = 1024
BK = 256
MASKED = -2e30
M_INIT = -1e30
LOG2E = 1.4426950408889634
VMEM_LIMIT = 56 * 1024 * 1024

ROW_MIX_G, ROW_LATENT, ROW_DECAY_ONES, ROW_OUT_G, ROW_MEM_Q_G, ROW_FFN_G, ROW_FINAL_G, ROW_MEM_KV_G = range(8)


def _mm(a, b):
    return jnp.dot(a, b, preferred_element_type=F32)


def _mm_nt(a, b):
    return lax.dot_general(a, b, (((1,), (1,)), ((), ())), preferred_element_type=F32)


def _rms(x, g):
    return x * lax.rsqrt(jnp.mean(x * x, axis=-1, keepdims=True) + EPS) * g


def _split3(x):
    hi = x.astype(BF16).astype(F32)
    r = x - hi
    mid = r.astype(BF16).astype(F32)
    lo = (r - mid).astype(BF16).astype(F32)
    return hi, mid, lo


def _proj_in_kernel(h_ref, rows_ref, tabs_ref, wx_ref, wq_ref, wkv_ref, sel_ref, tri_ref,
                    q_ref, k_ref, vt_ref, carry_ref, *, q_lora, kv_lora):
    tm = h_ref.shape[1]
    n_sub = vt_ref.shape[1]
    bk = vt_ref.shape[3]
    width = N_HEADS * HEAD_PAD
    nv = N_HEADS * HEAD_V

    @pl.when(pl.program_id(1) == 0)
    def _():
        carry_ref[...] = jnp.zeros_like(carry_ref)

    xn = _rms(h_ref[0], rows_ref[ROW_MIX_G:ROW_MIX_G + 1, :]).astype(BF16)

    o_ckv = q_lora
    o_kf = o_ckv + kv_lora
    o_fq = o_kf + LANE
    o_fk = o_fq + nv
    o_fv = o_fk + nv

    cos_q = tabs_ref[:, 0:LANE]
    sin_q = tabs_ref[:, LANE:2 * LANE]
    cos_k = tabs_ref[:, 2 * LANE:3 * LANE]
    sin_k = tabs_ref[:, 3 * LANE:4 * LANE]

    def put_values(v, row0):
        vt = v.T.astype(BF16)
        for c in range(n_sub):
            vt_ref[0, c, row0:row0 + nv, :] = vt[:, c * bk:(c + 1) * bk]

    lane = lax.broadcasted_iota(jnp.int32, (tm, LANE), 1)
    low = lane < HEAD_V

    def put_head_pairs(out_ref, col0, narrow, fill):
        for pair in range(N_HEADS // 2):
            blk = narrow[:, pair * LANE:(pair + 1) * LANE]
            other = fill(pair)
            c = col0 + 2 * pair * HEAD_PAD
            out_ref[0, :, c:c + HEAD_PAD] = jnp.where(low, blk, other).astype(BF16)
            odd = pltpu.roll(jnp.where(low, other, blk), HEAD_V, axis=1)
            out_ref[0, :, c + HEAD_PAD:c + 2 * HEAD_PAD] = odd.astype(BF16)

    cq = _mm(xn, wx_ref[:, 0:q_lora])
    ckv = _mm(xn, wx_ref[:, o_ckv:o_kf])
    o_bias = q_lora + kv_lora
    kf = _mm(xn, wx_ref[:, o_kf:o_fq])
    f3 = kf + rows_ref[ROW_LATENT:ROW_LATENT + 1, o_bias:o_bias + LANE]
    put_values(_mm(xn, wx_ref[:, o_fv:o_fv + nv]), nv)

    cqn = _rms(cq, rows_ref[ROW_LATENT:ROW_LATENT + 1, 0:q_lora]).astype(BF16)
    qa = _mm(cqn, wq_ref[...])
    first_half = lane < NOPE + ROPE // 2

    def rotary(x, cos, sin):
        swapped = jnp.where(first_half, pltpu.roll(x, LANE - ROPE // 2, axis=1),
                            pltpu.roll(x, ROPE // 2, axis=1))
        return x * cos + swapped * sin

    for hd in range(N_HEADS):
        sl = slice(hd * HEAD_PAD, (hd + 1) * HEAD_PAD)
        q_ref[0, :, sl] = rotary(qa[:, sl], cos_q, sin_q).astype(BF16)

    ckvn = _rms(ckv, rows_ref[ROW_LATENT:ROW_LATENT + 1, q_lora:q_lora + kv_lora]).astype(BF16)
    kn = _mm(ckvn, wkv_ref[:, 0:nv])
    kr = rotary(kf, cos_k, sin_k)
    kr_both = kr + pltpu.roll(kr, HEAD_V, axis=1)
    put_head_pairs(k_ref, 0, kn, lambda pair: kr_both)
    put_values(_mm(ckvn, wkv_ref[:, nv:2 * nv]), 0)

    live = lane < N_SPLIT * N_HEADS
    log_f = jnp.minimum(f3, 0.0) - jnp.log1p(jnp.exp(-jnp.abs(f3)))
    log_f = jnp.where(live, log_f, 0.0)
    pieces = jnp.concatenate([p.astype(BF16) for p in _split3(log_f)], axis=1)
    csum = _mm(tri_ref[...], pieces)
    fq = _mm(xn, wx_ref[:, o_fq:o_fk])
    cum = csum[:, 0:LANE] + csum[:, LANE:2 * LANE] + csum[:, 2 * LANE:3 * LANE] + carry_ref[...]
    carry_ref[...] = cum[tm - 1:tm, :]
    c_hi, c_mid, c_lo = _split3(cum * LOG2E)
    c_sel = jnp.where(lane < N_HEADS, c_hi, jnp.where(lane < 2 * N_HEADS, c_mid, c_lo))
    c_sel = jnp.where(live, c_sel, 0.0).astype(BF16)
    fk = _mm(xn, wx_ref[:, o_fk:o_fv])
    aug = _mm(c_sel, sel_ref[...])
    aug = aug + rows_ref[ROW_DECAY_ONES:ROW_DECAY_ONES + 1, :]
    put_head_pairs(q_ref, width, fq, lambda pair: aug[:, pair * LANE:(pair + 1) * LANE])
    put_head_pairs(k_ref, width, fk, lambda pair: aug[:, nv + pair * LANE:nv + (pair + 1) * LANE])


def _attn_kernel(q_ref, k_ref, vt_ref, o_ref, s_scr, p_scr, mt_scr, al_scr, m_scr, acc_scr, *, bq):
    bk = vt_ref.shape[3]
    n_sub = bq // bk
    nq = q_ref.shape[1] // bq
    heads = [slice(hh * HEAD_PAD, (hh + 1) * HEAD_PAD) for hh in range(2)]

    def k_tile(j):
        return k_ref[0, pl.ds(pl.multiple_of(j * bk, bk), bk), :]

    def q_rows(qi, first, hh):
        return q_ref[0, pl.ds(pl.multiple_of(qi * bq + first, bk), bq - first), heads[hh]]

    def put_scores(u, hh, s):
        s_scr[u, hh, :, 0:bq] = s
        mt_scr[u, hh] = jnp.max(s, axis=0, keepdims=True)

    def scores_group(g, qi):
        for u in range(n_sub):
            kb = k_tile(n_sub * g + u)
            for hh in range(2):
                put_scores(u, hh, _mm_nt(kb[:, heads[hh]], q_rows(qi, 0, hh)))

    def scores_diagonal(qi):
        tri = (lax.broadcasted_iota(jnp.int32, (bk, bk), 0)
               <= lax.broadcasted_iota(jnp.int32, (bk, bk), 1))
        for u in range(n_sub):
            kb = k_tile(n_sub * qi + u)
            for hh in range(2):
                s = _mm_nt(kb[:, heads[hh]], q_rows(qi, u * bk, hh))
                parts = [jnp.full((bk, u * bk), MASKED, F32)] if u > 0 else []
                parts.append(jnp.where(tri, s[:, :bk], MASKED))
                if u < n_sub - 1:
                    parts.append(s[:, bk:])
                put_scores(u, hh, jnp.concatenate(parts, axis=1) if len(parts) > 1 else parts[0])

    def softmax_group():
        for hh in range(2):
            mt = mt_scr[0, hh]
            for u in range(1, n_sub):
                mt = jnp.maximum(mt, mt_scr[u, hh])
            m_new = jnp.maximum(m_scr[hh], mt)
            al_scr[hh] = jnp.exp2(m_scr[hh] - m_new)
            m_scr[hh] = m_new
            for u in range(n_sub):
                p_scr[u, hh, :, 0:bq] = jnp.exp2(s_scr[u, hh, :, 0:bq] - m_new).astype(BF16)

    def values_at(j0):
        ones = jnp.ones((ONES_ROWS, bk), BF16)
        for hh in range(2):
            pv = None
            for u in range(n_sub):
                vb = vt_ref[0, j0 + u]
                v = jnp.concatenate([vb[hh * HEAD_V:(hh + 1) * HEAD_V, :], ones], axis=0)
                d = _mm(v, p_scr[u, hh, :, 0:bq])
                pv = d if pv is None else pv + d
            acc_scr[hh] = al_scr[hh] * acc_scr[hh] + pv

    def last_group_start(qi):
        return jnp.where(qi == 0, 0, n_sub * (qi - 1))

    def reset():
        m_scr[...] = jnp.full_like(m_scr, M_INIT)
        acc_scr[...] = jnp.zeros_like(acc_scr)

    def finish(qi):
        ot = jnp.concatenate(
            [acc_scr[hh, 0:HEAD_V] * (1.0 / acc_scr[hh, HEAD_V:HEAD_V + 1]) for hh in range(2)], axis=0)
        o_ref[0, pl.ds(pl.multiple_of(qi * bq, bq), bq), :] = ot.T
        reset()

    def below_diagonal(qi):
        def body(g, carry):
            softmax_group()
            scores_group(g, qi)
            values_at(jnp.where(g == 0, n_sub * qi, n_sub * (g - 1)))
            return carry

        lax.fori_loop(0, qi, body, 0)

    reset()
    scores_diagonal(0)

    def block(qi, carry):
        softmax_group()
        scores_diagonal(qi)
        values_at(last_group_start(qi - 1))
        finish(qi - 1)
        below_diagonal(qi)
        return carry

    lax.fori_loop(1, nq, block, 0)
    softmax_group()
    values_at(last_group_start(nq - 1))
    finish(nq - 1)


def _post_attn_kernel(o_ref, h_ref, rows_ref, kv_ref, wo_ref, wmq_ref, wmo_ref, out_ref):
    half = o_ref.shape[2] // 2
    o = o_ref[0]
    merged = jnp.concatenate(
        [_rms(o[:, 0:half], rows_ref[ROW_OUT_G:ROW_OUT_G + 1, 0:half]),
         _rms(o[:, half:], rows_ref[ROW_OUT_G:ROW_OUT_G + 1, half:])],
        axis=1).astype(BF16)
    h1 = h_ref[0] + _mm(merged, wo_ref[...])
    xn = _rms(h1, rows_ref[ROW_MEM_Q_G:ROW_MEM_Q_G + 1, :]).astype(BF16)
    q = (_mm(xn, wmq_ref[...]) * (MEM_HD ** -0.5)).astype(BF16)
    kv = kv_ref[0]
    heads = []
    for hd in range(N_MEM_HEADS):
        kh = kv[:, 2 * hd * MEM_HD:(2 * hd + 1) * MEM_HD]
        vh = kv[:, (2 * hd + 1) * MEM_HD:(2 * hd + 2) * MEM_HD]
        s = _mm_nt(q[:, hd * MEM_HD:(hd + 1) * MEM_HD], kh)
        e = jnp.exp(s - jnp.max(s, axis=-1, keepdims=True))
        l = jnp.sum(e, axis=-1, keepdims=True)
        heads.append(_mm(e.astype(BF16), vh) * (1.0 / l))
    om = jnp.concatenate(heads, axis=1).astype(BF16)
    out_ref[0] = h1 + _mm(om, wmo_ref[...])


def _mem_kv_kernel(mem_ref, rows_ref, w_ref, kv_ref):
    g = rows_ref[ROW_MEM_KV_G:ROW_MEM_KV_G + 1, :]
    kv_ref[0] = _mm(_rms(mem_ref[0], g).astype(BF16), w_ref[...]).astype(BF16)


def _ffn_kernel(h_ref, rows_ref, wg_ref, wu_ref, wd_ref, out_ref, *, final_norm):
    h = h_ref[0]
    xn = _rms(h, rows_ref[ROW_FFN_G:ROW_FFN_G + 1, :]).astype(BF16)
    g = _mm(xn, wg_ref[...])
    u = _mm(xn, wu_ref[...])
    a = (g * (1.0 / (1.0 + jnp.exp(-g))) * u).astype(BF16)
    y = h + _mm(a, wd_ref[...])
    if final_norm:
        y = _rms(y, rows_ref[ROW_FINAL_G:ROW_FINAL_G + 1, :])
    out_ref[0] = y


def _const_spec(shape):
    return pl.BlockSpec(shape, lambda *_: (0,) * len(shape), pipeline_mode=pl.Buffered(1))


def _layer_spec(shape, l):
    return pl.BlockSpec((None,) + tuple(shape), lambda *_: (l,) + (0,) * len(shape),
                        pipeline_mode=pl.Buffered(1))


def _decay_constants():
    n_half = N_HEADS * HEAD_V
    sel = np.zeros((LANE, 2 * n_half), np.float32)
    ones = np.zeros((2 * n_half,), np.float32)
    for hd in range(N_HEADS):
        base = (hd // 2) * LANE + (HEAD_V if hd % 2 == 0 else 0)
        for part in range(N_SPLIT):
            src = part * N_HEADS + hd
            sel[src, base + part] = 1.0
            ones[base + N_SPLIT + part] = 1.0
            ones[n_half + base + part] = 1.0
            sel[src, n_half + base + N_SPLIT + part] = -1.0
    return sel, ones


def _rope_tables(seq):
    inv = 1.0 / (ROPE_THETA ** (jnp.arange(0, ROPE, 2, dtype=F32) / ROPE))
    ang = jnp.arange(seq, dtype=F32)[:, None] * inv[None, :]
    cos, sin = jnp.cos(ang), jnp.sin(ang)
    cc = jnp.concatenate([cos, cos], axis=1)
    ss = jnp.concatenate([-sin, sin], axis=1)
    ones = jnp.ones((seq, NOPE), F32)
    z_lo = jnp.zeros((seq, NOPE), F32)
    z_hi = jnp.zeros((seq, HEAD_PAD - NOPE - ROPE), F32)
    scale = LOG2E * (NOPE + ROPE) ** -0.5
    return jnp.concatenate([
        scale * jnp.concatenate([ones, cc, z_hi], axis=1),
        scale * jnp.concatenate([z_lo, ss, z_hi], axis=1),
        jnp.concatenate([z_lo, cc, z_hi], axis=1),
        jnp.concatenate([z_lo, ss, z_hi], axis=1)], axis=1)


def _mixer_weights(w_in, w_uq, w_ukv, q_lora, kv_lora):
    depth = w_in.shape[0]
    fox = N_HEADS * HEAD_V
    o = q_lora + kv_lora
    kr = w_in[:, :, o:o + ROPE]
    o += ROPE
    wfq, wfk, wfv = w_in[:, :, o:o + fox], w_in[:, :, o + fox:o + 2 * fox], w_in[:, :, o + 2 * fox:o + 3 * fox]
    wfl = w_in[:, :, o + 3 * fox:o + 3 * fox + N_HEADS]

    zeros = lambda n: jnp.zeros(w_in.shape[:2] + (n,), F32)
    gate_and_rope = jnp.concatenate(
        [wfl] * N_SPLIT + [zeros(NOPE - N_SPLIT * N_HEADS), kr, zeros(HEAD_PAD - NOPE - ROPE)], axis=2)
    wx = jnp.concatenate([
        w_in[:, :, 0:q_lora + kv_lora], gate_and_rope,
        wfq * (LOG2E * HEAD_V ** -0.5), wfk, wfv], axis=2).astype(BF16)

    uq = w_uq.reshape(depth, q_lora, N_HEADS, NOPE + ROPE)
    pad_hi = jnp.zeros((depth, q_lora, N_HEADS, HEAD_PAD - NOPE - ROPE), F32)
    wq = jnp.concatenate([uq, pad_hi], axis=3).reshape(depth, q_lora, -1).astype(BF16)

    ukv = w_ukv.reshape(depth, kv_lora, N_HEADS, NOPE + HEAD_V)
    wkv = jnp.concatenate([ukv[..., :NOPE].reshape(depth, kv_lora, -1),
                           ukv[..., NOPE:].reshape(depth, kv_lora, -1)], axis=2).astype(BF16)
    return wx, wq, wkv


def kernel(x, mem, mix_norm_g, w_in, cq_norm_g, ckv_norm_g, w_uq, w_ukv, forget_bias, mla_out_g, fox_out_g, w_out, mem_q_norm_g, mem_kv_norm_g, w_mq, w_mkv, w_mo, ffn_norm_g, w_gate, w_up, w_down, final_norm_g):
    bsz, seq, d_model = x.shape
    depth = w_in.shape[0]
    q_lora = cq_norm_g.shape[1]
    kv_lora = ckv_norm_g.shape[1]
    mem_len = mem.shape[1]
    d_ff = w_gate.shape[2]
    width = N_HEADS * HEAD_PAD
    n_v = 2 * N_HEADS * HEAD_V
    assert d_model == width == n_v, "layout assumes d_model = 8 heads * 128"
    assert seq % BQ == 0 and seq % TM_PROJ == 0 and TM_PROJ % BK == 0 and BQ % BK == 0
    assert seq % TM_POST == 0 and seq % TM_FFN == 0

    tabs = _rope_tables(seq)
    sel_np, decay_ones = _decay_constants()
    sel = jnp.asarray(sel_np, BF16)
    tri = jnp.asarray(np.tril(np.ones((TM_PROJ, TM_PROJ), np.float32)), BF16)
    params = pltpu.CompilerParams

    wx, wq, wkv = _mixer_weights(w_in, w_uq, w_ukv, q_lora, kv_lora)
    w_out_b, w_mq_b, w_mkv_b, w_mo_b = (w.astype(BF16) for w in (w_out, w_mq, w_mkv, w_mo))
    w_gate_b, w_up_b, w_down_b = (w.astype(BF16) for w in (w_gate, w_up, w_down))
    latent = jnp.pad(jnp.concatenate([cq_norm_g, ckv_norm_g] + [forget_bias] * N_SPLIT, axis=1),
                     ((0, 0), (0, d_model - q_lora - kv_lora - N_SPLIT * N_HEADS)))
    per_layer = lambda v: jnp.broadcast_to(v, (depth, d_model))
    rows = jnp.stack([mix_norm_g, latent, per_layer(jnp.asarray(decay_ones)),
                      jnp.concatenate([mla_out_g, fox_out_g], axis=1), mem_q_norm_g, ffn_norm_g,
                      per_layer(final_norm_g), mem_kv_norm_g], axis=1)
    rows_spec = lambda l: _layer_spec((8, d_model), l)

    h = x
    for l in range(depth):
        q_all, k_all, vt_all = pl.pallas_call(
            functools.partial(_proj_in_kernel, q_lora=q_lora, kv_lora=kv_lora),
            grid=(bsz, seq // TM_PROJ),
            in_specs=[
                pl.BlockSpec((1, TM_PROJ, d_model), lambda b, t: (b, t, 0)),
                rows_spec(l),
                pl.BlockSpec((TM_PROJ, 4 * LANE), lambda b, t: (t, 0)),
                _layer_spec(wx.shape[1:], l), _layer_spec(wq.shape[1:], l), _layer_spec(wkv.shape[1:], l),
                _const_spec(sel.shape), _const_spec(tri.shape),
            ],
            out_specs=[
                pl.BlockSpec((1, TM_PROJ, 2 * width), lambda b, t: (b, t, 0)),
                pl.BlockSpec((1, TM_PROJ, 2 * width), lambda b, t: (b, t, 0)),
                pl.BlockSpec((1, TM_PROJ // BK, n_v, BK), lambda b, t: (b, t, 0, 0)),
            ],
            out_shape=[
                jax.ShapeDtypeStruct((bsz, seq, 2 * width), BF16),
                jax.ShapeDtypeStruct((bsz, seq, 2 * width), BF16),
                jax.ShapeDtypeStruct((bsz, seq // BK, n_v, BK), BF16),
            ],
            scratch_shapes=[pltpu.VMEM((1, LANE), F32)],
            compiler_params=params(dimension_semantics=("arbitrary", "arbitrary"),
                                   vmem_limit_bytes=VMEM_LIMIT),
            name=f"proj_in_{l}",
        )(h, rows, tabs, wx, wq, wkv, sel, tri)

        o_all = pl.pallas_call(
            functools.partial(_attn_kernel, bq=BQ),
            grid=(bsz, N_HEADS),
            in_specs=[
                pl.BlockSpec((1, seq, 2 * HEAD_PAD), lambda b, p: (b, 0, p)),
                pl.BlockSpec((1, seq, 2 * HEAD_PAD), lambda b, p: (b, 0, p)),
                pl.BlockSpec((1, seq // BK, 2 * HEAD_V, BK), lambda b, p: (b, 0, p, 0)),
            ],
            out_specs=pl.BlockSpec((1, seq, 2 * HEAD_V), lambda b, p: (b, 0, p)),
            out_shape=jax.ShapeDtypeStruct((bsz, seq, n_v), F32),
            scratch_shapes=[
                pltpu.VMEM((BQ // BK, 2, BK, BQ + LANE), F32),
                pltpu.VMEM((BQ // BK, 2, BK, BQ + LANE), BF16),
                pltpu.VMEM((BQ // BK, 2, 1, BQ), F32),
                pltpu.VMEM((2, 1, BQ), F32),
                pltpu.VMEM((2, 1, BQ), F32),
                pltpu.VMEM((2, HEAD_V + ONES_ROWS, BQ), F32),
            ],
            compiler_params=params(dimension_semantics=("parallel", "parallel"),
                                   vmem_limit_bytes=VMEM_LIMIT),
            name=f"attn_{l}",
        )(q_all, k_all, vt_all)

        kv_mem = pl.pallas_call(
            _mem_kv_kernel,
            grid=(bsz,),
            in_specs=[
                pl.BlockSpec((1, mem_len, d_model), lambda b: (b, 0, 0)),
                rows_spec(l),
                _layer_spec(w_mkv.shape[1:], l),
            ],
            out_specs=pl.BlockSpec((1, mem_len, w_mkv.shape[2]), lambda b: (b, 0, 0)),
            out_shape=jax.ShapeDtypeStruct((bsz, mem_len, w_mkv.shape[2]), BF16),
            compiler_params=params(dimension_semantics=("parallel",), vmem_limit_bytes=VMEM_LIMIT),
            name=f"mem_kv_{l}",
        )(mem, rows, w_mkv_b)

        h = pl.pallas_call(
            _post_attn_kernel,
            grid=(bsz, seq // TM_POST),
            in_specs=[
                pl.BlockSpec((1, TM_POST, n_v), lambda b, t: (b, t, 0)),
                pl.BlockSpec((1, TM_POST, d_model), lambda b, t: (b, t, 0)),
                rows_spec(l),
                pl.BlockSpec((1, mem_len, w_mkv.shape[2]), lambda b, t: (b, 0, 0)),
                _layer_spec(w_out.shape[1:], l), _layer_spec(w_mq.shape[1:], l), _layer_spec(w_mo.shape[1:], l),
            ],
            out_specs=pl.BlockSpec((1, TM_POST, d_model), lambda b, t: (b, t, 0)),
            out_shape=jax.ShapeDtypeStruct((bsz, seq, d_model), F32),
            compiler_params=params(dimension_semantics=("parallel", "parallel"),
                                   vmem_limit_bytes=VMEM_LIMIT),
            name=f"post_attn_{l}",
        )(o_all, h, rows, kv_mem, w_out_b, w_mq_b, w_mo_b)

        last = l == depth - 1
        h = pl.pallas_call(
            functools.partial(_ffn_kernel, final_norm=last),
            grid=(bsz, seq // TM_FFN),
            in_specs=[
                pl.BlockSpec((1, TM_FFN, d_model), lambda b, t: (b, t, 0)),
                rows_spec(l),
                _layer_spec((d_model, d_ff), l), _layer_spec((d_model, d_ff), l), _layer_spec((d_ff, d_model), l),
            ],
            out_specs=pl.BlockSpec((1, TM_FFN, d_model), lambda b, t: (b, t, 0)),
            out_shape=jax.ShapeDtypeStruct((bsz, seq, d_model), F32),
            compiler_params=params(dimension_semantics=("parallel", "parallel"),
                                   vmem_limit_bytes=VMEM_LIMIT),
            name=f"ffn_{l}",
        )(h, rows, w_gate_b, w_up_b, w_down_b)
    return h
```

```python
import functools

import numpy as np
import jax
import jax.numpy as jnp
from jax import lax
from jax.experimental import pallas as pl
from jax.experimental.pallas import tpu as pltpu

F32 = jnp.float32
BF16 = jnp.bfloat16

EPS = 1e-6
ROPE_THETA = 10000.0
N_HEADS = 8
NOPE = 64
ROPE = 32
HEAD_V = 64
HEAD_PAD = 128
N_MEM_HEADS = 4
MEM_HD = 128
N_SPLIT = 3
ONES_ROWS = 16

LANE = 128
TM_PROJ = 512
TM_POST = 1024
TM_FFN = 512
BQ = 1024
BK = 512
MASKED = -2e30
M_INIT = -1e30
LOG2E = 1.4426950408889634
VMEM_LIMIT = 56 * 1024 * 1024

ROW_MIX_G, ROW_LATENT, ROW_DECAY_ONES, ROW_OUT_G, ROW_MEM_Q_G, ROW_FFN_G, ROW_FINAL_G, ROW_MEM_KV_G = range(8)


def _mm(a, b):
    return jnp.dot(a, b, preferred_element_type=F32)


def _mm_nt(a, b):
    return lax.dot_general(a, b, (((1,), (1,)), ((), ())), preferred_element_type=F32)


def _rms(x, g):
    return x * lax.rsqrt(jnp.mean(x * x, axis=-1, keepdims=True) + EPS) * g


def _split3(x):
    hi = x.astype(BF16).astype(F32)
    r = x - hi
    mid = r.astype(BF16).astype(F32)
    lo = (r - mid).astype(BF16).astype(F32)
    return hi, mid, lo


def _proj_in_kernel(h_ref, rows_ref, tabs_ref, wx_ref, wq_ref, wkv_ref, sel_ref, tri_ref,
                    q_ref, k_ref, vt_ref, carry_ref, *, q_lora, kv_lora):
    tm = h_ref.shape[1]
    n_sub = vt_ref.shape[1]
    bk = vt_ref.shape[3]
    width = N_HEADS * HEAD_PAD
    nv = N_HEADS * HEAD_V

    @pl.when(pl.program_id(1) == 0)
    def _():
        carry_ref[...] = jnp.zeros_like(carry_ref)

    xn = _rms(h_ref[0], rows_ref[ROW_MIX_G:ROW_MIX_G + 1, :]).astype(BF16)

    o_ckv = q_lora
    o_kf = o_ckv + kv_lora
    o_fq = o_kf + LANE
    o_fk = o_fq + nv
    o_fv = o_fk + nv

    cos_q = tabs_ref[:, 0:LANE]
    sin_q = tabs_ref[:, LANE:2 * LANE]
    cos_k = tabs_ref[:, 2 * LANE:3 * LANE]
    sin_k = tabs_ref[:, 3 * LANE:4 * LANE]

    def put_values(v, row0):
        vt = v.T.astype(BF16)
        for c in range(n_sub):
            vt_ref[0, c, row0:row0 + nv, :] = vt[:, c * bk:(c + 1) * bk]

    lane = lax.broadcasted_iota(jnp.int32, (tm, LANE), 1)
    low = lane < HEAD_V

    def put_head_pairs(out_ref, col0, narrow, fill):
        for pair in range(N_HEADS // 2):
            blk = narrow[:, pair * LANE:(pair + 1) * LANE]
            other = fill(pair)
            c = col0 + 2 * pair * HEAD_PAD
            out_ref[0, :, c:c + HEAD_PAD] = jnp.where(low, blk, other).astype(BF16)
            odd = pltpu.roll(jnp.where(low, other, blk), HEAD_V, axis=1)
            out_ref[0, :, c + HEAD_PAD:c + 2 * HEAD_PAD] = odd.astype(BF16)

    cq = _mm(xn, wx_ref[:, 0:q_lora])
    ckv = _mm(xn, wx_ref[:, o_ckv:o_kf])
    o_bias = q_lora + kv_lora
    kf = _mm(xn, wx_ref[:, o_kf:o_fq])
    f3 = kf + rows_ref[ROW_LATENT:ROW_LATENT + 1, o_bias:o_bias + LANE]
    put_values(_mm(xn, wx_ref[:, o_fv:o_fv + nv]), nv)

    cqn = _rms(cq, rows_ref[ROW_LATENT:ROW_LATENT + 1, 0:q_lora]).astype(BF16)
    qa = _mm(cqn, wq_ref[...])
    first_half = lane < NOPE + ROPE // 2

    def rotary(x, cos, sin):
        swapped = jnp.where(first_half, pltpu.roll(x, LANE - ROPE // 2, axis=1),
                            pltpu.roll(x, ROPE // 2, axis=1))
        return x * cos + swapped * sin

    for hd in range(N_HEADS):
        sl = slice(hd * HEAD_PAD, (hd + 1) * HEAD_PAD)
        q_ref[0, :, sl] = rotary(qa[:, sl], cos_q, sin_q).astype(BF16)

    ckvn = _rms(ckv, rows_ref[ROW_LATENT:ROW_LATENT + 1, q_lora:q_lora + kv_lora]).astype(BF16)
    kn = _mm(ckvn, wkv_ref[:, 0:nv])
    kr = rotary(kf, cos_k, sin_k)
    kr_both = kr + pltpu.roll(kr, HEAD_V, axis=1)
    put_head_pairs(k_ref, 0, kn, lambda pair: kr_both)
    put_values(_mm(ckvn, wkv_ref[:, nv:2 * nv]), 0)

    live = lane < N_SPLIT * N_HEADS
    log_f = jnp.minimum(f3, 0.0) - jnp.log1p(jnp.exp(-jnp.abs(f3)))
    log_f = jnp.where(live, log_f, 0.0)
    pieces = jnp.concatenate([p.astype(BF16) for p in _split3(log_f)], axis=1)
    csum = _mm(tri_ref[...], pieces)
    fq = _mm(xn, wx_ref[:, o_fq:o_fk])
    cum = csum[:, 0:LANE] + csum[:, LANE:2 * LANE] + csum[:, 2 * LANE:3 * LANE] + carry_ref[...]
    carry_ref[...] = cum[tm - 1:tm, :]
    c_hi, c_mid, c_lo = _split3(cum * LOG2E)
    c_sel = jnp.where(lane < N_HEADS, c_hi, jnp.where(lane < 2 * N_HEADS, c_mid, c_lo))
    c_sel = jnp.where(live, c_sel, 0.0).astype(BF16)
    fk = _mm(xn, wx_ref[:, o_fk:o_fv])
    aug = _mm(c_sel, sel_ref[...])
    aug = aug + rows_ref[ROW_DECAY_ONES:ROW_DECAY_ONES + 1, :]
    put_head_pairs(q_ref, width, fq, lambda pair: aug[:, pair * LANE:(pair + 1) * LANE])
    put_head_pairs(k_ref, width, fk, lambda pair: aug[:, nv + pair * LANE:nv + (pair + 1) * LANE])


def _attn_kernel(q_ref, k_ref, vt_ref, o_ref, s_scr, p_scr, mt_scr, al_scr, m_scr, acc_scr, *, bq):
    bk = vt_ref.shape[3]
    n_sub = bq // bk
    nq = q_ref.shape[1] // bq
    heads = [slice(hh * HEAD_PAD, (hh + 1) * HEAD_PAD) for hh in range(2)]

    def k_tile(j):
        return k_ref[0, pl.ds(pl.multiple_of(j * bk, bk), bk), :]

    def q_rows(qi, first, hh):
        return q_ref[0, pl.ds(pl.multiple_of(qi * bq + first, bk), bq - first), heads[hh]]

    def put_scores(u, hh, s):
        s_scr[u, hh, :, 0:bq] = s
        mt_scr[u, hh] = jnp.max(s, axis=0, keepdims=True)

    def scores_group(g, qi):
        for u in range(n_sub):
            kb = k_tile(n_sub * g + u)
            for hh in range(2):
                put_scores(u, hh, _mm_nt(kb[:, heads[hh]], q_rows(qi, 0, hh)))

    def scores_diagonal(qi):
        tri = (lax.broadcasted_iota(jnp.int32, (bk, bk), 0)
               <= lax.broadcasted_iota(jnp.int32, (bk, bk), 1))
        for u in range(n_sub):
            kb = k_tile(n_sub * qi + u)
            for hh in range(2):
                s = _mm_nt(kb[:, heads[hh]], q_rows(qi, u * bk, hh))
                parts = [jnp.full((bk, u * bk), MASKED, F32)] if u > 0 else []
                parts.append(jnp.where(tri, s[:, :bk], MASKED))
                if u < n_sub - 1:
                    parts.append(s[:, bk:])
                put_scores(u, hh, jnp.concatenate(parts, axis=1) if len(parts) > 1 else parts[0])

    def softmax_group():
        for hh in range(2):
            mt = mt_scr[0, hh]
            for u in range(1, n_sub):
                mt = jnp.maximum(mt, mt_scr[u, hh])
            m_new = jnp.maximum(m_scr[hh], mt)
            al_scr[hh] = jnp.exp2(m_scr[hh] - m_new)
            m_scr[hh] = m_new
            for u in range(n_sub):
                p_scr[u, hh, :, 0:bq] = jnp.exp2(s_scr[u, hh, :, 0:bq] - m_new).astype(BF16)

    def values_at(j0):
        ones = jnp.ones((ONES_ROWS, bk), BF16)
        for hh in range(2):
            pv = None
            for u in range(n_sub):
                vb = vt_ref[0, j0 + u]
                v = jnp.concatenate([vb[hh * HEAD_V:(hh + 1) * HEAD_V, :], ones], axis=0)
                d = _mm(v, p_scr[u, hh, :, 0:bq])
                pv = d if pv is None else pv + d
            acc_scr[hh] = al_scr[hh] * acc_scr[hh] + pv

    def last_group_start(qi):
        return jnp.where(qi == 0, 0, n_sub * (qi - 1))

    def reset():
        m_scr[...] = jnp.full_like(m_scr, M_INIT)
        acc_scr[...] = jnp.zeros_like(acc_scr)

    def finish(qi):
        ot = jnp.concatenate(
            [acc_scr[hh, 0:HEAD_V] * (1.0 / acc_scr[hh, HEAD_V:HEAD_V + 1]) for hh in range(2)], axis=0)
        o_ref[0, pl.ds(pl.multiple_of(qi * bq, bq), bq), :] = ot.T
        reset()

    def below_diagonal(qi):
        def body(g, carry):
            softmax_group()
            scores_group(g, qi)
            values_at(jnp.where(g == 0, n_sub * qi, n_sub * (g - 1)))
            return carry

        lax.fori_loop(0, qi, body, 0)

    reset()
    scores_diagonal(0)

    def block(qi, carry):
        softmax_group()
        scores_diagonal(qi)
        values_at(last_group_start(qi - 1))
        finish(qi - 1)
        below_diagonal(qi)
        return carry

    lax.fori_loop(1, nq, block, 0)
    softmax_group()
    values_at(last_group_start(nq - 1))
    finish(nq - 1)


def _post_attn_kernel(o_ref, h_ref, rows_ref, kv_ref, wo_ref, wmq_ref, wmo_ref, out_ref):
    half = o_ref.shape[2] // 2
    o = o_ref[0]
    merged = jnp.concatenate(
        [_rms(o[:, 0:half], rows_ref[ROW_OUT_G:ROW_OUT_G + 1, 0:half]),
         _rms(o[:, half:], rows_ref[ROW_OUT_G:ROW_OUT_G + 1, half:])],
        axis=1).astype(BF16)
    h1 = h_ref[0] + _mm(merged, wo_ref[...])
    xn = _rms(h1, rows_ref[ROW_MEM_Q_G:ROW_MEM_Q_G + 1, :]).astype(BF16)
    q = (_mm(xn, wmq_ref[...]) * (MEM_HD ** -0.5)).astype(BF16)
    kv = kv_ref[0]
    heads = []
    for hd in range(N_MEM_HEADS):
        kh = kv[:, 2 * hd * MEM_HD:(2 * hd + 1) * MEM_HD]
        vh = kv[:, (2 * hd + 1) * MEM_HD:(2 * hd + 2) * MEM_HD]
        s = _mm_nt(q[:, hd * MEM_HD:(hd + 1) * MEM_HD], kh)
        e = jnp.exp(s - jnp.max(s, axis=-1, keepdims=True))
        l = jnp.sum(e, axis=-1, keepdims=True)
        heads.append(_mm(e.astype(BF16), vh) * (1.0 / l))
    om = jnp.concatenate(heads, axis=1).astype(BF16)
    out_ref[0] = h1 + _mm(om, wmo_ref[...])


def _mem_kv_kernel(mem_ref, rows_ref, w_ref, kv_ref):
    g = rows_ref[ROW_MEM_KV_G:ROW_MEM_KV_G + 1, :]
    kv_ref[0] = _mm(_rms(mem_ref[0], g).astype(BF16), w_ref[...]).astype(BF16)


def _ffn_kernel(h_ref, rows_ref, wg_ref, wu_ref, wd_ref, out_ref, *, final_norm):
    h = h_ref[0]
    xn = _rms(h, rows_ref[ROW_FFN_G:ROW_FFN_G + 1, :]).astype(BF16)
    g = _mm(xn, wg_ref[...])
    u = _mm(xn, wu_ref[...])
    a = (g * (1.0 / (1.0 + jnp.exp(-g))) * u).astype(BF16)
    y = h + _mm(a, wd_ref[...])
    if final_norm:
        y = _rms(y, rows_ref[ROW_FINAL_G:ROW_FINAL_G + 1, :])
    out_ref[0] = y


def _const_spec(shape):
    return pl.BlockSpec(shape, lambda *_: (0,) * len(shape), pipeline_mode=pl.Buffered(1))


def _layer_spec(shape, l):
    return pl.BlockSpec((None,) + tuple(shape), lambda *_: (l,) + (0,) * len(shape),
                        pipeline_mode=pl.Buffered(1))


def _decay_constants():
    n_half = N_HEADS * HEAD_V
    sel = np.zeros((LANE, 2 * n_half), np.float32)
    ones = np.zeros((2 * n_half,), np.float32)
    for hd in range(N_HEADS):
        base = (hd // 2) * LANE + (HEAD_V if hd % 2 == 0 else 0)
        for part in range(N_SPLIT):
            src = part * N_HEADS + hd
            sel[src, base + part] = 1.0
            ones[base + N_SPLIT + part] = 1.0
            ones[n_half + base + part] = 1.0
            sel[src, n_half + base + N_SPLIT + part] = -1.0
    return sel, ones


def _rope_tables(seq):
    inv = 1.0 / (ROPE_THETA ** (jnp.arange(0, ROPE, 2, dtype=F32) / ROPE))
    ang = jnp.arange(seq, dtype=F32)[:, None] * inv[None, :]
    cos, sin = jnp.cos(ang), jnp.sin(ang)
    cc = jnp.concatenate([cos, cos], axis=1)
    ss = jnp.concatenate([-sin, sin], axis=1)
    ones = jnp.ones((seq, NOPE), F32)
    z_lo = jnp.zeros((seq, NOPE), F32)
    z_hi = jnp.zeros((seq, HEAD_PAD - NOPE - ROPE), F32)
    scale = LOG2E * (NOPE + ROPE) ** -0.5
    return jnp.concatenate([
        scale * jnp.concatenate([ones, cc, z_hi], axis=1),
        scale * jnp.concatenate([z_lo, ss, z_hi], axis=1),
        jnp.concatenate([z_lo, cc, z_hi], axis=1),
        jnp.concatenate([z_lo, ss, z_hi], axis=1)], axis=1)


def _mixer_weights(w_in, w_uq, w_ukv, q_lora, kv_lora):
    depth = w_in.shape[0]
    fox = N_HEADS * HEAD_V
    o = q_lora + kv_lora
    kr = w_in[:, :, o:o + ROPE]
    o += ROPE
    wfq, wfk, wfv = w_in[:, :, o:o + fox], w_in[:, :, o + fox:o + 2 * fox], w_in[:, :, o + 2 * fox:o + 3 * fox]
    wfl = w_in[:, :, o + 3 * fox:o + 3 * fox + N_HEADS]

    zeros = lambda n: jnp.zeros(w_in.shape[:2] + (n,), F32)
    gate_and_rope = jnp.concatenate(
        [wfl] * N_SPLIT + [zeros(NOPE - N_SPLIT * N_HEADS), kr, zeros(HEAD_PAD - NOPE - ROPE)], axis=2)
    wx = jnp.concatenate([
        w_in[:, :, 0:q_lora + kv_lora], gate_and_rope,
        wfq * (LOG2E * HEAD_V ** -0.5), wfk, wfv], axis=2).astype(BF16)

    uq = w_uq.reshape(depth, q_lora, N_HEADS, NOPE + ROPE)
    pad_hi = jnp.zeros((depth, q_lora, N_HEADS, HEAD_PAD - NOPE - ROPE), F32)
    wq = jnp.concatenate([uq, pad_hi], axis=3).reshape(depth, q_lora, -1).astype(BF16)

    ukv = w_ukv.reshape(depth, kv_lora, N_HEADS, NOPE + HEAD_V)
    wkv = jnp.concatenate([ukv[..., :NOPE].reshape(depth, kv_lora, -1),
                           ukv[..., NOPE:].reshape(depth, kv_lora, -1)], axis=2).astype(BF16)
    return wx, wq, wkv


def kernel(x, mem, mix_norm_g, w_in, cq_norm_g, ckv_norm_g, w_uq, w_ukv, forget_bias, mla_out_g, fox_out_g, w_out, mem_q_norm_g, mem_kv_norm_g, w_mq, w_mkv, w_mo, ffn_norm_g, w_gate, w_up, w_down, final_norm_g):
    bsz, seq, d_model = x.shape
    depth = w_in.shape[0]
    q_lora = cq_norm_g.shape[1]
    kv_lora = ckv_norm_g.shape[1]
    mem_len = mem.shape[1]
    d_ff = w_gate.shape[2]
    width = N_HEADS * HEAD_PAD
    n_v = 2 * N_HEADS * HEAD_V
    assert d_model == width == n_v, "layout assumes d_model = 8 heads * 128"
    assert seq % BQ == 0 and seq % TM_PROJ == 0 and TM_PROJ % BK == 0 and BQ % BK == 0
    assert seq % TM_POST == 0 and seq % TM_FFN == 0

    tabs = _rope_tables(seq)
    sel_np, decay_ones = _decay_constants()
    sel = jnp.asarray(sel_np, BF16)
    tri = jnp.asarray(np.tril(np.ones((TM_PROJ, TM_PROJ), np.float32)), BF16)
    params = pltpu.CompilerParams

    wx, wq, wkv = _mixer_weights(w_in, w_uq, w_ukv, q_lora, kv_lora)
    w_out_b, w_mq_b, w_mkv_b, w_mo_b = (w.astype(BF16) for w in (w_out, w_mq, w_mkv, w_mo))
    w_gate_b, w_up_b, w_down_b = (w.astype(BF16) for w in (w_gate, w_up, w_down))
    latent = jnp.pad(jnp.concatenate([cq_norm_g, ckv_norm_g] + [forget_bias] * N_SPLIT, axis=1),
                     ((0, 0), (0, d_model - q_lora - kv_lora - N_SPLIT * N_HEADS)))
    per_layer = lambda v: jnp.broadcast_to(v, (depth, d_model))
    rows = jnp.stack([mix_norm_g, latent, per_layer(jnp.asarray(decay_ones)),
                      jnp.concatenate([mla_out_g, fox_out_g], axis=1), mem_q_norm_g, ffn_norm_g,
                      per_layer(final_norm_g), mem_kv_norm_g], axis=1)
    rows_spec = lambda l: _layer_spec((8, d_model), l)

    h = x
    for l in range(depth):
        q_all, k_all, vt_all = pl.pallas_call(
            functools.partial(_proj_in_kernel, q_lora=q_lora, kv_lora=kv_lora),
            grid=(bsz, seq // TM_PROJ),
            in_specs=[
                pl.BlockSpec((1, TM_PROJ, d_model), lambda b, t: (b, t, 0)),
                rows_spec(l),
                pl.BlockSpec((TM_PROJ, 4 * LANE), lambda b, t: (t, 0)),
                _layer_spec(wx.shape[1:], l), _layer_spec(wq.shape[1:], l), _layer_spec(wkv.shape[1:], l),
                _const_spec(sel.shape), _const_spec(tri.shape),
            ],
            out_specs=[
                pl.BlockSpec((1, TM_PROJ, 2 * width), lambda b, t: (b, t, 0)),
                pl.BlockSpec((1, TM_PROJ, 2 * width), lambda b, t: (b, t, 0)),
                pl.BlockSpec((1, TM_PROJ // BK, n_v, BK), lambda b, t: (b, t, 0, 0)),
            ],
            out_shape=[
                jax.ShapeDtypeStruct((bsz, seq, 2 * width), BF16),
                jax.ShapeDtypeStruct((bsz, seq, 2 * width), BF16),
                jax.ShapeDtypeStruct((bsz, seq // BK, n_v, BK), BF16),
            ],
            scratch_shapes=[pltpu.VMEM((1, LANE), F32)],
            compiler_params=params(dimension_semantics=("arbitrary", "arbitrary"),
                                   vmem_limit_bytes=VMEM_LIMIT),
            name=f"proj_in_{l}",
        )(h, rows, tabs, wx, wq, wkv, sel, tri)

        o_all = pl.pallas_call(
            functools.partial(_attn_kernel, bq=BQ),
            grid=(bsz, N_HEADS),
            in_specs=[
                pl.BlockSpec((1, seq, 2 * HEAD_PAD), lambda b, p: (b, 0, p)),
                pl.BlockSpec((1, seq, 2 * HEAD_PAD), lambda b, p: (b, 0, p)),
                pl.BlockSpec((1, seq // BK, 2 * HEAD_V, BK), lambda b, p: (b, 0, p, 0)),
            ],
            out_specs=pl.BlockSpec((1, seq, 2 * HEAD_V), lambda b, p: (b, 0, p)),
            out_shape=jax.ShapeDtypeStruct((bsz, seq, n_v), F32),
            scratch_shapes=[
                pltpu.VMEM((BQ // BK, 2, BK, BQ + LANE), F32),
                pltpu.VMEM((BQ // BK, 2, BK, BQ + LANE), BF16),
                pltpu.VMEM((BQ // BK, 2, 1, BQ), F32),
                pltpu.VMEM((2, 1, BQ), F32),
                pltpu.VMEM((2, 1, BQ), F32),
                pltpu.VMEM((2, HEAD_V + ONES_ROWS, BQ), F32),
            ],
            compiler_params=params(dimension_semantics=("parallel", "parallel"),
                                   vmem_limit_bytes=VMEM_LIMIT),
            name=f"attn_{l}",
        )(q_all, k_all, vt_all)

        kv_mem = pl.pallas_call(
            _mem_kv_kernel,
            grid=(bsz,),
            in_specs=[
                pl.BlockSpec((1, mem_len, d_model), lambda b: (b, 0, 0)),
                rows_spec(l),
                _layer_spec(w_mkv.shape[1:], l),
            ],
            out_specs=pl.BlockSpec((1, mem_len, w_mkv.shape[2]), lambda b: (b, 0, 0)),
            out_shape=jax.ShapeDtypeStruct((bsz, mem_len, w_mkv.shape[2]), BF16),
            compiler_params=params(dimension_semantics=("parallel",), vmem_limit_bytes=VMEM_LIMIT),
            name=f"mem_kv_{l}",
        )(mem, rows, w_mkv_b)

        h = pl.pallas_call(
            _post_attn_kernel,
            grid=(bsz, seq // TM_POST),
            in_specs=[
                pl.BlockSpec((1, TM_POST, n_v), lambda b, t: (b, t, 0)),
                pl.BlockSpec((1, TM_POST, d_model), lambda b, t: (b, t, 0)),
                rows_spec(l),
                pl.BlockSpec((1, mem_len, w_mkv.shape[2]), lambda b, t: (b, 0, 0)),
                _layer_spec(w_out.shape[1:], l), _layer_spec(w_mq.shape[1:], l), _layer_spec(w_mo.shape[1:], l),
            ],
            out_specs=pl.BlockSpec((1, TM_POST, d_model), lambda b, t: (b, t, 0)),
            out_shape=jax.ShapeDtypeStruct((bsz, seq, d_model), F32),
            compiler_params=params(dimension_semantics=("parallel", "parallel"),
                                   vmem_limit_bytes=VMEM_LIMIT),
            name=f"post_attn_{l}",
        )(o_all, h, rows, kv_mem, w_out_b, w_mq_b, w_mo_b)

        last = l == depth - 1
        h = pl.pallas_call(
            functools.partial(_ffn_kernel, final_norm=last),
            grid=(bsz, seq // TM_FFN),
            in_specs=[
                pl.BlockSpec((1, TM_FFN, d_model), lambda b, t: (b, t, 0)),
                rows_spec(l),
                _layer_spec((d_model, d_ff), l), _layer_spec((d_model, d_ff), l), _layer_spec((d_ff, d_model), l),
            ],
            out_specs=pl.BlockSpec((1, TM_FFN, d_model), lambda b, t: (b, t, 0)),
            out_shape=jax.ShapeDtypeStruct((bsz, seq, d_model), F32),
            compiler_params=params(dimension_semantics=("parallel", "parallel"),
                                   vmem_limit_bytes=VMEM_LIMIT),
            name=f"ffn_{l}",
        )(h, rows, w_gate_b, w_up_b, w_down_b)
    return h
```

```python
import functools

import numpy as np
import jax
import jax.numpy as jnp
from jax import lax
from jax.experimental import pallas as pl
from jax.experimental.pallas import tpu as pltpu

F32 = jnp.float32
BF16 = jnp.bfloat16

EPS = 1e-6
ROPE_THETA = 10000.0
N_HEADS = 8
NOPE = 64
ROPE = 32
HEAD_V = 64
HEAD_PAD = 128
N_MEM_HEADS = 4
MEM_HD = 128
N_SPLIT = 3
ONES_ROWS = 16

LANE = 128
TM_PROJ = 512
TM_POST = 1024
TM_FFN = 512
BQ = 1024
BK = 256
MASKED = -2e30
M_INIT = -1e30
LOG2E = 1.4426950408889634
VMEM_LIMIT = 56 * 1024 * 1024

ROW_MIX_G, ROW_LATENT, ROW_DECAY_ONES, ROW_OUT_G, ROW_MEM_Q_G, ROW_FFN_G, ROW_FINAL_G, ROW_MEM_KV_G = range(8)


def _mm(a, b):
    return jnp.dot(a, b, preferred_element_type=F32)


def _mm_nt(a, b):
    return lax.dot_general(a, b, (((1,), (1,)), ((), ())), preferred_element_type=F32)


def _rms(x, g):
    return x * lax.rsqrt(jnp.mean(x * x, axis=-1, keepdims=True) + EPS) * g


def _split3(x):
    hi = x.astype(BF16).astype(F32)
    r = x - hi
    mid = r.astype(BF16).astype(F32)
    lo = (r - mid).astype(BF16).astype(F32)
    return hi, mid, lo


def _proj_in_kernel(h_ref, rows_ref, tabs_ref, wx_ref, wq_ref, wkv_ref, sel_ref, tri_ref,
                    q_ref, k_ref, vt_ref, carry_ref, *, q_lora, kv_lora):
    tm = h_ref.shape[1]
    n_sub = vt_ref.shape[1]
    bk = vt_ref.shape[3]
    width = N_HEADS * HEAD_PAD
    nv = N_HEADS * HEAD_V

    @pl.when(pl.program_id(1) == 0)
    def _():
        carry_ref[...] = jnp.zeros_like(carry_ref)

    xn = _rms(h_ref[0], rows_ref[ROW_MIX_G:ROW_MIX_G + 1, :]).astype(BF16)

    o_ckv = q_lora
    o_kf = o_ckv + kv_lora
    o_fq = o_kf + LANE
    o_fk = o_fq + nv
    o_fv = o_fk + nv

    cos_q = tabs_ref[:, 0:LANE]
    sin_q = tabs_ref[:, LANE:2 * LANE]
    cos_k = tabs_ref[:, 2 * LANE:3 * LANE]
    sin_k = tabs_ref[:, 3 * LANE:4 * LANE]

    def put_values(v, row0):
        vt = v.T.astype(BF16)
        for c in range(n_sub):
            vt_ref[0, c, row0:row0 + nv, :] = vt[:, c * bk:(c + 1) * bk]

    lane = lax.broadcasted_iota(jnp.int32, (tm, LANE), 1)
    low = lane < HEAD_V

    def put_head_pairs(out_ref, col0, narrow, fill):
        for pair in range(N_HEADS // 2):
            blk = narrow[:, pair * LANE:(pair + 1) * LANE]
            other = fill(pair)
            c = col0 + 2 * pair * HEAD_PAD
            out_ref[0, :, c:c + HEAD_PAD] = jnp.where(low, blk, other).astype(BF16)
            odd = pltpu.roll(jnp.where(low, other, blk), HEAD_V, axis=1)
            out_ref[0, :, c + HEAD_PAD:c + 2 * HEAD_PAD] = odd.astype(BF16)

    cq = _mm(xn, wx_ref[:, 0:q_lora])
    ckv = _mm(xn, wx_ref[:, o_ckv:o_kf])
    o_bias = q_lora + kv_lora
    kf = _mm(xn, wx_ref[:, o_kf:o_fq])
    f3 = kf + rows_ref[ROW_LATENT:ROW_LATENT + 1, o_bias:o_bias + LANE]
    put_values(_mm(xn, wx_ref[:, o_fv:o_fv + nv]), nv)

    cqn = _rms(cq, rows_ref[ROW_LATENT:ROW_LATENT + 1, 0:q_lora]).astype(BF16)
    qa = _mm(cqn, wq_ref[...])
    first_half = lane < NOPE + ROPE // 2

    def rotary(x, cos, sin):
        swapped = jnp.where(first_half, pltpu.roll(x, LANE - ROPE // 2, axis=1),
                            pltpu.roll(x, ROPE // 2, axis=1))
        return x * cos + swapped * sin

    for hd in range(N_HEADS):
        sl = slice(hd * HEAD_PAD, (hd + 1) * HEAD_PAD)
        q_ref[0, :, sl] = rotary(qa[:, sl], cos_q, sin_q).astype(BF16)

    ckvn = _rms(ckv, rows_ref[ROW_LATENT:ROW_LATENT + 1, q_lora:q_lora + kv_lora]).astype(BF16)
    kn = _mm(ckvn, wkv_ref[:, 0:nv])
    kr = rotary(kf, cos_k, sin_k)
    kr_both = kr + pltpu.roll(kr, HEAD_V, axis=1)
    put_head_pairs(k_ref, 0, kn, lambda pair: kr_both)
    put_values(_mm(ckvn, wkv_ref[:, nv:2 * nv]), 0)

    live = lane < N_SPLIT * N_HEADS
    log_f = jnp.minimum(f3, 0.0) - jnp.log1p(jnp.exp(-jnp.abs(f3)))
    log_f = jnp.where(live, log_f, 0.0)
    pieces = jnp.concatenate([p.astype(BF16) for p in _split3(log_f)], axis=1)
    csum = _mm(tri_ref[...], pieces)
    fq = _mm(xn, wx_ref[:, o_fq:o_fk])
    cum = csum[:, 0:LANE] + csum[:, LANE:2 * LANE] + csum[:, 2 * LANE:3 * LANE] + carry_ref[...]
    carry_ref[...] = cum[tm - 1:tm, :]
    c_hi, c_mid, c_lo = _split3(cum * LOG2E)
    c_sel = jnp.where(lane < N_HEADS, c_hi, jnp.where(lane < 2 * N_HEADS, c_mid, c_lo))
    c_sel = jnp.where(live, c_sel, 0.0).astype(BF16)
    fk = _mm(xn, wx_ref[:, o_fk:o_fv])
    aug = _mm(c_sel, sel_ref[...])
    aug = aug + rows_ref[ROW_DECAY_ONES:ROW_DECAY_ONES + 1, :]
    put_head_pairs(q_ref, width, fq, lambda pair: aug[:, pair * LANE:(pair + 1) * LANE])
    put_head_pairs(k_ref, width, fk, lambda pair: aug[:, nv + pair * LANE:nv + (pair + 1) * LANE])


def _attn_kernel(q_ref, k_ref, vt_ref, o_ref, s_scr, p_scr, mt_scr, al_scr, m_scr, acc_scr, *, bq):
    bk = vt_ref.shape[3]
    n_sub = bq // bk
    nq = q_ref.shape[1] // bq
    heads = [slice(hh * HEAD_PAD, (hh + 1) * HEAD_PAD) for hh in range(2)]

    def k_tile(j):
        return k_ref[0, pl.ds(pl.multiple_of(j * bk, bk), bk), :]

    def q_rows(qi, first, hh):
        return q_ref[0, pl.ds(pl.multiple_of(qi * bq + first, bk), bq - first), heads[hh]]

    def put_scores(u, hh, s):
        s_scr[u, hh, :, 0:bq] = s
        mt_scr[u, hh] = jnp.max(s, axis=0, keepdims=True)

    def scores_group(g, qi):
        for u in range(n_sub):
            kb = k_tile(n_sub * g + u)
            for hh in range(2):
                put_scores(u, hh, _mm_nt(kb[:, heads[hh]], q_rows(qi, 0, hh)))

    def scores_diagonal(qi):
        tri = (lax.broadcasted_iota(jnp.int32, (bk, bk), 0)
               <= lax.broadcasted_iota(jnp.int32, (bk, bk), 1))
        for u in range(n_sub):
            kb = k_tile(n_sub * qi + u)
            for hh in range(2):
                s = _mm_nt(kb[:, heads[hh]], q_rows(qi, u * bk, hh))
                parts = [jnp.full((bk, u * bk), MASKED, F32)] if u > 0 else []
                parts.append(jnp.where(tri, s[:, :bk], MASKED))
                if u < n_sub - 1:
                    parts.append(s[:, bk:])
                put_scores(u, hh, jnp.concatenate(parts, axis=1) if len(parts) > 1 else parts[0])

    def softmax_group():
        for hh in range(2):
            mt = mt_scr[0, hh]
            for u in range(1, n_sub):
                mt = jnp.maximum(mt, mt_scr[u, hh])
            m_new = jnp.maximum(m_scr[hh], mt)
            al_scr[hh] = jnp.exp2(m_scr[hh] - m_new)
            m_scr[hh] = m_new
            for u in range(n_sub):
                p_scr[u, hh, :, 0:bq] = jnp.exp2(s_scr[u, hh, :, 0:bq] - m_new).astype(BF16)

    def values_at(j0):
        ones = jnp.ones((ONES_ROWS, bk), BF16)
        for hh in range(2):
            pv = None
            for u in range(n_sub):
                vb = vt_ref[0, j0 + u]
                v = jnp.concatenate([vb[hh * HEAD_V:(hh + 1) * HEAD_V, :], ones], axis=0)
                d = _mm(v, p_scr[u, hh, :, 0:bq])
                pv = d if pv is None else pv + d
            acc_scr[hh] = al_scr[hh] * acc_scr[hh] + pv

    def last_group_start(qi):
        return jnp.where(qi == 0, 0, n_sub * (qi - 1))

    def reset():
        m_scr[...] = jnp.full_like(m_scr, M_INIT)
        acc_scr[...] = jnp.zeros_like(acc_scr)

    def finish(qi):
        ot = jnp.concatenate(
            [acc_scr[hh, 0:HEAD_V] * (1.0 / acc_scr[hh, HEAD_V:HEAD_V + 1]) for hh in range(2)], axis=0)
        o_ref[0, pl.ds(pl.multiple_of(qi * bq, bq), bq), :] = ot.T
        reset()

    def below_diagonal(qi):
        def body(g, carry):
            softmax_group()
            scores_group(g, qi)
            values_at(jnp.where(g == 0, n_sub * qi, n_sub * (g - 1)))
            return carry

        lax.fori_loop(0, qi, body, 0)

    reset()
    scores_diagonal(0)

    def block(qi, carry):
        softmax_group()
        scores_diagonal(qi)
        values_at(last_group_start(qi - 1))
        finish(qi - 1)
        below_diagonal(qi)
        return carry

    lax.fori_loop(1, nq, block, 0)
    softmax_group()
    values_at(last_group_start(nq - 1))
    finish(nq - 1)


def _post_attn_kernel(o_ref, h_ref, rows_ref, kv_ref, wo_ref, wmq_ref, wmo_ref, out_ref):
    half = o_ref.shape[2] // 2
    o = o_ref[0]
    merged = jnp.concatenate(
        [_rms(o[:, 0:half], rows_ref[ROW_OUT_G:ROW_OUT_G + 1, 0:half]),
         _rms(o[:, half:], rows_ref[ROW_OUT_G:ROW_OUT_G + 1, half:])],
        axis=1).astype(BF16)
    h1 = h_ref[0] + _mm(merged, wo_ref[...])
    xn = _rms(h1, rows_ref[ROW_MEM_Q_G:ROW_MEM_Q_G + 1, :]).astype(BF16)
    q = (_mm(xn, wmq_ref[...]) * (MEM_HD ** -0.5)).astype(BF16)
    kv = kv_ref[0]
    heads = []
    for hd in range(N_MEM_HEADS):
        kh = kv[:, 2 * hd * MEM_HD:(2 * hd + 1) * MEM_HD]
        vh = kv[:, (2 * hd + 1) * MEM_HD:(2 * hd + 2) * MEM_HD]
        s = _mm_nt(q[:, hd * MEM_HD:(hd + 1) * MEM_HD], kh)
        e = jnp.exp(s - jnp.max(s, axis=-1, keepdims=True))
        l = jnp.sum(e, axis=-1, keepdims=True)
        heads.append(_mm(e.astype(BF16), vh) * (1.0 / l))
    om = jnp.concatenate(heads, axis=1).astype(BF16)
    out_ref[0] = h1 + _mm(om, wmo_ref[...])


def _mem_kv_kernel(mem_ref, rows_ref, w_ref, kv_ref):
    g = rows_ref[ROW_MEM_KV_G:ROW_MEM_KV_G + 1, :]
    kv_ref[0] = _mm(_rms(mem_ref[0], g).astype(BF16), w_ref[...]).astype(BF16)


def _ffn_kernel(h_ref, rows_ref, wg_ref, wu_ref, wd_ref, out_ref, *, final_norm):
    half = h_ref.shape[1] // 2
    parts = [slice(0, half), slice(half, 2 * half)]
    hs = [h_ref[0, r, :] for r in parts]
    xn = [_rms(h, rows_ref[ROW_FFN_G:ROW_FFN_G + 1, :]).astype(BF16) for h in hs]
    gu = [(_mm(x, wg_ref[...]), _mm(x, wu_ref[...])) for x in xn]
    act = [(g * (1.0 / (1.0 + jnp.exp(-g))) * u).astype(BF16) for g, u in gu]
    for r, h, a in zip(parts, hs, act):
        y = h + _mm(a, wd_ref[...])
        if final_norm:
            y = _rms(y, rows_ref[ROW_FINAL_G:ROW_FINAL_G + 1, :])
        out_ref[0, r, :] = y


def _const_spec(shape):
    return pl.BlockSpec(shape, lambda *_: (0,) * len(shape), pipeline_mode=pl.Buffered(1))


def _layer_spec(shape, l):
    return pl.BlockSpec((None,) + tuple(shape), lambda *_: (l,) + (0,) * len(shape),
                        pipeline_mode=pl.Buffered(1))


def _decay_constants():
    n_half = N_HEADS * HEAD_V
    sel = np.zeros((LANE, 2 * n_half), np.float32)
    ones = np.zeros((2 * n_half,), np.float32)
    for hd in range(N_HEADS):
        base = (hd // 2) * LANE + (HEAD_V if hd % 2 == 0 else 0)
        for part in range(N_SPLIT):
            src = part * N_HEADS + hd
            sel[src, base + part] = 1.0
            ones[base + N_SPLIT + part] = 1.0
            ones[n_half + base + part] = 1.0
            sel[src, n_half + base + N_SPLIT + part] = -1.0
    return sel, ones


def _rope_tables(seq):
    inv = 1.0 / (ROPE_THETA ** (jnp.arange(0, ROPE, 2, dtype=F32) / ROPE))
    ang = jnp.arange(seq, dtype=F32)[:, None] * inv[None, :]
    cos, sin = jnp.cos(ang), jnp.sin(ang)
    cc = jnp.concatenate([cos, cos], axis=1)
    ss = jnp.concatenate([-sin, sin], axis=1)
    ones = jnp.ones((seq, NOPE), F32)
    z_lo = jnp.zeros((seq, NOPE), F32)
    z_hi = jnp.zeros((seq, HEAD_PAD - NOPE - ROPE), F32)
    scale = LOG2E * (NOPE + ROPE) ** -0.5
    return jnp.concatenate([
        scale * jnp.concatenate([ones, cc, z_hi], axis=1),
        scale * jnp.concatenate([z_lo, ss, z_hi], axis=1),
        jnp.concatenate([z_lo, cc, z_hi], axis=1),
        jnp.concatenate([z_lo, ss, z_hi], axis=1)], axis=1)


def _mixer_weights(w_in, w_uq, w_ukv, q_lora, kv_lora):
    depth = w_in.shape[0]
    fox = N_HEADS * HEAD_V
    o = q_lora + kv_lora
    kr = w_in[:, :, o:o + ROPE]
    o += ROPE
    wfq, wfk, wfv = w_in[:, :, o:o + fox], w_in[:, :, o + fox:o + 2 * fox], w_in[:, :, o + 2 * fox:o + 3 * fox]
    wfl = w_in[:, :, o + 3 * fox:o + 3 * fox + N_HEADS]

    zeros = lambda n: jnp.zeros(w_in.shape[:2] + (n,), BF16)
    cast = lambda w: w.astype(BF16)
    wx = jnp.concatenate(
        [cast(w_in[:, :, 0:q_lora + kv_lora])] + [cast(wfl)] * N_SPLIT
        + [zeros(NOPE - N_SPLIT * N_HEADS), cast(kr), zeros(HEAD_PAD - NOPE - ROPE),
           cast(wfq * (LOG2E * HEAD_V ** -0.5)), cast(wfk), cast(wfv)], axis=2)

    uq = w_uq.reshape(depth, q_lora, N_HEADS, NOPE + ROPE)
    pad_hi = jnp.zeros((depth, q_lora, N_HEADS, HEAD_PAD - NOPE - ROPE), F32)
    wq = jnp.concatenate([uq, pad_hi], axis=3).reshape(depth, q_lora, -1).astype(BF16)

    ukv = w_ukv.reshape(depth, kv_lora, N_HEADS, NOPE + HEAD_V)
    wkv = jnp.concatenate([ukv[..., :NOPE].reshape(depth, kv_lora, -1),
                           ukv[..., NOPE:].reshape(depth, kv_lora, -1)], axis=2).astype(BF16)
    return wx, wq, wkv


def kernel(x, mem, mix_norm_g, w_in, cq_norm_g, ckv_norm_g, w_uq, w_ukv, forget_bias, mla_out_g, fox_out_g, w_out, mem_q_norm_g, mem_kv_norm_g, w_mq, w_mkv, w_mo, ffn_norm_g, w_gate, w_up, w_down, final_norm_g):
    bsz, seq, d_model = x.shape
    depth = w_in.shape[0]
    q_lora = cq_norm_g.shape[1]
    kv_lora = ckv_norm_g.shape[1]
    mem_len = mem.shape[1]
    d_ff = w_gate.shape[2]
    width = N_HEADS * HEAD_PAD
    n_v = 2 * N_HEADS * HEAD_V
    assert d_model == width == n_v, "layout assumes d_model = 8 heads * 128"
    assert seq % BQ == 0 and seq % TM_PROJ == 0 and TM_PROJ % BK == 0 and BQ % BK == 0
    assert seq % TM_POST == 0 and seq % TM_FFN == 0

    tabs = _rope_tables(seq)
    sel_np, decay_ones = _decay_constants()
    sel = jnp.asarray(sel_np, BF16)
    tri = jnp.asarray(np.tril(np.ones((TM_PROJ, TM_PROJ), np.float32)), BF16)
    params = pltpu.CompilerParams

    wx, wq, wkv = _mixer_weights(w_in, w_uq, w_ukv, q_lora, kv_lora)
    w_out_b, w_mq_b, w_mkv_b, w_mo_b = (w.astype(BF16) for w in (w_out, w_mq, w_mkv, w_mo))
    w_gate_b, w_up_b, w_down_b = (w.astype(BF16) for w in (w_gate, w_up, w_down))
    latent = jnp.pad(jnp.concatenate([cq_norm_g, ckv_norm_g] + [forget_bias] * N_SPLIT, axis=1),
                     ((0, 0), (0, d_model - q_lora - kv_lora - N_SPLIT * N_HEADS)))
    per_layer = lambda v: jnp.broadcast_to(v, (depth, d_model))
    rows = jnp.stack([mix_norm_g, latent, per_layer(jnp.asarray(decay_ones)),
                      jnp.concatenate([mla_out_g, fox_out_g], axis=1), mem_q_norm_g, ffn_norm_g,
                      per_layer(final_norm_g), mem_kv_norm_g], axis=1)
    rows_spec = lambda l: _layer_spec((8, d_model), l)

    h = x
    for l in range(depth):
        q_all, k_all, vt_all = pl.pallas_call(
            functools.partial(_proj_in_kernel, q_lora=q_lora, kv_lora=kv_lora),
            grid=(bsz, seq // TM_PROJ),
            in_specs=[
                pl.BlockSpec((1, TM_PROJ, d_model), lambda b, t: (b, t, 0)),
                rows_spec(l),
                pl.BlockSpec((TM_PROJ, 4 * LANE), lambda b, t: (t, 0)),
                _layer_spec(wx.shape[1:], l), _layer_spec(wq.shape[1:], l), _layer_spec(wkv.shape[1:], l),
                _const_spec(sel.shape), _const_spec(tri.shape),
            ],
            out_specs=[
                pl.BlockSpec((1, TM_PROJ, 2 * width), lambda b, t: (b, t, 0)),
                pl.BlockSpec((1, TM_PROJ, 2 * width), lambda b, t: (b, t, 0)),
                pl.BlockSpec((1, TM_PROJ // BK, n_v, BK), lambda b, t: (b, t, 0, 0)),
            ],
            out_shape=[
                jax.ShapeDtypeStruct((bsz, seq, 2 * width), BF16),
                jax.ShapeDtypeStruct((bsz, seq, 2 * width), BF16),
                jax.ShapeDtypeStruct((bsz, seq // BK, n_v, BK), BF16),
            ],
            scratch_shapes=[pltpu.VMEM((1, LANE), F32)],
            compiler_params=params(dimension_semantics=("arbitrary", "arbitrary"),
                                   vmem_limit_bytes=VMEM_LIMIT),
            name=f"proj_in_{l}",
        )(h, rows, tabs, wx, wq, wkv, sel, tri)

        o_all = pl.pallas_call(
            functools.partial(_attn_kernel, bq=BQ),
            grid=(bsz, N_HEADS),
            in_specs=[
                pl.BlockSpec((1, seq, 2 * HEAD_PAD), lambda b, p: (b, 0, p)),
                pl.BlockSpec((1, seq, 2 * HEAD_PAD), lambda b, p: (b, 0, p)),
                pl.BlockSpec((1, seq // BK, 2 * HEAD_V, BK), lambda b, p: (b, 0, p, 0)),
            ],
            out_specs=pl.BlockSpec((1, seq, 2 * HEAD_V), lambda b, p: (b, 0, p)),
            out_shape=jax.ShapeDtypeStruct((bsz, seq, n_v), F32),
            scratch_shapes=[
                pltpu.VMEM((BQ // BK, 2, BK, BQ + LANE), F32),
                pltpu.VMEM((BQ // BK, 2, BK, BQ + LANE), BF16),
                pltpu.VMEM((BQ // BK, 2, 1, BQ), F32),
                pltpu.VMEM((2, 1, BQ), F32),
                pltpu.VMEM((2, 1, BQ), F32),
                pltpu.VMEM((2, HEAD_V + ONES_ROWS, BQ), F32),
            ],
            compiler_params=params(dimension_semantics=("parallel", "parallel"),
                                   vmem_limit_bytes=VMEM_LIMIT),
            name=f"attn_{l}",
        )(q_all, k_all, vt_all)

        kv_mem = pl.pallas_call(
            _mem_kv_kernel,
            grid=(bsz,),
            in_specs=[
                pl.BlockSpec((1, mem_len, d_model), lambda b: (b, 0, 0)),
                rows_spec(l),
                _layer_spec(w_mkv.shape[1:], l),
            ],
            out_specs=pl.BlockSpec((1, mem_len, w_mkv.shape[2]), lambda b: (b, 0, 0)),
            out_shape=jax.ShapeDtypeStruct((bsz, mem_len, w_mkv.shape[2]), BF16),
            compiler_params=params(dimension_semantics=("parallel",), vmem_limit_bytes=VMEM_LIMIT),
            name=f"mem_kv_{l}",
        )(mem, rows, w_mkv_b)

        h = pl.pallas_call(
            _post_attn_kernel,
            grid=(bsz, seq // TM_POST),
            in_specs=[
                pl.BlockSpec((1, TM_POST, n_v), lambda b, t: (b, t, 0)),
                pl.BlockSpec((1, TM_POST, d_model), lambda b, t: (b, t, 0)),
                rows_spec(l),
                pl.BlockSpec((1, mem_len, w_mkv.shape[2]), lambda b, t: (b, 0, 0)),
                _layer_spec(w_out.shape[1:], l), _layer_spec(w_mq.shape[1:], l), _layer_spec(w_mo.shape[1:], l),
            ],
            out_specs=pl.BlockSpec((1, TM_POST, d_model), lambda b, t: (b, t, 0)),
            out_shape=jax.ShapeDtypeStruct((bsz, seq, d_model), F32),
            compiler_params=params(dimension_semantics=("parallel", "parallel"),
                                   vmem_limit_bytes=VMEM_LIMIT),
            name=f"post_attn_{l}",
        )(o_all, h, rows, kv_mem, w_out_b, w_mq_b, w_mo_b)

        last = l == depth - 1
        h = pl.pallas_call(
            functools.partial(_ffn_kernel, final_norm=last),
            grid=(bsz, seq // TM_FFN),
            in_specs=[
                pl.BlockSpec((1, TM_FFN, d_model), lambda b, t: (b, t, 0)),
                rows_spec(l),
                _layer_spec((d_model, d_ff), l), _layer_spec((d_model, d_ff), l), _layer_spec((d_ff, d_model), l),
            ],
            out_specs=pl.BlockSpec((1, TM_FFN, d_model), lambda b, t: (b, t, 0)),
            out_shape=jax.ShapeDtypeStruct((bsz, seq, d_model), F32),
            compiler_params=params(dimension_semantics=("parallel", "parallel"),
                                   vmem_limit_bytes=VMEM_LIMIT),
            name=f"ffn_{l}",
        )(h, rows, w_gate_b, w_up_b, w_down_b)
    return h
```

```python
import functools

import numpy as np
import jax
import jax.numpy as jnp
from jax import lax
from jax.experimental import pallas as pl
from jax.experimental.pallas import tpu as pltpu

F32 = jnp.float32
BF16 = jnp.bfloat16

EPS = 1e-6
ROPE_THETA = 10000.0
N_HEADS = 8
NOPE = 64
ROPE = 32
HEAD_V = 64
HEAD_PAD = 128
N_MEM_HEADS = 4
MEM_HD = 128
N_SPLIT = 3
ONES_ROWS = 16

LANE = 128
TM_PROJ = 512
TM_POST = 1024
TM_FFN = 512
BQ = 1024
BK = 256
MASKED = -2e30
M_INIT = -1e30
LOG2E = 1.4426950408889634
VMEM_LIMIT = 56 * 1024 * 1024

ROW_MIX_G, ROW_LATENT, ROW_DECAY_ONES, ROW_OUT_G, ROW_MEM_Q_G, ROW_FFN_G, ROW_FINAL_G, ROW_MEM_KV_G = range(8)


def _mm(a, b):
    return jnp.dot(a, b, preferred_element_type=F32)


def _mm_nt(a, b):
    return lax.dot_general(a, b, (((1,), (1,)), ((), ())), preferred_element_type=F32)


def _rms(x, g):
    return x * lax.rsqrt(jnp.mean(x * x, axis=-1, keepdims=True) + EPS) * g


def _split3(x):
    hi = x.astype(BF16).astype(F32)
    r = x - hi
    mid = r.astype(BF16).astype(F32)
    lo = (r - mid).astype(BF16).astype(F32)
    return hi, mid, lo


def _proj_in_kernel(h_ref, rows_ref, tabs_ref, wx_ref, wq_ref, wkv_ref, sel_ref, tri_ref,
                    q_ref, k_ref, vt_ref, carry_ref, *, q_lora, kv_lora):
    tm = h_ref.shape[1]
    n_sub = vt_ref.shape[1]
    bk = vt_ref.shape[3]
    width = N_HEADS * HEAD_PAD
    nv = N_HEADS * HEAD_V

    @pl.when(pl.program_id(1) == 0)
    def _():
        carry_ref[...] = jnp.zeros_like(carry_ref)

    xn = _rms(h_ref[0], rows_ref[ROW_MIX_G:ROW_MIX_G + 1, :]).astype(BF16)

    o_ckv = q_lora
    o_kf = o_ckv + kv_lora
    o_fq = o_kf + LANE
    o_fk = o_fq + nv
    o_fv = o_fk + nv

    cos_k = tabs_ref[:, 0:LANE]
    sin_k = tabs_ref[:, LANE:2 * LANE]
    q_scale = LOG2E * (NOPE + ROPE) ** -0.5
    lane = lax.broadcasted_iota(jnp.int32, (tm, LANE), 1)
    cos_q = jnp.where(lane < NOPE, q_scale, q_scale * cos_k)
    sin_q = q_scale * sin_k
    low = lane < HEAD_V

    def put_values(v, row0):
        vt = v.T.astype(BF16)
        for c in range(n_sub):
            vt_ref[0, c, row0:row0 + nv, :] = vt[:, c * bk:(c + 1) * bk]

    def put_head_pairs(out_ref, col0, narrow, fill):
        for pair in range(N_HEADS // 2):
            blk = narrow[:, pair * LANE:(pair + 1) * LANE]
            other = fill(pair)
            c = col0 + 2 * pair * HEAD_PAD
            out_ref[0, :, c:c + HEAD_PAD] = jnp.where(low, blk, other).astype(BF16)
            odd = pltpu.roll(jnp.where(low, other, blk), HEAD_V, axis=1)
            out_ref[0, :, c + HEAD_PAD:c + 2 * HEAD_PAD] = odd.astype(BF16)

    cq = _mm(xn, wx_ref[:, 0:q_lora])
    ckv = _mm(xn, wx_ref[:, o_ckv:o_kf])
    o_bias = q_lora + kv_lora
    kf = _mm(xn, wx_ref[:, o_kf:o_fq])
    f3 = kf + rows_ref[ROW_LATENT:ROW_LATENT + 1, o_bias:o_bias + LANE]
    put_values(_mm(xn, wx_ref[:, o_fv:o_fv + nv]), nv)

    cqn = _rms(cq, rows_ref[ROW_LATENT:ROW_LATENT + 1, 0:q_lora]).astype(BF16)
    qa = _mm(cqn, wq_ref[...])
    first_half = lane < NOPE + ROPE // 2

    def rotary(x, cos, sin):
        swapped = jnp.where(first_half, pltpu.roll(x, LANE - ROPE // 2, axis=1),
                            pltpu.roll(x, ROPE // 2, axis=1))
        return x * cos + swapped * sin

    for hd in range(N_HEADS):
        sl = slice(hd * HEAD_PAD, (hd + 1) * HEAD_PAD)
        q_ref[0, :, sl] = rotary(qa[:, sl], cos_q, sin_q).astype(BF16)

    ckvn = _rms(ckv, rows_ref[ROW_LATENT:ROW_LATENT + 1, q_lora:q_lora + kv_lora]).astype(BF16)
    kn = _mm(ckvn, wkv_ref[:, 0:nv])
    kr = rotary(kf, cos_k, sin_k)
    kr_both = kr + pltpu.roll(kr, HEAD_V, axis=1)
    put_head_pairs(k_ref, 0, kn, lambda pair: kr_both)
    put_values(_mm(ckvn, wkv_ref[:, nv:2 * nv]), 0)

    live = lane < N_SPLIT * N_HEADS
    log_f = jnp.minimum(f3, 0.0) - jnp.log1p(jnp.exp(-jnp.abs(f3)))
    log_f = jnp.where(live, log_f, 0.0)
    pieces = jnp.concatenate([p.astype(BF16) for p in _split3(log_f)], axis=1)
    csum = _mm(tri_ref[...], pieces)
    fq = _mm(xn, wx_ref[:, o_fq:o_fk])
    cum = csum[:, 0:LANE] + csum[:, LANE:2 * LANE] + csum[:, 2 * LANE:3 * LANE] + carry_ref[...]
    carry_ref[...] = cum[tm - 1:tm, :]
    c_hi, c_mid, c_lo = _split3(cum * LOG2E)
    c_sel = jnp.where(lane < N_HEADS, c_hi, jnp.where(lane < 2 * N_HEADS, c_mid, c_lo))
    c_sel = jnp.where(live, c_sel, 0.0).astype(BF16)
    fk = _mm(xn, wx_ref[:, o_fk:o_fv])
    aug = _mm(c_sel, sel_ref[...])
    aug = aug + rows_ref[ROW_DECAY_ONES:ROW_DECAY_ONES + 1, :]
    put_head_pairs(q_ref, width, fq, lambda pair: aug[:, pair * LANE:(pair + 1) * LANE])
    put_head_pairs(k_ref, width, fk, lambda pair: aug[:, nv + pair * LANE:nv + (pair + 1) * LANE])


def _attn_kernel(q_ref, k_ref, vt_ref, o_ref, s_scr, p_scr, mt_scr, al_scr, m_scr, acc_scr, *, bq):
    bk = vt_ref.shape[3]
    n_sub = bq // bk
    nq = q_ref.shape[1] // bq
    heads = [slice(hh * HEAD_PAD, (hh + 1) * HEAD_PAD) for hh in range(2)]

    def k_tile(j):
        return k_ref[0, pl.ds(pl.multiple_of(j * bk, bk), bk), :]

    def q_rows(qi, first, hh):
        return q_ref[0, pl.ds(pl.multiple_of(qi * bq + first, bk), bq - first), heads[hh]]

    def put_scores(u, hh, s):
        s_scr[u, hh, :, 0:bq] = s
        mt_scr[u, hh] = jnp.max(s, axis=0, keepdims=True)

    def scores_group(g, qi):
        for u in range(n_sub):
            kb = k_tile(n_sub * g + u)
            for hh in range(2):
                put_scores(u, hh, _mm_nt(kb[:, heads[hh]], q_rows(qi, 0, hh)))

    def scores_diagonal(qi):
        tri = (lax.broadcasted_iota(jnp.int32, (bk, bk), 0)
               <= lax.broadcasted_iota(jnp.int32, (bk, bk), 1))
        for u in range(n_sub):
            kb = k_tile(n_sub * qi + u)
            for hh in range(2):
                s = _mm_nt(kb[:, heads[hh]], q_rows(qi, u * bk, hh))
                parts = [jnp.full((bk, u * bk), MASKED, F32)] if u > 0 else []
                parts.append(jnp.where(tri, s[:, :bk], MASKED))
                if u < n_sub - 1:
                    parts.append(s[:, bk:])
                put_scores(u, hh, jnp.concatenate(parts, axis=1) if len(parts) > 1 else parts[0])

    def softmax_group():
        for hh in range(2):
            mt = mt_scr[0, hh]
            for u in range(1, n_sub):
                mt = jnp.maximum(mt, mt_scr[u, hh])
            m_new = jnp.maximum(m_scr[hh], mt)
            al_scr[hh] = jnp.exp2(m_scr[hh] - m_new)
            m_scr[hh] = m_new
            for u in range(n_sub):
                p_scr[u, hh, :, 0:bq] = jnp.exp2(s_scr[u, hh, :, 0:bq] - m_new).astype(BF16)

    def values_at(j0):
        ones = jnp.ones((ONES_ROWS, bk), BF16)
        for hh in range(2):
            pv = None
            for u in range(n_sub):
                vb = vt_ref[0, j0 + u]
                v = jnp.concatenate([vb[hh * HEAD_V:(hh + 1) * HEAD_V, :], ones], axis=0)
                d = _mm(v, p_scr[u, hh, :, 0:bq])
                pv = d if pv is None else pv + d
            acc_scr[hh] = al_scr[hh] * acc_scr[hh] + pv

    def last_group_start(qi):
        return jnp.where(qi == 0, 0, n_sub * (qi - 1))

    def reset():
        m_scr[...] = jnp.full_like(m_scr, M_INIT)
        acc_scr[...] = jnp.zeros_like(acc_scr)

    def finish(qi):
        ot = jnp.concatenate(
            [acc_scr[hh, 0:HEAD_V] * (1.0 / acc_scr[hh, HEAD_V:HEAD_V + 1]) for hh in range(2)], axis=0)
        o_ref[0, pl.ds(pl.multiple_of(qi * bq, bq), bq), :] = ot.T
        reset()

    def below_diagonal(qi):
        def body(g, carry):
            softmax_group()
            scores_group(g, qi)
            values_at(jnp.where(g == 0, n_sub * qi, n_sub * (g - 1)))
            return carry

        lax.fori_loop(0, qi, body, 0)

    reset()
    scores_diagonal(0)

    def block(qi, carry):
        softmax_group()
        scores_diagonal(qi)
        values_at(last_group_start(qi - 1))
        finish(qi - 1)
        below_diagonal(qi)
        return carry

    lax.fori_loop(1, nq, block, 0)
    softmax_group()
    values_at(last_group_start(nq - 1))
    finish(nq - 1)


def _post_attn_kernel(o_ref, h_ref, rows_ref, kv_ref, wo_ref, wmq_ref, wmo_ref, out_ref):
    half = o_ref.shape[2] // 2
    o = o_ref[0]
    merged = jnp.concatenate(
        [_rms(o[:, 0:half], rows_ref[ROW_OUT_G:ROW_OUT_G + 1, 0:half]),
         _rms(o[:, half:], rows_ref[ROW_OUT_G:ROW_OUT_G + 1, half:])],
        axis=1).astype(BF16)
    h1 = h_ref[0] + _mm(merged, wo_ref[...])
    xn = _rms(h1, rows_ref[ROW_MEM_Q_G:ROW_MEM_Q_G + 1, :]).astype(BF16)
    q = (_mm(xn, wmq_ref[...]) * (MEM_HD ** -0.5)).astype(BF16)
    kv = kv_ref[0]
    heads = []
    for hd in range(N_MEM_HEADS):
        kh = kv[:, 2 * hd * MEM_HD:(2 * hd + 1) * MEM_HD]
        vh = kv[:, (2 * hd + 1) * MEM_HD:(2 * hd + 2) * MEM_HD]
        s = _mm_nt(q[:, hd * MEM_HD:(hd + 1) * MEM_HD], kh)
        e = jnp.exp(s - jnp.max(s, axis=-1, keepdims=True))
        l = jnp.sum(e, axis=-1, keepdims=True)
        heads.append(_mm(e.astype(BF16), vh) * (1.0 / l))
    om = jnp.concatenate(heads, axis=1).astype(BF16)
    out_ref[0] = h1 + _mm(om, wmo_ref[...])


def _mem_kv_kernel(mem_ref, rows_ref, w_ref, kv_ref):
    g = rows_ref[ROW_MEM_KV_G:ROW_MEM_KV_G + 1, :]
    kv_ref[0] = _mm(_rms(mem_ref[0], g).astype(BF16), w_ref[...]).astype(BF16)


def _ffn_kernel(h_ref, rows_ref, wg_ref, wu_ref, wd_ref, out_ref, *, final_norm):
    half = h_ref.shape[1] // 2
    parts = [slice(0, half), slice(half, 2 * half)]
    hs = [h_ref[0, r, :] for r in parts]
    xn = [_rms(h, rows_ref[ROW_FFN_G:ROW_FFN_G + 1, :]).astype(BF16) for h in hs]
    gu = [(_mm(x, wg_ref[...]), _mm(x, wu_ref[...])) for x in xn]
    act = [(g * (1.0 / (1.0 + jnp.exp(-g))) * u).astype(BF16) for g, u in gu]
    for r, h, a in zip(parts, hs, act):
        y = h + _mm(a, wd_ref[...])
        if final_norm:
            y = _rms(y, rows_ref[ROW_FINAL_G:ROW_FINAL_G + 1, :])
        out_ref[0, r, :] = y


def _const_spec(shape):
    return pl.BlockSpec(shape, lambda *_: (0,) * len(shape), pipeline_mode=pl.Buffered(1))


def _layer_spec(shape, l):
    return pl.BlockSpec((None,) + tuple(shape), lambda *_: (l,) + (0,) * len(shape),
                        pipeline_mode=pl.Buffered(1))


def _decay_constants():
    n_half = N_HEADS * HEAD_V
    sel = np.zeros((LANE, 2 * n_half), np.float32)
    ones = np.zeros((2 * n_half,), np.float32)
    for hd in range(N_HEADS):
        base = (hd // 2) * LANE + (HEAD_V if hd % 2 == 0 else 0)
        for part in range(N_SPLIT):
            src = part * N_HEADS + hd
            sel[src, base + part] = 1.0
            ones[base + N_SPLIT + part] = 1.0
            ones[n_half + base + part] = 1.0
            sel[src, n_half + base + N_SPLIT + part] = -1.0
    return sel, ones


def _rope_tables(seq):
    f32 = np.float32
    inv = (f32(1.0) / (f32(ROPE_THETA) ** (np.arange(0, ROPE, 2, dtype=f32) / f32(ROPE)))).astype(f32)
    ang = np.arange(seq, dtype=f32)[:, None] * inv[None, :]
    cos, sin = np.cos(ang).astype(f32), np.sin(ang).astype(f32)
    tab = np.zeros((seq, 2 * HEAD_PAD), f32)
    tab[:, NOPE:NOPE + ROPE] = np.concatenate([cos, cos], axis=1)
    tab[:, HEAD_PAD + NOPE:HEAD_PAD + NOPE + ROPE] = np.concatenate([-sin, sin], axis=1)
    return jnp.asarray(tab)


def _mixer_weights(w_in, w_uq, w_ukv, q_lora, kv_lora):
    depth = w_in.shape[0]
    fox = N_HEADS * HEAD_V
    o = q_lora + kv_lora
    kr = w_in[:, :, o:o + ROPE]
    o += ROPE
    wfq, wfk, wfv = w_in[:, :, o:o + fox], w_in[:, :, o + fox:o + 2 * fox], w_in[:, :, o + 2 * fox:o + 3 * fox]
    wfl = w_in[:, :, o + 3 * fox:o + 3 * fox + N_HEADS]

    zeros = lambda n: jnp.zeros(w_in.shape[:2] + (n,), BF16)
    cast = lambda w: w.astype(BF16)
    wx = jnp.concatenate(
        [cast(w_in[:, :, 0:q_lora + kv_lora])] + [cast(wfl)] * N_SPLIT
        + [zeros(NOPE - N_SPLIT * N_HEADS), cast(kr), zeros(HEAD_PAD - NOPE - ROPE),
           cast(wfq * (LOG2E * HEAD_V ** -0.5)), cast(wfk), cast(wfv)], axis=2)

    uq = w_uq.reshape(depth, q_lora, N_HEADS, NOPE + ROPE)
    pad_hi = jnp.zeros((depth, q_lora, N_HEADS, HEAD_PAD - NOPE - ROPE), F32)
    wq = jnp.concatenate([uq, pad_hi], axis=3).reshape(depth, q_lora, -1).astype(BF16)

    ukv = w_ukv.reshape(depth, kv_lora, N_HEADS, NOPE + HEAD_V)
    wkv = jnp.concatenate([ukv[..., :NOPE].reshape(depth, kv_lora, -1),
                           ukv[..., NOPE:].reshape(depth, kv_lora, -1)], axis=2).astype(BF16)
    return wx, wq, wkv


def kernel(x, mem, mix_norm_g, w_in, cq_norm_g, ckv_norm_g, w_uq, w_ukv, forget_bias, mla_out_g, fox_out_g, w_out, mem_q_norm_g, mem_kv_norm_g, w_mq, w_mkv, w_mo, ffn_norm_g, w_gate, w_up, w_down, final_norm_g):
    bsz, seq, d_model = x.shape
    depth = w_in.shape[0]
    q_lora = cq_norm_g.shape[1]
    kv_lora = ckv_norm_g.shape[1]
    mem_len = mem.shape[1]
    d_ff = w_gate.shape[2]
    width = N_HEADS * HEAD_PAD
    n_v = 2 * N_HEADS * HEAD_V
    assert d_model == width == n_v, "layout assumes d_model = 8 heads * 128"
    assert seq % BQ == 0 and seq % TM_PROJ == 0 and TM_PROJ % BK == 0 and BQ % BK == 0
    assert seq % TM_POST == 0 and seq % TM_FFN == 0

    tabs = _rope_tables(seq)
    sel_np, decay_ones = _decay_constants()
    sel = jnp.asarray(sel_np, BF16)
    tri = jnp.asarray(np.tril(np.ones((TM_PROJ, TM_PROJ), np.float32)), BF16)
    params = pltpu.CompilerParams

    wx, wq, wkv = _mixer_weights(w_in, w_uq, w_ukv, q_lora, kv_lora)
    w_out_b, w_mq_b, w_mkv_b, w_mo_b = (w.astype(BF16) for w in (w_out, w_mq, w_mkv, w_mo))
    w_gate_b, w_up_b, w_down_b = (w.astype(BF16) for w in (w_gate, w_up, w_down))
    latent = jnp.pad(jnp.concatenate([cq_norm_g, ckv_norm_g] + [forget_bias] * N_SPLIT, axis=1),
                     ((0, 0), (0, d_model - q_lora - kv_lora - N_SPLIT * N_HEADS)))
    per_layer = lambda v: jnp.broadcast_to(v, (depth, d_model))
    rows = jnp.stack([mix_norm_g, latent, per_layer(jnp.asarray(decay_ones)),
                      jnp.concatenate([mla_out_g, fox_out_g], axis=1), mem_q_norm_g, ffn_norm_g,
                      per_layer(final_norm_g), mem_kv_norm_g], axis=1)
    rows_spec = lambda l: _layer_spec((8, d_model), l)

    h = x
    for l in range(depth):
        q_all, k_all, vt_all = pl.pallas_call(
            functools.partial(_proj_in_kernel, q_lora=q_lora, kv_lora=kv_lora),
            grid=(bsz, seq // TM_PROJ),
            in_specs=[
                pl.BlockSpec((1, TM_PROJ, d_model), lambda b, t: (b, t, 0)),
                rows_spec(l),
                pl.BlockSpec((TM_PROJ, 2 * LANE), lambda b, t: (t, 0)),
                _layer_spec(wx.shape[1:], l), _layer_spec(wq.shape[1:], l), _layer_spec(wkv.shape[1:], l),
                _const_spec(sel.shape), _const_spec(tri.shape),
            ],
            out_specs=[
                pl.BlockSpec((1, TM_PROJ, 2 * width), lambda b, t: (b, t, 0)),
                pl.BlockSpec((1, TM_PROJ, 2 * width), lambda b, t: (b, t, 0)),
                pl.BlockSpec((1, TM_PROJ // BK, n_v, BK), lambda b, t: (b, t, 0, 0)),
            ],
            out_shape=[
                jax.ShapeDtypeStruct((bsz, seq, 2 * width), BF16),
                jax.ShapeDtypeStruct((bsz, seq, 2 * width), BF16),
                jax.ShapeDtypeStruct((bsz, seq // BK, n_v, BK), BF16),
            ],
            scratch_shapes=[pltpu.VMEM((1, LANE), F32)],
            compiler_params=params(dimension_semantics=("arbitrary", "arbitrary"),
                                   vmem_limit_bytes=VMEM_LIMIT),
            name=f"proj_in_{l}",
        )(h, rows, tabs, wx, wq, wkv, sel, tri)

        o_all = pl.pallas_call(
            functools.partial(_attn_kernel, bq=BQ),
            grid=(bsz, N_HEADS),
            in_specs=[
                pl.BlockSpec((1, seq, 2 * HEAD_PAD), lambda b, p: (b, 0, p)),
                pl.BlockSpec((1, seq, 2 * HEAD_PAD), lambda b, p: (b, 0, p)),
                pl.BlockSpec((1, seq // BK, 2 * HEAD_V, BK), lambda b, p: (b, 0, p, 0)),
            ],
            out_specs=pl.BlockSpec((1, seq, 2 * HEAD_V), lambda b, p: (b, 0, p)),
            out_shape=jax.ShapeDtypeStruct((bsz, seq, n_v), F32),
            scratch_shapes=[
                pltpu.VMEM((BQ // BK, 2, BK, BQ + LANE), F32),
                pltpu.VMEM((BQ // BK, 2, BK, BQ + LANE), BF16),
                pltpu.VMEM((BQ // BK, 2, 1, BQ), F32),
                pltpu.VMEM((2, 1, BQ), F32),
                pltpu.VMEM((2, 1, BQ), F32),
                pltpu.VMEM((2, HEAD_V + ONES_ROWS, BQ), F32),
            ],
            compiler_params=params(dimension_semantics=("parallel", "parallel"),
                                   vmem_limit_bytes=VMEM_LIMIT),
            name=f"attn_{l}",
        )(q_all, k_all, vt_all)

        kv_mem = pl.pallas_call(
            _mem_kv_kernel,
            grid=(bsz,),
            in_specs=[
                pl.BlockSpec((1, mem_len, d_model), lambda b: (b, 0, 0)),
                rows_spec(l),
                _layer_spec(w_mkv.shape[1:], l),
            ],
            out_specs=pl.BlockSpec((1, mem_len, w_mkv.shape[2]), lambda b: (b, 0, 0)),
            out_shape=jax.ShapeDtypeStruct((bsz, mem_len, w_mkv.shape[2]), BF16),
            compiler_params=params(dimension_semantics=("parallel",), vmem_limit_bytes=VMEM_LIMIT),
            name=f"mem_kv_{l}",
        )(mem, rows, w_mkv_b)

        h = pl.pallas_call(
            _post_attn_kernel,
            grid=(bsz, seq // TM_POST),
            in_specs=[
                pl.BlockSpec((1, TM_POST, n_v), lambda b, t: (b, t, 0)),
                pl.BlockSpec((1, TM_POST, d_model), lambda b, t: (b, t, 0)),
                rows_spec(l),
                pl.BlockSpec((1, mem_len, w_mkv.shape[2]), lambda b, t: (b, 0, 0)),
                _layer_spec(w_out.shape[1:], l), _layer_spec(w_mq.shape[1:], l), _layer_spec(w_mo.shape[1:], l),
            ],
            out_specs=pl.BlockSpec((1, TM_POST, d_model), lambda b, t: (b, t, 0)),
            out_shape=jax.ShapeDtypeStruct((bsz, seq, d_model), F32),
            compiler_params=params(dimension_semantics=("parallel", "parallel"),
                                   vmem_limit_bytes=VMEM_LIMIT),
            name=f"post_attn_{l}",
        )(o_all, h, rows, kv_mem, w_out_b, w_mq_b, w_mo_b)

        last = l == depth - 1
        h = pl.pallas_call(
            functools.partial(_ffn_kernel, final_norm=last),
            grid=(bsz, seq // TM_FFN),
            in_specs=[
                pl.BlockSpec((1, TM_FFN, d_model), lambda b, t: (b, t, 0)),
                rows_spec(l),
                _layer_spec((d_model, d_ff), l), _layer_spec((d_model, d_ff), l), _layer_spec((d_ff, d_model), l),
            ],
            out_specs=pl.BlockSpec((1, TM_FFN, d_model), lambda b, t: (b, t, 0)),
            out_shape=jax.ShapeDtypeStruct((bsz, seq, d_model), F32),
            compiler_params=params(dimension_semantics=("parallel", "parallel"),
                                   vmem_limit_bytes=VMEM_LIMIT),
            name=f"ffn_{l}",
        )(h, rows, w_gate_b, w_up_b, w_down_b)
    return h
```

```python
import functools

import numpy as np
import jax
import jax.numpy as jnp
from jax import lax
from jax.experimental import pallas as pl
from jax.experimental.pallas import tpu as pltpu

F32 = jnp.float32
BF16 = jnp.bfloat16

EPS = 1e-6
ROPE_THETA = 10000.0
N_HEADS = 8
NOPE = 64
ROPE = 32
HEAD_V = 64
HEAD_PAD = 128
N_MEM_HEADS = 4
MEM_HD = 128
N_SPLIT = 3
ONES_ROWS = 16

LANE = 128
TM_PROJ = 512
TM_POST = 1024
TM_FFN = 512
BQ = 1024
BK = 256
MASKED = -2e30
M_INIT = -1e30
LOG2E = 1.4426950408889634
VMEM_LIMIT = 56 * 1024 * 1024

ROW_MIX_G, ROW_LATENT, ROW_DECAY_ONES, ROW_OUT_G, ROW_MEM_Q_G, ROW_FFN_G, ROW_FINAL_G, ROW_MEM_KV_G = range(8)


def _mm(a, b):
    return jnp.dot(a, b, preferred_element_type=F32)


def _mm_nt(a, b):
    return lax.dot_general(a, b, (((1,), (1,)), ((), ())), preferred_element_type=F32)


def _rms(x, g):
    return x * lax.rsqrt(jnp.mean(x * x, axis=-1, keepdims=True) + EPS) * g


def _split3(x):
    hi = x.astype(BF16).astype(F32)
    r = x - hi
    mid = r.astype(BF16).astype(F32)
    lo = (r - mid).astype(BF16).astype(F32)
    return hi, mid, lo


def _proj_in_kernel(h_ref, rows_ref, tabs_ref, wx_ref, wq_ref, wkv_ref, sel_ref, tri_ref,
                    q_ref, k_ref, vt_ref, carry_ref, *, q_lora, kv_lora):
    tm = h_ref.shape[1]
    n_sub = vt_ref.shape[1]
    bk = vt_ref.shape[3]
    width = N_HEADS * HEAD_PAD
    nv = N_HEADS * HEAD_V

    @pl.when(pl.program_id(1) == 0)
    def _():
        carry_ref[...] = jnp.zeros_like(carry_ref)

    xn = _rms(h_ref[0], rows_ref[ROW_MIX_G:ROW_MIX_G + 1, :]).astype(BF16)

    o_ckv = q_lora
    o_kf = o_ckv + kv_lora
    o_fq = o_kf + LANE
    o_fk = o_fq + nv
    o_fv = o_fk + nv

    cos_k = tabs_ref[:, 0:LANE]
    sin_k = tabs_ref[:, LANE:2 * LANE]
    q_scale = LOG2E * (NOPE + ROPE) ** -0.5
    lane = lax.broadcasted_iota(jnp.int32, (tm, LANE), 1)
    cos_q = jnp.where(lane < NOPE, q_scale, q_scale * cos_k)
    sin_q = q_scale * sin_k
    low = lane < HEAD_V

    def put_values(v, row0):
        vt = v.T.astype(BF16)
        for c in range(n_sub):
            vt_ref[0, c, row0:row0 + nv, :] = vt[:, c * bk:(c + 1) * bk]

    def put_head_pairs(out_ref, col0, narrow, fill):
        for pair in range(N_HEADS // 2):
            blk = narrow[:, pair * LANE:(pair + 1) * LANE]
            other = fill(pair)
            c = col0 + 2 * pair * HEAD_PAD
            out_ref[0, :, c:c + HEAD_PAD] = jnp.where(low, blk, other).astype(BF16)
            odd = pltpu.roll(jnp.where(low, other, blk), HEAD_V, axis=1)
            out_ref[0, :, c + HEAD_PAD:c + 2 * HEAD_PAD] = odd.astype(BF16)

    cq = _mm(xn, wx_ref[:, 0:q_lora])
    ckv = _mm(xn, wx_ref[:, o_ckv:o_kf])
    o_bias = q_lora + kv_lora
    kf = _mm(xn, wx_ref[:, o_kf:o_fq])
    f3 = kf + rows_ref[ROW_LATENT:ROW_LATENT + 1, o_bias:o_bias + LANE]
    put_values(_mm(xn, wx_ref[:, o_fv:o_fv + nv]), nv)

    cqn = _rms(cq, rows_ref[ROW_LATENT:ROW_LATENT + 1, 0:q_lora]).astype(BF16)
    qa = _mm(cqn, wq_ref[...])
    first_half = lane < NOPE + ROPE // 2

    def rotary(x, cos, sin):
        swapped = jnp.where(first_half, pltpu.roll(x, LANE - ROPE // 2, axis=1),
                            pltpu.roll(x, ROPE // 2, axis=1))
        return x * cos + swapped * sin

    for hd in range(N_HEADS):
        sl = slice(hd * HEAD_PAD, (hd + 1) * HEAD_PAD)
        q_ref[0, :, sl] = rotary(qa[:, sl], cos_q, sin_q).astype(BF16)

    ckvn = _rms(ckv, rows_ref[ROW_LATENT:ROW_LATENT + 1, q_lora:q_lora + kv_lora]).astype(BF16)
    kn = _mm(ckvn, wkv_ref[:, 0:nv])
    kr = rotary(kf, cos_k, sin_k)
    kr_both = kr + pltpu.roll(kr, HEAD_V, axis=1)
    put_head_pairs(k_ref, 0, kn, lambda pair: kr_both)
    put_values(_mm(ckvn, wkv_ref[:, nv:2 * nv]), 0)

    live = lane < N_SPLIT * N_HEADS
    log_f = jnp.minimum(f3, 0.0) - jnp.log1p(jnp.exp(-jnp.abs(f3)))
    log_f = jnp.where(live, log_f, 0.0)
    pieces = jnp.concatenate([p.astype(BF16) for p in _split3(log_f)], axis=1)
    csum = _mm(tri_ref[...], pieces)
    fq = _mm(xn, wx_ref[:, o_fq:o_fk])
    cum = csum[:, 0:LANE] + csum[:, LANE:2 * LANE] + csum[:, 2 * LANE:3 * LANE] + carry_ref[...]
    carry_ref[...] = cum[tm - 1:tm, :]
    c_hi, c_mid, c_lo = _split3(cum * LOG2E)
    c_sel = jnp.where(lane < N_HEADS, c_hi, jnp.where(lane < 2 * N_HEADS, c_mid, c_lo))
    c_sel = jnp.where(live, c_sel, 0.0).astype(BF16)
    fk = _mm(xn, wx_ref[:, o_fk:o_fv])
    aug = _mm(c_sel, sel_ref[...])
    aug = aug + rows_ref[ROW_DECAY_ONES:ROW_DECAY_ONES + 1, :]
    put_head_pairs(q_ref, width, fq, lambda pair: aug[:, pair * LANE:(pair + 1) * LANE])
    put_head_pairs(k_ref, width, fk, lambda pair: aug[:, nv + pair * LANE:nv + (pair + 1) * LANE])


def _attn_kernel(q_ref, k_ref, vt_ref, o_ref, s_scr, p_scr, mt_scr, al_scr, m_scr, acc_scr, out_scr, *, bq):
    bk = vt_ref.shape[3]
    n_sub = bq // bk
    nq = q_ref.shape[1] // bq
    heads = [slice(hh * HEAD_PAD, (hh + 1) * HEAD_PAD) for hh in range(2)]

    def k_tile(j):
        return k_ref[0, pl.ds(pl.multiple_of(j * bk, bk), bk), :]

    def q_rows(qi, first, hh):
        return q_ref[0, pl.ds(pl.multiple_of(qi * bq + first, bk), bq - first), heads[hh]]

    def put_scores(u, hh, s):
        s_scr[u, hh, :, 0:bq] = s
        mt_scr[u, hh] = jnp.max(s, axis=0, keepdims=True)

    def scores_group(g, qi):
        for u in range(n_sub):
            kb = k_tile(n_sub * g + u)
            for hh in range(2):
                put_scores(u, hh, _mm_nt(kb[:, heads[hh]], q_rows(qi, 0, hh)))

    def scores_diagonal(qi):
        tri = (lax.broadcasted_iota(jnp.int32, (bk, bk), 0)
               <= lax.broadcasted_iota(jnp.int32, (bk, bk), 1))
        for u in range(n_sub):
            kb = k_tile(n_sub * qi + u)
            for hh in range(2):
                s = _mm_nt(kb[:, heads[hh]], q_rows(qi, u * bk, hh))
                parts = [jnp.full((bk, u * bk), MASKED, F32)] if u > 0 else []
                parts.append(jnp.where(tri, s[:, :bk], MASKED))
                if u < n_sub - 1:
                    parts.append(s[:, bk:])
                put_scores(u, hh, jnp.concatenate(parts, axis=1) if len(parts) > 1 else parts[0])

    def softmax_group():
        for hh in range(2):
            mt = mt_scr[0, hh]
            for u in range(1, n_sub):
                mt = jnp.maximum(mt, mt_scr[u, hh])
            m_new = jnp.maximum(m_scr[hh], mt)
            al_scr[hh] = jnp.exp2(m_scr[hh] - m_new)
            m_scr[hh] = m_new
            for u in range(n_sub):
                p_scr[u, hh, :, 0:bq] = jnp.exp2(s_scr[u, hh, :, 0:bq] - m_new).astype(BF16)

    def values_at(j0):
        ones = jnp.ones((ONES_ROWS, bk), BF16)
        for hh in range(2):
            pv = None
            for u in range(n_sub):
                vb = vt_ref[0, j0 + u]
                v = jnp.concatenate([vb[hh * HEAD_V:(hh + 1) * HEAD_V, :], ones], axis=0)
                d = _mm(v, p_scr[u, hh, :, 0:bq])
                pv = d if pv is None else pv + d
            acc_scr[hh] = al_scr[hh] * acc_scr[hh] + pv

    def last_group_start(qi):
        return jnp.where(qi == 0, 0, n_sub * (qi - 1))

    def reset():
        m_scr[...] = jnp.full_like(m_scr, M_INIT)
        acc_scr[...] = jnp.zeros_like(acc_scr)

    def normalise():
        out_scr[...] = jnp.concatenate(
            [acc_scr[hh, 0:HEAD_V] * (1.0 / acc_scr[hh, HEAD_V:HEAD_V + 1]) for hh in range(2)], axis=0)
        reset()

    def store(qi):
        o_ref[0, pl.ds(pl.multiple_of(qi * bq, bq), bq), :] = out_scr[...].T

    def below_diagonal(qi):
        def body(g, carry):
            softmax_group()
            scores_group(g, qi)
            values_at(jnp.where(g == 0, n_sub * qi, n_sub * (g - 1)))
            return carry

        lax.fori_loop(0, qi, body, 0)

    reset()
    scores_diagonal(0)

    def block(qi, store_previous):
        if store_previous:
            store(qi - 2)
        softmax_group()
        scores_diagonal(qi)
        values_at(last_group_start(qi - 1))
        normalise()
        below_diagonal(qi)

    if nq > 1:
        block(1, False)
        lax.fori_loop(2, nq, lambda qi, carry: (block(qi, True), carry)[1], 0)
        store(nq - 2)
    softmax_group()
    values_at(last_group_start(nq - 1))
    normalise()
    store(nq - 1)


def _post_attn_kernel(o_ref, h_ref, rows_ref, kv_ref, wo_ref, wmq_ref, wmo_ref, out_ref):
    half = o_ref.shape[2] // 2
    o = o_ref[0]
    merged = jnp.concatenate(
        [_rms(o[:, 0:half], rows_ref[ROW_OUT_G:ROW_OUT_G + 1, 0:half]),
         _rms(o[:, half:], rows_ref[ROW_OUT_G:ROW_OUT_G + 1, half:])],
        axis=1).astype(BF16)
    h1 = h_ref[0] + _mm(merged, wo_ref[...])
    xn = _rms(h1, rows_ref[ROW_MEM_Q_G:ROW_MEM_Q_G + 1, :]).astype(BF16)
    q = (_mm(xn, wmq_ref[...]) * (MEM_HD ** -0.5)).astype(BF16)
    kv = kv_ref[0]
    heads = []
    for hd in range(N_MEM_HEADS):
        kh = kv[:, 2 * hd * MEM_HD:(2 * hd + 1) * MEM_HD]
        vh = kv[:, (2 * hd + 1) * MEM_HD:(2 * hd + 2) * MEM_HD]
        s = _mm_nt(q[:, hd * MEM_HD:(hd + 1) * MEM_HD], kh)
        e = jnp.exp(s - jnp.max(s, axis=-1, keepdims=True))
        l = jnp.sum(e, axis=-1, keepdims=True)
        heads.append(_mm(e.astype(BF16), vh) * (1.0 / l))
    om = jnp.concatenate(heads, axis=1).astype(BF16)
    out_ref[0] = h1 + _mm(om, wmo_ref[...])


def _mem_kv_kernel(mem_ref, rows_ref, w_ref, kv_ref):
    g = rows_ref[ROW_MEM_KV_G:ROW_MEM_KV_G + 1, :]
    kv_ref[0] = _mm(_rms(mem_ref[0], g).astype(BF16), w_ref[...]).astype(BF16)


def _ffn_kernel(h_ref, rows_ref, wg_ref, wu_ref, wd_ref, out_ref, *, final_norm):
    half = h_ref.shape[1] // 2
    parts = [slice(0, half), slice(half, 2 * half)]
    hs = [h_ref[0, r, :] for r in parts]
    xn = [_rms(h, rows_ref[ROW_FFN_G:ROW_FFN_G + 1, :]).astype(BF16) for h in hs]
    gu = [(_mm(x, wg_ref[...]), _mm(x, wu_ref[...])) for x in xn]
    act = [(g * (1.0 / (1.0 + jnp.exp(-g))) * u).astype(BF16) for g, u in gu]
    for r, h, a in zip(parts, hs, act):
        y = h + _mm(a, wd_ref[...])
        if final_norm:
            y = _rms(y, rows_ref[ROW_FINAL_G:ROW_FINAL_G + 1, :])
        out_ref[0, r, :] = y


def _const_spec(shape):
    return pl.BlockSpec(shape, lambda *_: (0,) * len(shape), pipeline_mode=pl.Buffered(1))


def _layer_spec(shape, l):
    return pl.BlockSpec((None,) + tuple(shape), lambda *_: (l,) + (0,) * len(shape),
                        pipeline_mode=pl.Buffered(1))


def _decay_constants():
    n_half = N_HEADS * HEAD_V
    sel = np.zeros((LANE, 2 * n_half), np.float32)
    ones = np.zeros((2 * n_half,), np.float32)
    for hd in range(N_HEADS):
        base = (hd // 2) * LANE + (HEAD_V if hd % 2 == 0 else 0)
        for part in range(N_SPLIT):
            src = part * N_HEADS + hd
            sel[src, base + part] = 1.0
            ones[base + N_SPLIT + part] = 1.0
            ones[n_half + base + part] = 1.0
            sel[src, n_half + base + N_SPLIT + part] = -1.0
    return sel, ones


def _rope_tables(seq):
    f32 = np.float32
    inv = (f32(1.0) / (f32(ROPE_THETA) ** (np.arange(0, ROPE, 2, dtype=f32) / f32(ROPE)))).astype(f32)
    ang = np.arange(seq, dtype=f32)[:, None] * inv[None, :]
    cos, sin = np.cos(ang).astype(f32), np.sin(ang).astype(f32)
    tab = np.zeros((seq, 2 * HEAD_PAD), f32)
    tab[:, NOPE:NOPE + ROPE] = np.concatenate([cos, cos], axis=1)
    tab[:, HEAD_PAD + NOPE:HEAD_PAD + NOPE + ROPE] = np.concatenate([-sin, sin], axis=1)
    return jnp.asarray(tab)


def _mixer_weights(w_in, w_uq, w_ukv, q_lora, kv_lora):
    depth = w_in.shape[0]
    fox = N_HEADS * HEAD_V
    o = q_lora + kv_lora
    kr = w_in[:, :, o:o + ROPE]
    o += ROPE
    wfq, wfk, wfv = w_in[:, :, o:o + fox], w_in[:, :, o + fox:o + 2 * fox], w_in[:, :, o + 2 * fox:o + 3 * fox]
    wfl = w_in[:, :, o + 3 * fox:o + 3 * fox + N_HEADS]

    zeros = lambda n: jnp.zeros(w_in.shape[:2] + (n,), BF16)
    cast = lambda w: w.astype(BF16)
    wx = jnp.concatenate(
        [cast(w_in[:, :, 0:q_lora + kv_lora])] + [cast(wfl)] * N_SPLIT
        + [zeros(NOPE - N_SPLIT * N_HEADS), cast(kr), zeros(HEAD_PAD - NOPE - ROPE),
           cast(wfq * (LOG2E * HEAD_V ** -0.5)), cast(wfk), cast(wfv)], axis=2)

    uq = w_uq.reshape(depth, q_lora, N_HEADS, NOPE + ROPE)
    pad_hi = jnp.zeros((depth, q_lora, N_HEADS, HEAD_PAD - NOPE - ROPE), F32)
    wq = jnp.concatenate([uq, pad_hi], axis=3).reshape(depth, q_lora, -1).astype(BF16)

    ukv = w_ukv.reshape(depth, kv_lora, N_HEADS, NOPE + HEAD_V)
    wkv = jnp.concatenate([ukv[..., :NOPE].reshape(depth, kv_lora, -1),
                           ukv[..., NOPE:].reshape(depth, kv_lora, -1)], axis=2).astype(BF16)
    return wx, wq, wkv


def kernel(x, mem, mix_norm_g, w_in, cq_norm_g, ckv_norm_g, w_uq, w_ukv, forget_bias, mla_out_g, fox_out_g, w_out, mem_q_norm_g, mem_kv_norm_g, w_mq, w_mkv, w_mo, ffn_norm_g, w_gate, w_up, w_down, final_norm_g):
    bsz, seq, d_model = x.shape
    depth = w_in.shape[0]
    q_lora = cq_norm_g.shape[1]
    kv_lora = ckv_norm_g.shape[1]
    mem_len = mem.shape[1]
    d_ff = w_gate.shape[2]
    width = N_HEADS * HEAD_PAD
    n_v = 2 * N_HEADS * HEAD_V
    assert d_model == width == n_v, "layout assumes d_model = 8 heads * 128"
    assert seq % BQ == 0 and seq % TM_PROJ == 0 and TM_PROJ % BK == 0 and BQ % BK == 0
    assert seq % TM_POST == 0 and seq % TM_FFN == 0

    tabs = _rope_tables(seq)
    sel_np, decay_ones = _decay_constants()
    sel = jnp.asarray(sel_np, BF16)
    tri = jnp.asarray(np.tril(np.ones((TM_PROJ, TM_PROJ), np.float32)), BF16)
    params = pltpu.CompilerParams

    wx, wq, wkv = _mixer_weights(w_in, w_uq, w_ukv, q_lora, kv_lora)
    w_out_b, w_mq_b, w_mkv_b, w_mo_b = (w.astype(BF16) for w in (w_out, w_mq, w_mkv, w_mo))
    w_gate_b, w_up_b, w_down_b = (w.astype(BF16) for w in (w_gate, w_up, w_down))
    latent = jnp.pad(jnp.concatenate([cq_norm_g, ckv_norm_g] + [forget_bias] * N_SPLIT, axis=1),
                     ((0, 0), (0, d_model - q_lora - kv_lora - N_SPLIT * N_HEADS)))
    per_layer = lambda v: jnp.broadcast_to(v, (depth, d_model))
    rows = jnp.stack([mix_norm_g, latent, per_layer(jnp.asarray(decay_ones)),
                      jnp.concatenate([mla_out_g, fox_out_g], axis=1), mem_q_norm_g, ffn_norm_g,
                      per_layer(final_norm_g), mem_kv_norm_g], axis=1)
    rows_spec = lambda l: _layer_spec((8, d_model), l)

    h = x
    for l in range(depth):
        q_all, k_all, vt_all = pl.pallas_call(
            functools.partial(_proj_in_kernel, q_lora=q_lora, kv_lora=kv_lora),
            grid=(bsz, seq // TM_PROJ),
            in_specs=[
                pl.BlockSpec((1, TM_PROJ, d_model), lambda b, t: (b, t, 0)),
                rows_spec(l),
                pl.BlockSpec((TM_PROJ, 2 * LANE), lambda b, t: (t, 0)),
                _layer_spec(wx.shape[1:], l), _layer_spec(wq.shape[1:], l), _layer_spec(wkv.shape[1:], l),
                _const_spec(sel.shape), _const_spec(tri.shape),
            ],
            out_specs=[
                pl.BlockSpec((1, TM_PROJ, 2 * width), lambda b, t: (b, t, 0)),
                pl.BlockSpec((1, TM_PROJ, 2 * width), lambda b, t: (b, t, 0)),
                pl.BlockSpec((1, TM_PROJ // BK, n_v, BK), lambda b, t: (b, t, 0, 0)),
            ],
            out_shape=[
                jax.ShapeDtypeStruct((bsz, seq, 2 * width), BF16),
                jax.ShapeDtypeStruct((bsz, seq, 2 * width), BF16),
                jax.ShapeDtypeStruct((bsz, seq // BK, n_v, BK), BF16),
            ],
            scratch_shapes=[pltpu.VMEM((1, LANE), F32)],
            compiler_params=params(dimension_semantics=("arbitrary", "arbitrary"),
                                   vmem_limit_bytes=VMEM_LIMIT),
            name=f"proj_in_{l}",
        )(h, rows, tabs, wx, wq, wkv, sel, tri)

        o_all = pl.pallas_call(
            functools.partial(_attn_kernel, bq=BQ),
            grid=(bsz, N_HEADS),
            in_specs=[
                pl.BlockSpec((1, seq, 2 * HEAD_PAD), lambda b, p: (b, 0, p)),
                pl.BlockSpec((1, seq, 2 * HEAD_PAD), lambda b, p: (b, 0, p)),
                pl.BlockSpec((1, seq // BK, 2 * HEAD_V, BK), lambda b, p: (b, 0, p, 0)),
            ],
            out_specs=pl.BlockSpec((1, seq, 2 * HEAD_V), lambda b, p: (b, 0, p)),
            out_shape=jax.ShapeDtypeStruct((bsz, seq, n_v), F32),
            scratch_shapes=[
                pltpu.VMEM((BQ // BK, 2, BK, BQ + LANE), F32),
                pltpu.VMEM((BQ // BK, 2, BK, BQ + LANE), BF16),
                pltpu.VMEM((BQ // BK, 2, 1, BQ), F32),
                pltpu.VMEM((2, 1, BQ), F32),
                pltpu.VMEM((2, 1, BQ), F32),
                pltpu.VMEM((2, HEAD_V + ONES_ROWS, BQ), F32),
                pltpu.VMEM((2 * HEAD_V, BQ), F32),
            ],
            compiler_params=params(dimension_semantics=("parallel", "parallel"),
                                   vmem_limit_bytes=VMEM_LIMIT),
            name=f"attn_{l}",
        )(q_all, k_all, vt_all)

        kv_mem = pl.pallas_call(
            _mem_kv_kernel,
            grid=(bsz,),
            in_specs=[
                pl.BlockSpec((1, mem_len, d_model), lambda b: (b, 0, 0)),
                rows_spec(l),
                _layer_spec(w_mkv.shape[1:], l),
            ],
            out_specs=pl.BlockSpec((1, mem_len, w_mkv.shape[2]), lambda b: (b, 0, 0)),
            out_shape=jax.ShapeDtypeStruct((bsz, mem_len, w_mkv.shape[2]), BF16),
            compiler_params=params(dimension_semantics=("parallel",), vmem_limit_bytes=VMEM_LIMIT),
            name=f"mem_kv_{l}",
        )(mem, rows, w_mkv_b)

        h = pl.pallas_call(
            _post_attn_kernel,
            grid=(bsz, seq // TM_POST),
            in_specs=[
                pl.BlockSpec((1, TM_POST, n_v), lambda b, t: (b, t, 0)),
                pl.BlockSpec((1, TM_POST, d_model), lambda b, t: (b, t, 0)),
                rows_spec(l),
                pl.BlockSpec((1, mem_len, w_mkv.shape[2]), lambda b, t: (b, 0, 0)),
                _layer_spec(w_out.shape[1:], l), _layer_spec(w_mq.shape[1:], l), _layer_spec(w_mo.shape[1:], l),
            ],
            out_specs=pl.BlockSpec((1, TM_POST, d_model), lambda b, t: (b, t, 0)),
            out_shape=jax.ShapeDtypeStruct((bsz, seq, d_model), F32),
            compiler_params=params(dimension_semantics=("parallel", "parallel"),
                                   vmem_limit_bytes=VMEM_LIMIT),
            name=f"post_attn_{l}",
        )(o_all, h, rows, kv_mem, w_out_b, w_mq_b, w_mo_b)

        last = l == depth - 1
        h = pl.pallas_call(
            functools.partial(_ffn_kernel, final_norm=last),
            grid=(bsz, seq // TM_FFN),
            in_specs=[
                pl.BlockSpec((1, TM_FFN, d_model), lambda b, t: (b, t, 0)),
                rows_spec(l),
                _layer_spec((d_model, d_ff), l), _layer_spec((d_model, d_ff), l), _layer_spec((d_ff, d_model), l),
            ],
            out_specs=pl.BlockSpec((1, TM_FFN, d_model), lambda b, t: (b, t, 0)),
            out_shape=jax.ShapeDtypeStruct((bsz, seq, d_model), F32),
            compiler_params=params(dimension_semantics=("parallel", "parallel"),
                                   vmem_limit_bytes=VMEM_LIMIT),
            name=f"ffn_{l}",
        )(h, rows, w_gate_b, w_up_b, w_down_b)
    return h
```

```python
import functools

import numpy as np
import jax
import jax.numpy as jnp
from jax import lax
from jax.experimental import pallas as pl
from jax.experimental.pallas import tpu as pltpu

F32 = jnp.float32
BF16 = jnp.bfloat16

EPS = 1e-6
ROPE_THETA = 10000.0
N_HEADS = 8
NOPE = 64
ROPE = 32
HEAD_V = 64
HEAD_PAD = 128
N_MEM_HEADS = 4
MEM_HD = 128
N_SPLIT = 3
ONES_ROWS = 16

LANE = 128
TM_PROJ = 512
TM_POST = 1024
TM_FFN = 512
BQ = 1024
BK = 256
MASKED = -2e30
M_INIT = -1e30
LOG2E = 1.4426950408889634
VMEM_LIMIT = 56 * 1024 * 1024

ROW_MIX_G, ROW_LATENT, ROW_DECAY_ONES, ROW_OUT_G, ROW_MEM_Q_G, ROW_FFN_G, ROW_FINAL_G, ROW_MEM_KV_G = range(8)


def _mm(a, b):
    return jnp.dot(a, b, preferred_element_type=F32)


def _mm_nt(a, b):
    return lax.dot_general(a, b, (((1,), (1,)), ((), ())), preferred_element_type=F32)


def _rms(x, g):
    return x * lax.rsqrt(jnp.mean(x * x, axis=-1, keepdims=True) + EPS) * g


def _split3(x):
    hi = x.astype(BF16).astype(F32)
    r = x - hi
    mid = r.astype(BF16).astype(F32)
    lo = (r - mid).astype(BF16).astype(F32)
    return hi, mid, lo


def _proj_in_kernel(h_ref, rows_ref, tabs_ref, wx_ref, wq_ref, wkv_ref, sel_ref, tri_ref,
                    q_ref, k_ref, vt_ref, carry_ref, *, q_lora, kv_lora):
    tm = h_ref.shape[1]
    n_sub = vt_ref.shape[1]
    bk = vt_ref.shape[3]
    width = N_HEADS * HEAD_PAD
    nv = N_HEADS * HEAD_V

    @pl.when(pl.program_id(1) == 0)
    def _():
        carry_ref[...] = jnp.zeros_like(carry_ref)

    xn = _rms(h_ref[0], rows_ref[ROW_MIX_G:ROW_MIX_G + 1, :]).astype(BF16)

    o_ckv = q_lora
    o_kf = o_ckv + kv_lora
    o_fq = o_kf + LANE
    o_fk = o_fq + nv
    o_fv = o_fk + nv

    cos_k = tabs_ref[:, 0:LANE]
    sin_k = tabs_ref[:, LANE:2 * LANE]
    q_scale = LOG2E * (NOPE + ROPE) ** -0.5
    lane = lax.broadcasted_iota(jnp.int32, (tm, LANE), 1)
    cos_q = jnp.where(lane < NOPE, q_scale, q_scale * cos_k)
    sin_q = q_scale * sin_k
    low = lane < HEAD_V

    def put_values(v, row0):
        vt = v.T.astype(BF16)
        for c in range(n_sub):
            vt_ref[0, c, row0:row0 + nv, :] = vt[:, c * bk:(c + 1) * bk]

    def put_head_pairs(out_ref, col0, narrow, fill):
        for pair in range(N_HEADS // 2):
            blk = narrow[:, pair * LANE:(pair + 1) * LANE]
            other = fill(pair)
            c = col0 + 2 * pair * HEAD_PAD
            out_ref[0, :, c:c + HEAD_PAD] = jnp.where(low, blk, other).astype(BF16)
            odd = pltpu.roll(jnp.where(low, other, blk), HEAD_V, axis=1)
            out_ref[0, :, c + HEAD_PAD:c + 2 * HEAD_PAD] = odd.astype(BF16)

    cq = _mm(xn, wx_ref[:, 0:q_lora])
    ckv = _mm(xn, wx_ref[:, o_ckv:o_kf])
    o_bias = q_lora + kv_lora
    kf = _mm(xn, wx_ref[:, o_kf:o_fq])
    f3 = kf + rows_ref[ROW_LATENT:ROW_LATENT + 1, o_bias:o_bias + LANE]
    put_values(_mm(xn, wx_ref[:, o_fv:o_fv + nv]), nv)

    cqn = _rms(cq, rows_ref[ROW_LATENT:ROW_LATENT + 1, 0:q_lora]).astype(BF16)
    qa = _mm(cqn, wq_ref[...])
    first_half = lane < NOPE + ROPE // 2

    def rotary(x, cos, sin):
        swapped = jnp.where(first_half, pltpu.roll(x, LANE - ROPE // 2, axis=1),
                            pltpu.roll(x, ROPE // 2, axis=1))
        return x * cos + swapped * sin

    for hd in range(N_HEADS):
        sl = slice(hd * HEAD_PAD, (hd + 1) * HEAD_PAD)
        q_ref[0, :, sl] = rotary(qa[:, sl], cos_q, sin_q).astype(BF16)

    ckvn = _rms(ckv, rows_ref[ROW_LATENT:ROW_LATENT + 1, q_lora:q_lora + kv_lora]).astype(BF16)
    kn = _mm(ckvn, wkv_ref[:, 0:nv])
    kr = rotary(kf, cos_k, sin_k)
    kr_both = kr + pltpu.roll(kr, HEAD_V, axis=1)
    put_head_pairs(k_ref, 0, kn, lambda pair: kr_both)
    put_values(_mm(ckvn, wkv_ref[:, nv:2 * nv]), 0)

    live = lane < N_SPLIT * N_HEADS
    log_f = jnp.minimum(f3, 0.0) - jnp.log1p(jnp.exp(-jnp.abs(f3)))
    log_f = jnp.where(live, log_f, 0.0)
    pieces = jnp.concatenate([p.astype(BF16) for p in _split3(log_f)], axis=1)
    csum = _mm(tri_ref[...], pieces)
    fq = _mm(xn, wx_ref[:, o_fq:o_fk])
    cum = csum[:, 0:LANE] + csum[:, LANE:2 * LANE] + csum[:, 2 * LANE:3 * LANE] + carry_ref[...]
    carry_ref[...] = cum[tm - 1:tm, :]
    c_hi, c_mid, c_lo = _split3(cum * LOG2E)
    c_sel = jnp.where(lane < N_HEADS, c_hi, jnp.where(lane < 2 * N_HEADS, c_mid, c_lo))
    c_sel = jnp.where(live, c_sel, 0.0).astype(BF16)
    fk = _mm(xn, wx_ref[:, o_fk:o_fv])
    aug = _mm(c_sel, sel_ref[...])
    aug = aug + rows_ref[ROW_DECAY_ONES:ROW_DECAY_ONES + 1, :]
    put_head_pairs(q_ref, width, fq, lambda pair: aug[:, pair * LANE:(pair + 1) * LANE])
    put_head_pairs(k_ref, width, fk, lambda pair: aug[:, nv + pair * LANE:nv + (pair + 1) * LANE])


def _attn_kernel(q_ref, k_ref, vt_ref, o_ref, s_scr, p_scr, mt_scr, al_scr, m_scr, acc_scr, out_scr, *, bq):
    bk = vt_ref.shape[3]
    n_sub = bq // bk
    nq = q_ref.shape[1] // bq
    heads = [slice(hh * HEAD_PAD, (hh + 1) * HEAD_PAD) for hh in range(2)]

    def k_tile(j):
        return k_ref[0, pl.ds(pl.multiple_of(j * bk, bk), bk), :]

    def q_rows(qi, first, hh):
        return q_ref[0, pl.ds(pl.multiple_of(qi * bq + first, bk), bq - first), heads[hh]]

    def put_scores(u, hh, s):
        s_scr[u, hh, :, 0:bq] = s
        mt_scr[u, hh] = jnp.max(s, axis=0, keepdims=True)

    def scores_group(g, qi):
        for u in range(n_sub):
            kb = k_tile(n_sub * g + u)
            for hh in range(2):
                put_scores(u, hh, _mm_nt(kb[:, heads[hh]], q_rows(qi, 0, hh)))

    def scores_diagonal(qi):
        tri = (lax.broadcasted_iota(jnp.int32, (bk, bk), 0)
               <= lax.broadcasted_iota(jnp.int32, (bk, bk), 1))
        for u in range(n_sub):
            kb = k_tile(n_sub * qi + u)
            for hh in range(2):
                s = _mm_nt(kb[:, heads[hh]], q_rows(qi, u * bk, hh))
                parts = [jnp.full((bk, u * bk), MASKED, F32)] if u > 0 else []
                parts.append(jnp.where(tri, s[:, :bk], MASKED))
                if u < n_sub - 1:
                    parts.append(s[:, bk:])
                put_scores(u, hh, jnp.concatenate(parts, axis=1) if len(parts) > 1 else parts[0])

    def softmax_group():
        for hh in range(2):
            mt = mt_scr[0, hh]
            for u in range(1, n_sub):
                mt = jnp.maximum(mt, mt_scr[u, hh])
            m_new = jnp.maximum(m_scr[hh], mt)
            al_scr[hh] = jnp.exp2(m_scr[hh] - m_new)
            m_scr[hh] = m_new
            for u in range(n_sub):
                p_scr[u, hh, :, 0:bq] = jnp.exp2(s_scr[u, hh, :, 0:bq] - m_new).astype(BF16)

    def values_at(j0):
        ones = jnp.ones((ONES_ROWS, bk), BF16)
        for hh in range(2):
            pv = None
            for u in range(n_sub):
                vb = vt_ref[0, j0 + u]
                v = jnp.concatenate([vb[hh * HEAD_V:(hh + 1) * HEAD_V, :], ones], axis=0)
                d = _mm(v, p_scr[u, hh, :, 0:bq])
                pv = d if pv is None else pv + d
            acc_scr[hh] = al_scr[hh] * acc_scr[hh] + pv

    def last_group_start(qi):
        return jnp.where(qi == 0, 0, n_sub * (qi - 1))

    def reset():
        m_scr[...] = jnp.full_like(m_scr, M_INIT)
        acc_scr[...] = jnp.zeros_like(acc_scr)

    def normalise():
        out_scr[...] = jnp.concatenate(
            [acc_scr[hh, 0:HEAD_V] * (1.0 / acc_scr[hh, HEAD_V:HEAD_V + 1]) for hh in range(2)], axis=0)
        reset()

    def store(qi):
        o_ref[0, pl.ds(pl.multiple_of(qi * bq, bq), bq), :] = out_scr[...].T

    def below_diagonal(qi):
        def body(g, carry):
            softmax_group()
            scores_group(g, qi)
            values_at(jnp.where(g == 0, n_sub * qi, n_sub * (g - 1)))
            return carry

        lax.fori_loop(0, qi, body, 0)

    reset()
    scores_diagonal(0)

    def block(qi, store_previous):
        if store_previous:
            store(qi - 2)
        softmax_group()
        scores_diagonal(qi)
        values_at(last_group_start(qi - 1))
        normalise()
        below_diagonal(qi)

    if nq > 1:
        block(1, False)
        lax.fori_loop(2, nq, lambda qi, carry: (block(qi, True), carry)[1], 0)
        store(nq - 2)
    softmax_group()
    values_at(last_group_start(nq - 1))
    normalise()
    store(nq - 1)


def _post_attn_kernel(o_ref, h_ref, rows_ref, kv_ref, wo_ref, wmq_ref, wmo_ref, out_ref):
    half = o_ref.shape[2] // 2
    o = o_ref[0]
    merged = jnp.concatenate(
        [_rms(o[:, 0:half], rows_ref[ROW_OUT_G:ROW_OUT_G + 1, 0:half]),
         _rms(o[:, half:], rows_ref[ROW_OUT_G:ROW_OUT_G + 1, half:])],
        axis=1).astype(BF16)
    h1 = h_ref[0] + _mm(merged, wo_ref[...])
    xn = _rms(h1, rows_ref[ROW_MEM_Q_G:ROW_MEM_Q_G + 1, :]).astype(BF16)
    q = (_mm(xn, wmq_ref[...]) * (MEM_HD ** -0.5)).astype(BF16)
    kv = kv_ref[0]
    heads = []
    for hd in range(N_MEM_HEADS):
        kh = kv[:, 2 * hd * MEM_HD:(2 * hd + 1) * MEM_HD]
        vh = kv[:, (2 * hd + 1) * MEM_HD:(2 * hd + 2) * MEM_HD]
        s = _mm_nt(q[:, hd * MEM_HD:(hd + 1) * MEM_HD], kh)
        e = jnp.exp(s - jnp.max(s, axis=-1, keepdims=True))
        l = jnp.sum(e, axis=-1, keepdims=True)
        heads.append(_mm(e.astype(BF16), vh) * (1.0 / l))
    om = jnp.concatenate(heads, axis=1).astype(BF16)
    out_ref[0] = h1 + _mm(om, wmo_ref[...])


def _mem_kv_kernel(mem_ref, rows_ref, w_ref, kv_ref):
    g = rows_ref[ROW_MEM_KV_G:ROW_MEM_KV_G + 1, :]
    kv_ref[0] = _mm(_rms(mem_ref[0], g).astype(BF16), w_ref[...]).astype(BF16)


def _ffn_kernel(h_ref, rows_ref, wg_ref, wu_ref, wd_ref, out_ref, *, final_norm):
    half = h_ref.shape[1] // 2
    parts = [slice(0, half), slice(half, 2 * half)]
    hs = [h_ref[0, r, :] for r in parts]
    xn = [_rms(h, rows_ref[ROW_FFN_G:ROW_FFN_G + 1, :]).astype(BF16) for h in hs]
    gu = [(_mm(x, wg_ref[...]), _mm(x, wu_ref[...])) for x in xn]
    act = [(g * (1.0 / (1.0 + jnp.exp(-g))) * u).astype(BF16) for g, u in gu]
    for r, h, a in zip(parts, hs, act):
        y = h + _mm(a, wd_ref[...])
        if final_norm:
            y = _rms(y, rows_ref[ROW_FINAL_G:ROW_FINAL_G + 1, :])
        out_ref[0, r, :] = y


def _const_spec(shape):
    return pl.BlockSpec(shape, lambda *_: (0,) * len(shape), pipeline_mode=pl.Buffered(1))


def _layer_spec(shape, l):
    return pl.BlockSpec((None,) + tuple(shape), lambda *_: (l,) + (0,) * len(shape),
                        pipeline_mode=pl.Buffered(1))


def _decay_constants():
    n_half = N_HEADS * HEAD_V
    sel = np.zeros((LANE, 2 * n_half), np.float32)
    ones = np.zeros((2 * n_half,), np.float32)
    for hd in range(N_HEADS):
        base = (hd // 2) * LANE + (HEAD_V if hd % 2 == 0 else 0)
        for part in range(N_SPLIT):
            src = part * N_HEADS + hd
            sel[src, base + part] = 1.0
            ones[base + N_SPLIT + part] = 1.0
            ones[n_half + base + part] = 1.0
            sel[src, n_half + base + N_SPLIT + part] = -1.0
    return sel, ones


def _rope_tables(seq):
    f32 = np.float32
    inv = (f32(1.0) / (f32(ROPE_THETA) ** (np.arange(0, ROPE, 2, dtype=f32) / f32(ROPE)))).astype(f32)
    ang = np.arange(seq, dtype=f32)[:, None] * inv[None, :]
    cos, sin = np.cos(ang).astype(f32), np.sin(ang).astype(f32)
    tab = np.zeros((seq, 2 * HEAD_PAD), f32)
    tab[:, NOPE:NOPE + ROPE] = np.concatenate([cos, cos], axis=1)
    tab[:, HEAD_PAD + NOPE:HEAD_PAD + NOPE + ROPE] = np.concatenate([-sin, sin], axis=1)
    return jnp.asarray(tab)


def _mixer_weights(w_in, w_uq, w_ukv, q_lora, kv_lora):
    depth = w_in.shape[0]
    fox = N_HEADS * HEAD_V
    o = q_lora + kv_lora
    kr = w_in[:, :, o:o + ROPE]
    o += ROPE
    wfq, wfk, wfv = w_in[:, :, o:o + fox], w_in[:, :, o + fox:o + 2 * fox], w_in[:, :, o + 2 * fox:o + 3 * fox]
    wfl = w_in[:, :, o + 3 * fox:o + 3 * fox + N_HEADS]

    zeros = lambda n: jnp.zeros(w_in.shape[:2] + (n,), BF16)
    cast = lambda w: w.astype(BF16)
    wx = jnp.concatenate(
        [cast(w_in[:, :, 0:q_lora + kv_lora])] + [cast(wfl)] * N_SPLIT
        + [zeros(NOPE - N_SPLIT * N_HEADS), cast(kr), zeros(HEAD_PAD - NOPE - ROPE),
           cast(wfq * (LOG2E * HEAD_V ** -0.5)), cast(wfk), cast(wfv)], axis=2)

    uq = w_uq.reshape(depth, q_lora, N_HEADS, NOPE + ROPE)
    pad_hi = jnp.zeros((depth, q_lora, N_HEADS, HEAD_PAD - NOPE - ROPE), F32)
    wq = jnp.concatenate([uq, pad_hi], axis=3).reshape(depth, q_lora, -1).astype(BF16)

    ukv = w_ukv.reshape(depth, kv_lora, N_HEADS, NOPE + HEAD_V)
    wkv = jnp.concatenate([ukv[..., :NOPE].reshape(depth, kv_lora, -1),
                           ukv[..., NOPE:].reshape(depth, kv_lora, -1)], axis=2).astype(BF16)
    return wx, wq, wkv


def kernel(x, mem, mix_norm_g, w_in, cq_norm_g, ckv_norm_g, w_uq, w_ukv, forget_bias, mla_out_g, fox_out_g, w_out, mem_q_norm_g, mem_kv_norm_g, w_mq, w_mkv, w_mo, ffn_norm_g, w_gate, w_up, w_down, final_norm_g):
    bsz, seq, d_model = x.shape
    depth = w_in.shape[0]
    q_lora = cq_norm_g.shape[1]
    kv_lora = ckv_norm_g.shape[1]
    mem_len = mem.shape[1]
    d_ff = w_gate.shape[2]
    width = N_HEADS * HEAD_PAD
    n_v = 2 * N_HEADS * HEAD_V
    assert d_model == width == n_v, "layout assumes d_model = 8 heads * 128"
    assert seq % BQ == 0 and seq % TM_PROJ == 0 and TM_PROJ % BK == 0 and BQ % BK == 0
    assert seq % TM_POST == 0 and seq % TM_FFN == 0

    tabs = _rope_tables(seq)
    sel_np, decay_ones = _decay_constants()
    sel = jnp.asarray(sel_np, BF16)
    tri = jnp.asarray(np.tril(np.ones((TM_PROJ, TM_PROJ), np.float32)), BF16)
    params = pltpu.CompilerParams

    wx, wq, wkv = _mixer_weights(w_in, w_uq, w_ukv, q_lora, kv_lora)
    w_out_b, w_mq_b, w_mkv_b, w_mo_b = (w.astype(BF16) for w in (w_out, w_mq, w_mkv, w_mo))
    w_gate_b, w_up_b, w_down_b = (w.astype(BF16) for w in (w_gate, w_up, w_down))
    latent = jnp.pad(jnp.concatenate([cq_norm_g, ckv_norm_g] + [forget_bias] * N_SPLIT, axis=1),
                     ((0, 0), (0, d_model - q_lora - kv_lora - N_SPLIT * N_HEADS)))
    per_layer = lambda v: jnp.broadcast_to(v, (depth, d_model))
    rows = jnp.stack([mix_norm_g, latent, per_layer(jnp.asarray(decay_ones)),
                      jnp.concatenate([mla_out_g, fox_out_g], axis=1), mem_q_norm_g, ffn_norm_g,
                      per_layer(final_norm_g), mem_kv_norm_g], axis=1)
    rows_spec = lambda l: _layer_spec((8, d_model), l)

    h = x
    for l in range(depth):
        q_all, k_all, vt_all = pl.pallas_call(
            functools.partial(_proj_in_kernel, q_lora=q_lora, kv_lora=kv_lora),
            grid=(bsz, seq // TM_PROJ),
            in_specs=[
                pl.BlockSpec((1, TM_PROJ, d_model), lambda b, t: (b, t, 0)),
                rows_spec(l),
                pl.BlockSpec((TM_PROJ, 2 * LANE), lambda b, t: (t, 0)),
                _layer_spec(wx.shape[1:], l), _layer_spec(wq.shape[1:], l), _layer_spec(wkv.shape[1:], l),
                _const_spec(sel.shape), _const_spec(tri.shape),
            ],
            out_specs=[
                pl.BlockSpec((1, TM_PROJ, 2 * width), lambda b, t: (b, t, 0)),
                pl.BlockSpec((1, TM_PROJ, 2 * width), lambda b, t: (b, t, 0)),
                pl.BlockSpec((1, TM_PROJ // BK, n_v, BK), lambda b, t: (b, t, 0, 0)),
            ],
            out_shape=[
                jax.ShapeDtypeStruct((bsz, seq, 2 * width), BF16),
                jax.ShapeDtypeStruct((bsz, seq, 2 * width), BF16),
                jax.ShapeDtypeStruct((bsz, seq // BK, n_v, BK), BF16),
            ],
            scratch_shapes=[pltpu.VMEM((1, LANE), F32)],
            compiler_params=params(dimension_semantics=("arbitrary", "arbitrary"),
                                   vmem_limit_bytes=VMEM_LIMIT),
            name=f"proj_in_{l}",
        )(h, rows, tabs, wx, wq, wkv, sel, tri)

        o_all = pl.pallas_call(
            functools.partial(_attn_kernel, bq=BQ),
            grid=(bsz, N_HEADS),
            in_specs=[
                pl.BlockSpec((1, seq, 2 * HEAD_PAD), lambda b, p: (b, 0, p)),
                pl.BlockSpec((1, seq, 2 * HEAD_PAD), lambda b, p: (b, 0, p)),
                pl.BlockSpec((1, seq // BK, 2 * HEAD_V, BK), lambda b, p: (b, 0, p, 0)),
            ],
            out_specs=pl.BlockSpec((1, seq, 2 * HEAD_V), lambda b, p: (b, 0, p)),
            out_shape=jax.ShapeDtypeStruct((bsz, seq, n_v), F32),
            scratch_shapes=[
                pltpu.VMEM((BQ // BK, 2, BK, BQ + 3 * LANE), F32),
                pltpu.VMEM((BQ // BK, 2, BK, BQ + 3 * LANE), BF16),
                pltpu.VMEM((BQ // BK, 2, 1, BQ), F32),
                pltpu.VMEM((2, 1, BQ), F32),
                pltpu.VMEM((2, 1, BQ), F32),
                pltpu.VMEM((2, HEAD_V + ONES_ROWS, BQ), F32),
                pltpu.VMEM((2 * HEAD_V, BQ), F32),
            ],
            compiler_params=params(dimension_semantics=("parallel", "parallel"),
                                   vmem_limit_bytes=VMEM_LIMIT),
            name=f"attn_{l}",
        )(q_all, k_all, vt_all)

        kv_mem = pl.pallas_call(
            _mem_kv_kernel,
            grid=(bsz,),
            in_specs=[
                pl.BlockSpec((1, mem_len, d_model), lambda b: (b, 0, 0)),
                rows_spec(l),
                _layer_spec(w_mkv.shape[1:], l),
            ],
            out_specs=pl.BlockSpec((1, mem_len, w_mkv.shape[2]), lambda b: (b, 0, 0)),
            out_shape=jax.ShapeDtypeStruct((bsz, mem_len, w_mkv.shape[2]), BF16),
            compiler_params=params(dimension_semantics=("parallel",), vmem_limit_bytes=VMEM_LIMIT),
            name=f"mem_kv_{l}",
        )(mem, rows, w_mkv_b)

        h = pl.pallas_call(
            _post_attn_kernel,
            grid=(bsz, seq // TM_POST),
            in_specs=[
                pl.BlockSpec((1, TM_POST, n_v), lambda b, t: (b, t, 0)),
                pl.BlockSpec((1, TM_POST, d_model), lambda b, t: (b, t, 0)),
                rows_spec(l),
                pl.BlockSpec((1, mem_len, w_mkv.shape[2]), lambda b, t: (b, 0, 0)),
                _layer_spec(w_out.shape[1:], l), _layer_spec(w_mq.shape[1:], l), _layer_spec(w_mo.shape[1:], l),
            ],
            out_specs=pl.BlockSpec((1, TM_POST, d_model), lambda b, t: (b, t, 0)),
            out_shape=jax.ShapeDtypeStruct((bsz, seq, d_model), F32),
            compiler_params=params(dimension_semantics=("parallel", "parallel"),
                                   vmem_limit_bytes=VMEM_LIMIT),
            name=f"post_attn_{l}",
        )(o_all, h, rows, kv_mem, w_out_b, w_mq_b, w_mo_b)

        last = l == depth - 1
        h = pl.pallas_call(
            functools.partial(_ffn_kernel, final_norm=last),
            grid=(bsz, seq // TM_FFN),
            in_specs=[
                pl.BlockSpec((1, TM_FFN, d_model), lambda b, t: (b, t, 0)),
                rows_spec(l),
                _layer_spec((d_model, d_ff), l), _layer_spec((d_model, d_ff), l), _layer_spec((d_ff, d_model), l),
            ],
            out_specs=pl.BlockSpec((1, TM_FFN, d_model), lambda b, t: (b, t, 0)),
            out_shape=jax.ShapeDtypeStruct((bsz, seq, d_model), F32),
            compiler_params=params(dimension_semantics=("parallel", "parallel"),
                                   vmem_limit_bytes=VMEM_LIMIT),
            name=f"ffn_{l}",
        )(h, rows, w_gate_b, w_up_b, w_down_b)
    return h
```

```python
import functools

import numpy as np
import jax
import jax.numpy as jnp
from jax import lax
from jax.experimental import pallas as pl
from jax.experimental.pallas import tpu as pltpu

F32 = jnp.float32
BF16 = jnp.bfloat16

EPS = 1e-6
ROPE_THETA = 10000.0
N_HEADS = 8
NOPE = 64
ROPE = 32
HEAD_V = 64
HEAD_PAD = 128
N_MEM_HEADS = 4
MEM_HD = 128
N_SPLIT = 3
ONES_ROWS = 16

LANE = 128
TM_PROJ = 512
TM_POST = 1024
TM_FFN = 512
BQ = 1024
BK = 256
MASKED = -2e30
M_INIT = -1e30
LOG2E = 1.4426950408889634
VMEM_LIMIT = 56 * 1024 * 1024

ROW_MIX_G, ROW_LATENT, ROW_DECAY_ONES, ROW_OUT_G, ROW_MEM_Q_G, ROW_FFN_G, ROW_FINAL_G, ROW_MEM_KV_G = range(8)


def _mm(a, b):
    return jnp.dot(a, b, preferred_element_type=F32)


def _mm_nt(a, b):
    return lax.dot_general(a, b, (((1,), (1,)), ((), ())), preferred_element_type=F32)


def _rms(x, g):
    return x * lax.rsqrt(jnp.mean(x * x, axis=-1, keepdims=True) + EPS) * g


def _split3(x):
    hi = x.astype(BF16).astype(F32)
    r = x - hi
    mid = r.astype(BF16).astype(F32)
    lo = (r - mid).astype(BF16).astype(F32)
    return hi, mid, lo


def _proj_in_kernel(h_ref, rows_ref, tabs_ref, wx_ref, wq_ref, wkv_ref, sel_ref, tri_ref,
                    q_ref, k_ref, vt_ref, carry_ref, *, q_lora, kv_lora):
    tm = h_ref.shape[1]
    n_sub = vt_ref.shape[1]
    bk = vt_ref.shape[3]
    width = N_HEADS * HEAD_PAD
    nv = N_HEADS * HEAD_V

    @pl.when(pl.program_id(1) == 0)
    def _():
        carry_ref[...] = jnp.zeros_like(carry_ref)

    xn = _rms(h_ref[0], rows_ref[ROW_MIX_G:ROW_MIX_G + 1, :]).astype(BF16)

    o_ckv = q_lora
    o_kf = o_ckv + kv_lora
    o_fq = o_kf + LANE
    o_fk = o_fq + nv
    o_fv = o_fk + nv

    cos_k = tabs_ref[:, 0:LANE]
    sin_k = tabs_ref[:, LANE:2 * LANE]
    q_scale = LOG2E * (NOPE + ROPE) ** -0.5
    lane = lax.broadcasted_iota(jnp.int32, (tm, LANE), 1)
    cos_q = jnp.where(lane < NOPE, q_scale, q_scale * cos_k)
    sin_q = q_scale * sin_k
    low = lane < HEAD_V

    def put_values(v, row0):
        vt = v.T.astype(BF16)
        for c in range(n_sub):
            vt_ref[0, c, row0:row0 + nv, :] = vt[:, c * bk:(c + 1) * bk]

    def put_q(col, x):
        q_ref[0, 0, col:col + HEAD_PAD, :] = x.T.astype(BF16)

    def put_k(col, x):
        k_ref[0, :, col:col + HEAD_PAD] = x.astype(BF16)

    def put_head_pairs(put, col0, narrow, fill):
        for pair in range(N_HEADS // 2):
            blk = narrow[:, pair * LANE:(pair + 1) * LANE]
            other = fill(pair)
            c = col0 + 2 * pair * HEAD_PAD
            put(c, jnp.where(low, blk, other))
            put(c + HEAD_PAD, pltpu.roll(jnp.where(low, other, blk), HEAD_V, axis=1))

    cq = _mm(xn, wx_ref[:, 0:q_lora])
    ckv = _mm(xn, wx_ref[:, o_ckv:o_kf])
    o_bias = q_lora + kv_lora
    kf = _mm(xn, wx_ref[:, o_kf:o_fq])
    f3 = kf + rows_ref[ROW_LATENT:ROW_LATENT + 1, o_bias:o_bias + LANE]
    put_values(_mm(xn, wx_ref[:, o_fv:o_fv + nv]), nv)

    cqn = _rms(cq, rows_ref[ROW_LATENT:ROW_LATENT + 1, 0:q_lora]).astype(BF16)
    qa = _mm(cqn, wq_ref[...])
    first_half = lane < NOPE + ROPE // 2

    def rotary(x, cos, sin):
        swapped = jnp.where(first_half, pltpu.roll(x, LANE - ROPE // 2, axis=1),
                            pltpu.roll(x, ROPE // 2, axis=1))
        return x * cos + swapped * sin

    for hd in range(N_HEADS):
        put_q(hd * HEAD_PAD, rotary(qa[:, hd * HEAD_PAD:(hd + 1) * HEAD_PAD], cos_q, sin_q))

    ckvn = _rms(ckv, rows_ref[ROW_LATENT:ROW_LATENT + 1, q_lora:q_lora + kv_lora]).astype(BF16)
    kn = _mm(ckvn, wkv_ref[:, 0:nv])
    kr = rotary(kf, cos_k, sin_k)
    kr_both = kr + pltpu.roll(kr, HEAD_V, axis=1)
    put_head_pairs(put_k, 0, kn, lambda pair: kr_both)
    put_values(_mm(ckvn, wkv_ref[:, nv:2 * nv]), 0)

    live = lane < N_SPLIT * N_HEADS
    log_f = jnp.minimum(f3, 0.0) - jnp.log1p(jnp.exp(-jnp.abs(f3)))
    log_f = jnp.where(live, log_f, 0.0)
    pieces = jnp.concatenate([p.astype(BF16) for p in _split3(log_f)], axis=1)
    csum = _mm(tri_ref[...], pieces)
    fq = _mm(xn, wx_ref[:, o_fq:o_fk])
    cum = csum[:, 0:LANE] + csum[:, LANE:2 * LANE] + csum[:, 2 * LANE:3 * LANE] + carry_ref[...]
    carry_ref[...] = cum[tm - 1:tm, :]
    c_hi, c_mid, c_lo = _split3(cum * LOG2E)
    c_sel = jnp.where(lane < N_HEADS, c_hi, jnp.where(lane < 2 * N_HEADS, c_mid, c_lo))
    c_sel = jnp.where(live, c_sel, 0.0).astype(BF16)
    fk = _mm(xn, wx_ref[:, o_fk:o_fv])
    aug = _mm(c_sel, sel_ref[...])
    aug = aug + rows_ref[ROW_DECAY_ONES:ROW_DECAY_ONES + 1, :]
    put_head_pairs(put_q, width, fq, lambda pair: aug[:, pair * LANE:(pair + 1) * LANE])
    put_head_pairs(put_k, width, fk, lambda pair: aug[:, nv + pair * LANE:nv + (pair + 1) * LANE])


def _attn_kernel(q_ref, k_ref, vt_ref, o_ref, s_scr, p_scr, mt_scr, al_scr, m_scr, acc_scr, out_scr, *, bq):
    bk = vt_ref.shape[3]
    n_sub = bq // bk
    nq = k_ref.shape[1] // bq
    q_tile = q_ref.shape[3]
    heads = [slice(hh * HEAD_PAD, (hh + 1) * HEAD_PAD) for hh in range(2)]

    def k_tile(j):
        return k_ref[0, pl.ds(pl.multiple_of(j * bk, bk), bk), :]

    def q_cols(qi, first, hh):
        per = bq // q_tile
        parts = [q_ref[0, per * qi + t, heads[hh], max(first - t * q_tile, 0):]
                 for t in range(per) if first < (t + 1) * q_tile]
        return parts[0] if len(parts) == 1 else jnp.concatenate(parts, axis=1)

    def put_scores(u, hh, s):
        s_scr[u, hh, :, 0:bq] = s
        mt_scr[u, hh] = jnp.max(s, axis=0, keepdims=True)

    def scores_group(g, qi):
        for u in range(n_sub):
            kb = k_tile(n_sub * g + u)
            for hh in range(2):
                put_scores(u, hh, _mm(kb[:, heads[hh]], q_cols(qi, 0, hh)))

    def scores_diagonal(qi):
        tri = (lax.broadcasted_iota(jnp.int32, (bk, bk), 0)
               <= lax.broadcasted_iota(jnp.int32, (bk, bk), 1))
        for u in range(n_sub):
            kb = k_tile(n_sub * qi + u)
            for hh in range(2):
                s = _mm(kb[:, heads[hh]], q_cols(qi, u * bk, hh))
                parts = [jnp.full((bk, u * bk), MASKED, F32)] if u > 0 else []
                parts.append(jnp.where(tri, s[:, :bk], MASKED))
                if u < n_sub - 1:
                    parts.append(s[:, bk:])
                put_scores(u, hh, jnp.concatenate(parts, axis=1) if len(parts) > 1 else parts[0])

    def softmax_group():
        for hh in range(2):
            mt = mt_scr[0, hh]
            for u in range(1, n_sub):
                mt = jnp.maximum(mt, mt_scr[u, hh])
            m_new = jnp.maximum(m_scr[hh], mt)
            al_scr[hh] = jnp.exp2(m_scr[hh] - m_new)
            m_scr[hh] = m_new
            for u in range(n_sub):
                p_scr[u, hh, :, 0:bq] = jnp.exp2(s_scr[u, hh, :, 0:bq] - m_new).astype(BF16)

    def values_at(j0):
        ones = jnp.ones((ONES_ROWS, bk), BF16)
        for hh in range(2):
            pv = None
            for u in range(n_sub):
                vb = vt_ref[0, j0 + u]
                v = jnp.concatenate([vb[hh * HEAD_V:(hh + 1) * HEAD_V, :], ones], axis=0)
                d = _mm(v, p_scr[u, hh, :, 0:bq])
                pv = d if pv is None else pv + d
            acc_scr[hh] = al_scr[hh] * acc_scr[hh] + pv

    def last_group_start(qi):
        return jnp.where(qi == 0, 0, n_sub * (qi - 1))

    def reset():
        m_scr[...] = jnp.full_like(m_scr, M_INIT)
        acc_scr[...] = jnp.zeros_like(acc_scr)

    def normalise():
        out_scr[...] = jnp.concatenate(
            [acc_scr[hh, 0:HEAD_V] * (1.0 / acc_scr[hh, HEAD_V:HEAD_V + 1]) for hh in range(2)], axis=0)
        reset()

    def store(qi):
        o_ref[0, pl.ds(pl.multiple_of(qi * bq, bq), bq), :] = out_scr[...].T

    def below_diagonal(qi):
        def body(g, carry):
            softmax_group()
            scores_group(g, qi)
            values_at(jnp.where(g == 0, n_sub * qi, n_sub * (g - 1)))
            return carry

        lax.fori_loop(0, qi, body, 0)

    reset()
    scores_diagonal(0)

    def block(qi, store_previous):
        if store_previous:
            store(qi - 2)
        softmax_group()
        scores_diagonal(qi)
        values_at(last_group_start(qi - 1))
        normalise()
        below_diagonal(qi)

    if nq > 1:
        block(1, False)
        lax.fori_loop(2, nq, lambda qi, carry: (block(qi, True), carry)[1], 0)
        store(nq - 2)
    softmax_group()
    values_at(last_group_start(nq - 1))
    normalise()
    store(nq - 1)


def _post_attn_kernel(o_ref, h_ref, rows_ref, kv_ref, wo_ref, wmq_ref, wmo_ref, out_ref):
    half = o_ref.shape[2] // 2
    o = o_ref[0]
    merged = jnp.concatenate(
        [_rms(o[:, 0:half], rows_ref[ROW_OUT_G:ROW_OUT_G + 1, 0:half]),
         _rms(o[:, half:], rows_ref[ROW_OUT_G:ROW_OUT_G + 1, half:])],
        axis=1).astype(BF16)
    h1 = h_ref[0] + _mm(merged, wo_ref[...])
    xn = _rms(h1, rows_ref[ROW_MEM_Q_G:ROW_MEM_Q_G + 1, :]).astype(BF16)
    q = (_mm(xn, wmq_ref[...]) * (MEM_HD ** -0.5)).astype(BF16)
    kv = kv_ref[0]
    heads = []
    for hd in range(N_MEM_HEADS):
        kh = kv[:, 2 * hd * MEM_HD:(2 * hd + 1) * MEM_HD]
        vh = kv[:, (2 * hd + 1) * MEM_HD:(2 * hd + 2) * MEM_HD]
        s = _mm_nt(q[:, hd * MEM_HD:(hd + 1) * MEM_HD], kh)
        e = jnp.exp(s - jnp.max(s, axis=-1, keepdims=True))
        l = jnp.sum(e, axis=-1, keepdims=True)
        heads.append(_mm(e.astype(BF16), vh) * (1.0 / l))
    om = jnp.concatenate(heads, axis=1).astype(BF16)
    out_ref[0] = h1 + _mm(om, wmo_ref[...])


def _mem_kv_kernel(mem_ref, rows_ref, w_ref, kv_ref):
    g = rows_ref[ROW_MEM_KV_G:ROW_MEM_KV_G + 1, :]
    kv_ref[0] = _mm(_rms(mem_ref[0], g).astype(BF16), w_ref[...]).astype(BF16)


def _ffn_kernel(h_ref, rows_ref, wg_ref, wu_ref, wd_ref, out_ref, *, final_norm):
    half = h_ref.shape[1] // 2
    parts = [slice(0, half), slice(half, 2 * half)]
    hs = [h_ref[0, r, :] for r in parts]
    xn = [_rms(h, rows_ref[ROW_FFN_G:ROW_FFN_G + 1, :]).astype(BF16) for h in hs]
    gu = [(_mm(x, wg_ref[...]), _mm(x, wu_ref[...])) for x in xn]
    act = [(g * (1.0 / (1.0 + jnp.exp(-g))) * u).astype(BF16) for g, u in gu]
    for r, h, a in zip(parts, hs, act):
        y = h + _mm(a, wd_ref[...])
        if final_norm:
            y = _rms(y, rows_ref[ROW_FINAL_G:ROW_FINAL_G + 1, :])
        out_ref[0, r, :] = y


def _const_spec(shape):
    return pl.BlockSpec(shape, lambda *_: (0,) * len(shape), pipeline_mode=pl.Buffered(1))


def _layer_spec(shape, l):
    return pl.BlockSpec((None,) + tuple(shape), lambda *_: (l,) + (0,) * len(shape),
                        pipeline_mode=pl.Buffered(1))


def _decay_constants():
    n_half = N_HEADS * HEAD_V
    sel = np.zeros((LANE, 2 * n_half), np.float32)
    ones = np.zeros((2 * n_half,), np.float32)
    for hd in range(N_HEADS):
        base = (hd // 2) * LANE + (HEAD_V if hd % 2 == 0 else 0)
        for part in range(N_SPLIT):
            src = part * N_HEADS + hd
            sel[src, base + part] = 1.0
            ones[base + N_SPLIT + part] = 1.0
            ones[n_half + base + part] = 1.0
            sel[src, n_half + base + N_SPLIT + part] = -1.0
    return sel, ones


def _rope_tables(seq):
    f32 = np.float32
    inv = (f32(1.0) / (f32(ROPE_THETA) ** (np.arange(0, ROPE, 2, dtype=f32) / f32(ROPE)))).astype(f32)
    ang = np.arange(seq, dtype=f32)[:, None] * inv[None, :]
    cos, sin = np.cos(ang).astype(f32), np.sin(ang).astype(f32)
    tab = np.zeros((seq, 2 * HEAD_PAD), f32)
    tab[:, NOPE:NOPE + ROPE] = np.concatenate([cos, cos], axis=1)
    tab[:, HEAD_PAD + NOPE:HEAD_PAD + NOPE + ROPE] = np.concatenate([-sin, sin], axis=1)
    return jnp.asarray(tab)


def _mixer_weights(w_in, w_uq, w_ukv, q_lora, kv_lora):
    depth = w_in.shape[0]
    fox = N_HEADS * HEAD_V
    o = q_lora + kv_lora
    kr = w_in[:, :, o:o + ROPE]
    o += ROPE
    wfq, wfk, wfv = w_in[:, :, o:o + fox], w_in[:, :, o + fox:o + 2 * fox], w_in[:, :, o + 2 * fox:o + 3 * fox]
    wfl = w_in[:, :, o + 3 * fox:o + 3 * fox + N_HEADS]

    zeros = lambda n: jnp.zeros(w_in.shape[:2] + (n,), BF16)
    cast = lambda w: w.astype(BF16)
    wx = jnp.concatenate(
        [cast(w_in[:, :, 0:q_lora + kv_lora])] + [cast(wfl)] * N_SPLIT
        + [zeros(NOPE - N_SPLIT * N_HEADS), cast(kr), zeros(HEAD_PAD - NOPE - ROPE),
           cast(wfq * (LOG2E * HEAD_V ** -0.5)), cast(wfk), cast(wfv)], axis=2)

    uq = w_uq.reshape(depth, q_lora, N_HEADS, NOPE + ROPE)
    pad_hi = jnp.zeros((depth, q_lora, N_HEADS, HEAD_PAD - NOPE - ROPE), F32)
    wq = jnp.concatenate([uq, pad_hi], axis=3).reshape(depth, q_lora, -1).astype(BF16)

    ukv = w_ukv.reshape(depth, kv_lora, N_HEADS, NOPE + HEAD_V)
    wkv = jnp.concatenate([ukv[..., :NOPE].reshape(depth, kv_lora, -1),
                           ukv[..., NOPE:].reshape(depth, kv_lora, -1)], axis=2).astype(BF16)
    return wx, wq, wkv


def kernel(x, mem, mix_norm_g, w_in, cq_norm_g, ckv_norm_g, w_uq, w_ukv, forget_bias, mla_out_g, fox_out_g, w_out, mem_q_norm_g, mem_kv_norm_g, w_mq, w_mkv, w_mo, ffn_norm_g, w_gate, w_up, w_down, final_norm_g):
    bsz, seq, d_model = x.shape
    depth = w_in.shape[0]
    q_lora = cq_norm_g.shape[1]
    kv_lora = ckv_norm_g.shape[1]
    mem_len = mem.shape[1]
    d_ff = w_gate.shape[2]
    width = N_HEADS * HEAD_PAD
    n_v = 2 * N_HEADS * HEAD_V
    assert d_model == width == n_v, "layout assumes d_model = 8 heads * 128"
    assert seq % BQ == 0 and seq % TM_PROJ == 0 and TM_PROJ % BK == 0 and BQ % BK == 0
    assert seq % TM_POST == 0 and seq % TM_FFN == 0 and BQ % TM_PROJ == 0 and TM_PROJ % BK == 0

    tabs = _rope_tables(seq)
    sel_np, decay_ones = _decay_constants()
    sel = jnp.asarray(sel_np, BF16)
    tri = jnp.asarray(np.tril(np.ones((TM_PROJ, TM_PROJ), np.float32)), BF16)
    params = pltpu.CompilerParams

    wx, wq, wkv = _mixer_weights(w_in, w_uq, w_ukv, q_lora, kv_lora)
    w_out_b, w_mq_b, w_mkv_b, w_mo_b = (w.astype(BF16) for w in (w_out, w_mq, w_mkv, w_mo))
    w_gate_b, w_up_b, w_down_b = (w.astype(BF16) for w in (w_gate, w_up, w_down))
    latent = jnp.pad(jnp.concatenate([cq_norm_g, ckv_norm_g] + [forget_bias] * N_SPLIT, axis=1),
                     ((0, 0), (0, d_model - q_lora - kv_lora - N_SPLIT * N_HEADS)))
    per_layer = lambda v: jnp.broadcast_to(v, (depth, d_model))
    rows = jnp.stack([mix_norm_g, latent, per_layer(jnp.asarray(decay_ones)),
                      jnp.concatenate([mla_out_g, fox_out_g], axis=1), mem_q_norm_g, ffn_norm_g,
                      per_layer(final_norm_g), mem_kv_norm_g], axis=1)
    rows_spec = lambda l: _layer_spec((8, d_model), l)

    h = x
    for l in range(depth):
        q_all, k_all, vt_all = pl.pallas_call(
            functools.partial(_proj_in_kernel, q_lora=q_lora, kv_lora=kv_lora),
            grid=(bsz, seq // TM_PROJ),
            in_specs=[
                pl.BlockSpec((1, TM_PROJ, d_model), lambda b, t: (b, t, 0)),
                rows_spec(l),
                pl.BlockSpec((TM_PROJ, 2 * LANE), lambda b, t: (t, 0)),
                _layer_spec(wx.shape[1:], l), _layer_spec(wq.shape[1:], l), _layer_spec(wkv.shape[1:], l),
                _const_spec(sel.shape), _const_spec(tri.shape),
            ],
            out_specs=[
                pl.BlockSpec((1, 1, 2 * width, TM_PROJ), lambda b, t: (b, t, 0, 0)),
                pl.BlockSpec((1, TM_PROJ, 2 * width), lambda b, t: (b, t, 0)),
                pl.BlockSpec((1, TM_PROJ // BK, n_v, BK), lambda b, t: (b, t, 0, 0)),
            ],
            out_shape=[
                jax.ShapeDtypeStruct((bsz, seq // TM_PROJ, 2 * width, TM_PROJ), BF16),
                jax.ShapeDtypeStruct((bsz, seq, 2 * width), BF16),
                jax.ShapeDtypeStruct((bsz, seq // BK, n_v, BK), BF16),
            ],
            scratch_shapes=[pltpu.VMEM((1, LANE), F32)],
            compiler_params=params(dimension_semantics=("arbitrary", "arbitrary"),
                                   vmem_limit_bytes=VMEM_LIMIT),
            name=f"proj_in_{l}",
        )(h, rows, tabs, wx, wq, wkv, sel, tri)

        o_all = pl.pallas_call(
            functools.partial(_attn_kernel, bq=BQ),
            grid=(bsz, N_HEADS),
            in_specs=[
                pl.BlockSpec((1, seq // TM_PROJ, 2 * HEAD_PAD, TM_PROJ), lambda b, p: (b, 0, p, 0)),
                pl.BlockSpec((1, seq, 2 * HEAD_PAD), lambda b, p: (b, 0, p)),
                pl.BlockSpec((1, seq // BK, 2 * HEAD_V, BK), lambda b, p: (b, 0, p, 0)),
            ],
            out_specs=pl.BlockSpec((1, seq, 2 * HEAD_V), lambda b, p: (b, 0, p)),
            out_shape=jax.ShapeDtypeStruct((bsz, seq, n_v), F32),
            scratch_shapes=[
                pltpu.VMEM((BQ // BK, 2, BK, BQ + 3 * LANE), F32),
                pltpu.VMEM((BQ // BK, 2, BK, BQ + 3 * LANE), BF16),
                pltpu.VMEM((BQ // BK, 2, 1, BQ), F32),
                pltpu.VMEM((2, 1, BQ), F32),
                pltpu.VMEM((2, 1, BQ), F32),
                pltpu.VMEM((2, HEAD_V + ONES_ROWS, BQ), F32),
                pltpu.VMEM((2 * HEAD_V, BQ), F32),
            ],
            compiler_params=params(dimension_semantics=("parallel", "parallel"),
                                   vmem_limit_bytes=VMEM_LIMIT),
            name=f"attn_{l}",
        )(q_all, k_all, vt_all)

        kv_mem = pl.pallas_call(
            _mem_kv_kernel,
            grid=(bsz,),
            in_specs=[
                pl.BlockSpec((1, mem_len, d_model), lambda b: (b, 0, 0)),
                rows_spec(l),
                _layer_spec(w_mkv.shape[1:], l),
            ],
            out_specs=pl.BlockSpec((1, mem_len, w_mkv.shape[2]), lambda b: (b, 0, 0)),
            out_shape=jax.ShapeDtypeStruct((bsz, mem_len, w_mkv.shape[2]), BF16),
            compiler_params=params(dimension_semantics=("parallel",), vmem_limit_bytes=VMEM_LIMIT),
            name=f"mem_kv_{l}",
        )(mem, rows, w_mkv_b)

        h = pl.pallas_call(
            _post_attn_kernel,
            grid=(bsz, seq // TM_POST),
            in_specs=[
                pl.BlockSpec((1, TM_POST, n_v), lambda b, t: (b, t, 0)),
                pl.BlockSpec((1, TM_POST, d_model), lambda b, t: (b, t, 0)),
                rows_spec(l),
                pl.BlockSpec((1, mem_len, w_mkv.shape[2]), lambda b, t: (b, 0, 0)),
                _layer_spec(w_out.shape[1:], l), _layer_spec(w_mq.shape[1:], l), _layer_spec(w_mo.shape[1:], l),
            ],
            out_specs=pl.BlockSpec((1, TM_POST, d_model), lambda b, t: (b, t, 0)),
            out_shape=jax.ShapeDtypeStruct((bsz, seq, d_model), F32),
            compiler_params=params(dimension_semantics=("parallel", "parallel"),
                                   vmem_limit_bytes=VMEM_LIMIT),
            name=f"post_attn_{l}",
        )(o_all, h, rows, kv_mem, w_out_b, w_mq_b, w_mo_b)

        last = l == depth - 1
        h = pl.pallas_call(
            functools.partial(_ffn_kernel, final_norm=last),
            grid=(bsz, seq // TM_FFN),
            in_specs=[
                pl.BlockSpec((1, TM_FFN, d_model), lambda b, t: (b, t, 0)),
                rows_spec(l),
                _layer_spec((d_model, d_ff), l), _layer_spec((d_model, d_ff), l), _layer_spec((d_ff, d_model), l),
            ],
            out_specs=pl.BlockSpec((1, TM_FFN, d_model), lambda b, t: (b, t, 0)),
            out_shape=jax.ShapeDtypeStruct((bsz, seq, d_model), F32),
            compiler_params=params(dimension_semantics=("parallel", "parallel"),
                                   vmem_limit_bytes=VMEM_LIMIT),
            name=f"ffn_{l}",
        )(h, rows, w_gate_b, w_up_b, w_down_b)
    return h
```

```python
import functools

import numpy as np
import jax
import jax.numpy as jnp
from jax import lax
from jax.experimental import pallas as pl
from jax.experimental.pallas import tpu as pltpu

F32 = jnp.float32
BF16 = jnp.bfloat16

EPS = 1e-6
ROPE_THETA = 10000.0
N_HEADS = 8
NOPE = 64
ROPE = 32
HEAD_V = 64
HEAD_PAD = 128
N_MEM_HEADS = 4
MEM_HD = 128
N_SPLIT = 3
ONES_ROWS = 16

LANE = 128
TM_PROJ = 512
TM_POST = 1024
TM_FFN = 512
BQ = 1024
BK = 256
MASKED = -2e30
M_INIT = -1e30
LOG2E = 1.4426950408889634
VMEM_LIMIT = 56 * 1024 * 1024

ROW_MIX_G, ROW_LATENT, ROW_DECAY_ONES, ROW_OUT_G, ROW_MEM_Q_G, ROW_FFN_G, ROW_FINAL_G, ROW_MEM_KV_G = range(8)


def _mm(a, b):
    return jnp.dot(a, b, preferred_element_type=F32)


def _mm_nt(a, b):
    return lax.dot_general(a, b, (((1,), (1,)), ((), ())), preferred_element_type=F32)


def _rms(x, g):
    return x * lax.rsqrt(jnp.mean(x * x, axis=-1, keepdims=True) + EPS) * g


def _split3(x):
    hi = x.astype(BF16).astype(F32)
    r = x - hi
    mid = r.astype(BF16).astype(F32)
    lo = (r - mid).astype(BF16).astype(F32)
    return hi, mid, lo


def _proj_in_kernel(h_ref, rows_ref, tabs_ref, wx_ref, wq_ref, wkv_ref, sel_ref, tri_ref,
                    q_ref, k_ref, vt_ref, carry_ref, *, q_lora, kv_lora):
    tm = h_ref.shape[1]
    n_sub = vt_ref.shape[1]
    bk = vt_ref.shape[3]
    width = N_HEADS * HEAD_PAD
    nv = N_HEADS * HEAD_V

    @pl.when(pl.program_id(1) == 0)
    def _():
        carry_ref[...] = jnp.zeros_like(carry_ref)

    xn = _rms(h_ref[0], rows_ref[ROW_MIX_G:ROW_MIX_G + 1, :]).astype(BF16)

    o_ckv = q_lora
    o_kf = o_ckv + kv_lora
    o_fq = o_kf + LANE
    o_fk = o_fq + nv
    o_fv = o_fk + nv

    cos_k = tabs_ref[:, 0:LANE]
    sin_k = tabs_ref[:, LANE:2 * LANE]
    q_scale = LOG2E * (NOPE + ROPE) ** -0.5
    lane = lax.broadcasted_iota(jnp.int32, (tm, LANE), 1)
    cos_q = jnp.where(lane < NOPE, q_scale, q_scale * cos_k)
    sin_q = q_scale * sin_k
    low = lane < HEAD_V

    def put_values(v, row0):
        vt = v.T.astype(BF16)
        for c in range(n_sub):
            vt_ref[0, c, row0:row0 + nv, :] = vt[:, c * bk:(c + 1) * bk]

    def put_q(col, x):
        q_ref[0, 0, col:col + HEAD_PAD, :] = x.T.astype(BF16)

    def put_k(col, x):
        k_ref[0, :, col:col + HEAD_PAD] = x.astype(BF16)

    def put_head_pairs(put, col0, narrow, fill):
        for pair in range(N_HEADS // 2):
            blk = narrow[:, pair * LANE:(pair + 1) * LANE]
            other = fill(pair)
            c = col0 + 2 * pair * HEAD_PAD
            put(c, jnp.where(low, blk, other))
            put(c + HEAD_PAD, pltpu.roll(jnp.where(low, other, blk), HEAD_V, axis=1))

    cq = _mm(xn, wx_ref[:, 0:q_lora])
    ckv = _mm(xn, wx_ref[:, o_ckv:o_kf])
    o_bias = q_lora + kv_lora
    kf = _mm(xn, wx_ref[:, o_kf:o_fq])
    f3 = kf + rows_ref[ROW_LATENT:ROW_LATENT + 1, o_bias:o_bias + LANE]
    put_values(_mm(xn, wx_ref[:, o_fv:o_fv + nv]), nv)

    cqn = _rms(cq, rows_ref[ROW_LATENT:ROW_LATENT + 1, 0:q_lora]).astype(BF16)
    qa = _mm(cqn, wq_ref[...])
    first_half = lane < NOPE + ROPE // 2

    def rotary(x, cos, sin):
        swapped = jnp.where(first_half, pltpu.roll(x, LANE - ROPE // 2, axis=1),
                            pltpu.roll(x, ROPE // 2, axis=1))
        return x * cos + swapped * sin

    for hd in range(N_HEADS):
        put_q(hd * HEAD_PAD, rotary(qa[:, hd * HEAD_PAD:(hd + 1) * HEAD_PAD], cos_q, sin_q))

    ckvn = _rms(ckv, rows_ref[ROW_LATENT:ROW_LATENT + 1, q_lora:q_lora + kv_lora]).astype(BF16)
    kn = _mm(ckvn, wkv_ref[:, 0:nv])
    kr = rotary(kf, cos_k, sin_k)
    kr_both = kr + pltpu.roll(kr, HEAD_V, axis=1)
    put_head_pairs(put_k, 0, kn, lambda pair: kr_both)
    put_values(_mm(ckvn, wkv_ref[:, nv:2 * nv]), 0)

    live = lane < N_SPLIT * N_HEADS
    log_f = jnp.minimum(f3, 0.0) - jnp.log1p(jnp.exp(-jnp.abs(f3)))
    log_f = jnp.where(live, log_f, 0.0)
    pieces = jnp.concatenate([p.astype(BF16) for p in _split3(log_f)], axis=1)
    csum = _mm(tri_ref[...], pieces)
    fq = _mm(xn, wx_ref[:, o_fq:o_fk])
    cum = csum[:, 0:LANE] + csum[:, LANE:2 * LANE] + csum[:, 2 * LANE:3 * LANE] + carry_ref[...]
    carry_ref[...] = cum[tm - 1:tm, :]
    c_hi, c_mid, c_lo = _split3(cum * LOG2E)
    c_sel = jnp.where(lane < N_HEADS, c_hi, jnp.where(lane < 2 * N_HEADS, c_mid, c_lo))
    c_sel = jnp.where(live, c_sel, 0.0).astype(BF16)
    fk = _mm(xn, wx_ref[:, o_fk:o_fv])
    aug = _mm(c_sel, sel_ref[...])
    aug = aug + rows_ref[ROW_DECAY_ONES:ROW_DECAY_ONES + 1, :]
    put_head_pairs(put_q, width, fq, lambda pair: aug[:, pair * LANE:(pair + 1) * LANE])
    put_head_pairs(put_k, width, fk, lambda pair: aug[:, nv + pair * LANE:nv + (pair + 1) * LANE])


def _attn_kernel(q_ref, k_ref, vt_ref, o_ref, s_scr, p_scr, mt_scr, al_scr, m_scr, acc_scr, out_scr, *, bq):
    bk = vt_ref.shape[3]
    n_sub = bq // bk
    nq = k_ref.shape[1] // bq
    q_tile = q_ref.shape[3]
    heads = [slice(hh * HEAD_PAD, (hh + 1) * HEAD_PAD) for hh in range(2)]

    def k_tile(j):
        return k_ref[0, pl.ds(pl.multiple_of(j * bk, bk), bk), :]

    def q_cols(qi, first, hh):
        per = bq // q_tile
        parts = [q_ref[0, per * qi + t, heads[hh], max(first - t * q_tile, 0):]
                 for t in range(per) if first < (t + 1) * q_tile]
        return parts[0] if len(parts) == 1 else jnp.concatenate(parts, axis=1)

    def put_scores(u, hh, s):
        s_scr[u, hh, :, 0:bq] = s
        mt_scr[u, hh] = jnp.max(s, axis=0, keepdims=True)

    def scores_group(g, qi):
        k_group = k_ref[0, pl.ds(pl.multiple_of(g * bq, bq), bq), :]
        for hh in range(2):
            s = _mm(k_group[:, heads[hh]], q_cols(qi, 0, hh))
            for u in range(n_sub):
                put_scores(u, hh, s[u * bk:(u + 1) * bk, :])

    def scores_diagonal(qi):
        tri = (lax.broadcasted_iota(jnp.int32, (bk, bk), 0)
               <= lax.broadcasted_iota(jnp.int32, (bk, bk), 1))
        for u in range(n_sub):
            kb = k_tile(n_sub * qi + u)
            for hh in range(2):
                s = _mm(kb[:, heads[hh]], q_cols(qi, u * bk, hh))
                parts = [jnp.full((bk, u * bk), MASKED, F32)] if u > 0 else []
                parts.append(jnp.where(tri, s[:, :bk], MASKED))
                if u < n_sub - 1:
                    parts.append(s[:, bk:])
                put_scores(u, hh, jnp.concatenate(parts, axis=1) if len(parts) > 1 else parts[0])

    def softmax_group():
        for hh in range(2):
            mt = mt_scr[0, hh]
            for u in range(1, n_sub):
                mt = jnp.maximum(mt, mt_scr[u, hh])
            m_new = jnp.maximum(m_scr[hh], mt)
            al_scr[hh] = jnp.exp2(m_scr[hh] - m_new)
            m_scr[hh] = m_new
            for u in range(n_sub):
                p_scr[u, hh, :, 0:bq] = jnp.exp2(s_scr[u, hh, :, 0:bq] - m_new).astype(BF16)

    def values_at(j0):
        ones = jnp.ones((ONES_ROWS, bk), BF16)
        for hh in range(2):
            pv = None
            for u in range(n_sub):
                vb = vt_ref[0, j0 + u]
                v = jnp.concatenate([vb[hh * HEAD_V:(hh + 1) * HEAD_V, :], ones], axis=0)
                d = _mm(v, p_scr[u, hh, :, 0:bq])
                pv = d if pv is None else pv + d
            acc_scr[hh] = al_scr[hh] * acc_scr[hh] + pv

    def last_group_start(qi):
        return jnp.where(qi == 0, 0, n_sub * (qi - 1))

    def reset():
        m_scr[...] = jnp.full_like(m_scr, M_INIT)
        acc_scr[...] = jnp.zeros_like(acc_scr)

    def normalise():
        out_scr[...] = jnp.concatenate(
            [acc_scr[hh, 0:HEAD_V] * (1.0 / acc_scr[hh, HEAD_V:HEAD_V + 1]) for hh in range(2)], axis=0)
        reset()

    def store(qi):
        o_ref[0, pl.ds(pl.multiple_of(qi * bq, bq), bq), :] = out_scr[...].T

    def below_diagonal(qi):
        def body(g, carry):
            softmax_group()
            scores_group(g, qi)
            values_at(jnp.where(g == 0, n_sub * qi, n_sub * (g - 1)))
            return carry

        lax.fori_loop(0, qi, body, 0)

    reset()
    scores_diagonal(0)

    def block(qi, store_previous):
        if store_previous:
            store(qi - 2)
        softmax_group()
        scores_diagonal(qi)
        values_at(last_group_start(qi - 1))
        normalise()
        below_diagonal(qi)

    if nq > 1:
        block(1, False)
        lax.fori_loop(2, nq, lambda qi, carry: (block(qi, True), carry)[1], 0)
        store(nq - 2)
    softmax_group()
    values_at(last_group_start(nq - 1))
    normalise()
    store(nq - 1)


def _post_attn_kernel(o_ref, h_ref, rows_ref, kv_ref, wo_ref, wmq_ref, wmo_ref, out_ref):
    half = o_ref.shape[2] // 2
    o = o_ref[0]
    merged = jnp.concatenate(
        [_rms(o[:, 0:half], rows_ref[ROW_OUT_G:ROW_OUT_G + 1, 0:half]),
         _rms(o[:, half:], rows_ref[ROW_OUT_G:ROW_OUT_G + 1, half:])],
        axis=1).astype(BF16)
    h1 = h_ref[0] + _mm(merged, wo_ref[...])
    xn = _rms(h1, rows_ref[ROW_MEM_Q_G:ROW_MEM_Q_G + 1, :]).astype(BF16)
    q = (_mm(xn, wmq_ref[...]) * (MEM_HD ** -0.5)).astype(BF16)
    kv = kv_ref[0]
    heads = []
    for hd in range(N_MEM_HEADS):
        kh = kv[:, 2 * hd * MEM_HD:(2 * hd + 1) * MEM_HD]
        vh = kv[:, (2 * hd + 1) * MEM_HD:(2 * hd + 2) * MEM_HD]
        s = _mm_nt(q[:, hd * MEM_HD:(hd + 1) * MEM_HD], kh)
        e = jnp.exp(s - jnp.max(s, axis=-1, keepdims=True))
        l = jnp.sum(e, axis=-1, keepdims=True)
        heads.append(_mm(e.astype(BF16), vh) * (1.0 / l))
    om = jnp.concatenate(heads, axis=1).astype(BF16)
    out_ref[0] = h1 + _mm(om, wmo_ref[...])


def _mem_kv_kernel(mem_ref, rows_ref, w_ref, kv_ref):
    g = rows_ref[ROW_MEM_KV_G:ROW_MEM_KV_G + 1, :]
    kv_ref[0] = _mm(_rms(mem_ref[0], g).astype(BF16), w_ref[...]).astype(BF16)


def _ffn_kernel(h_ref, rows_ref, wg_ref, wu_ref, wd_ref, out_ref, *, final_norm):
    half = h_ref.shape[1] // 2
    parts = [slice(0, half), slice(half, 2 * half)]
    hs = [h_ref[0, r, :] for r in parts]
    xn = [_rms(h, rows_ref[ROW_FFN_G:ROW_FFN_G + 1, :]).astype(BF16) for h in hs]
    gu = [(_mm(x, wg_ref[...]), _mm(x, wu_ref[...])) for x in xn]
    act = [(g * (1.0 / (1.0 + jnp.exp(-g))) * u).astype(BF16) for g, u in gu]
    for r, h, a in zip(parts, hs, act):
        y = h + _mm(a, wd_ref[...])
        if final_norm:
            y = _rms(y, rows_ref[ROW_FINAL_G:ROW_FINAL_G + 1, :])
        out_ref[0, r, :] = y


def _const_spec(shape):
    return pl.BlockSpec(shape, lambda *_: (0,) * len(shape), pipeline_mode=pl.Buffered(1))


def _layer_spec(shape, l):
    return pl.BlockSpec((None,) + tuple(shape), lambda *_: (l,) + (0,) * len(shape),
                        pipeline_mode=pl.Buffered(1))


def _decay_constants():
    n_half = N_HEADS * HEAD_V
    sel = np.zeros((LANE, 2 * n_half), np.float32)
    ones = np.zeros((2 * n_half,), np.float32)
    for hd in range(N_HEADS):
        base = (hd // 2) * LANE + (HEAD_V if hd % 2 == 0 else 0)
        for part in range(N_SPLIT):
            src = part * N_HEADS + hd
            sel[src, base + part] = 1.0
            ones[base + N_SPLIT + part] = 1.0
            ones[n_half + base + part] = 1.0
            sel[src, n_half + base + N_SPLIT + part] = -1.0
    return sel, ones


def _rope_tables(seq):
    f32 = np.float32
    inv = (f32(1.0) / (f32(ROPE_THETA) ** (np.arange(0, ROPE, 2, dtype=f32) / f32(ROPE)))).astype(f32)
    ang = np.arange(seq, dtype=f32)[:, None] * inv[None, :]
    cos, sin = np.cos(ang).astype(f32), np.sin(ang).astype(f32)
    tab = np.zeros((seq, 2 * HEAD_PAD), f32)
    tab[:, NOPE:NOPE + ROPE] = np.concatenate([cos, cos], axis=1)
    tab[:, HEAD_PAD + NOPE:HEAD_PAD + NOPE + ROPE] = np.concatenate([-sin, sin], axis=1)
    return jnp.asarray(tab)


def _mixer_weights(w_in, w_uq, w_ukv, q_lora, kv_lora):
    depth = w_in.shape[0]
    fox = N_HEADS * HEAD_V
    o = q_lora + kv_lora
    kr = w_in[:, :, o:o + ROPE]
    o += ROPE
    wfq, wfk, wfv = w_in[:, :, o:o + fox], w_in[:, :, o + fox:o + 2 * fox], w_in[:, :, o + 2 * fox:o + 3 * fox]
    wfl = w_in[:, :, o + 3 * fox:o + 3 * fox + N_HEADS]

    zeros = lambda n: jnp.zeros(w_in.shape[:2] + (n,), BF16)
    cast = lambda w: w.astype(BF16)
    wx = jnp.concatenate(
        [cast(w_in[:, :, 0:q_lora + kv_lora])] + [cast(wfl)] * N_SPLIT
        + [zeros(NOPE - N_SPLIT * N_HEADS), cast(kr), zeros(HEAD_PAD - NOPE - ROPE),
           cast(wfq * (LOG2E * HEAD_V ** -0.5)), cast(wfk), cast(wfv)], axis=2)

    uq = w_uq.reshape(depth, q_lora, N_HEADS, NOPE + ROPE)
    pad_hi = jnp.zeros((depth, q_lora, N_HEADS, HEAD_PAD - NOPE - ROPE), F32)
    wq = jnp.concatenate([uq, pad_hi], axis=3).reshape(depth, q_lora, -1).astype(BF16)

    ukv = w_ukv.reshape(depth, kv_lora, N_HEADS, NOPE + HEAD_V)
    wkv = jnp.concatenate([ukv[..., :NOPE].reshape(depth, kv_lora, -1),
                           ukv[..., NOPE:].reshape(depth, kv_lora, -1)], axis=2).astype(BF16)
    return wx, wq, wkv


def kernel(x, mem, mix_norm_g, w_in, cq_norm_g, ckv_norm_g, w_uq, w_ukv, forget_bias, mla_out_g, fox_out_g, w_out, mem_q_norm_g, mem_kv_norm_g, w_mq, w_mkv, w_mo, ffn_norm_g, w_gate, w_up, w_down, final_norm_g):
    bsz, seq, d_model = x.shape
    depth = w_in.shape[0]
    q_lora = cq_norm_g.shape[1]
    kv_lora = ckv_norm_g.shape[1]
    mem_len = mem.shape[1]
    d_ff = w_gate.shape[2]
    width = N_HEADS * HEAD_PAD
    n_v = 2 * N_HEADS * HEAD_V
    assert d_model == width == n_v, "layout assumes d_model = 8 heads * 128"
    assert seq % BQ == 0 and seq % TM_PROJ == 0 and TM_PROJ % BK == 0 and BQ % BK == 0
    assert seq % TM_POST == 0 and seq % TM_FFN == 0 and BQ % TM_PROJ == 0 and TM_PROJ % BK == 0

    tabs = _rope_tables(seq)
    sel_np, decay_ones = _decay_constants()
    sel = jnp.asarray(sel_np, BF16)
    tri = jnp.asarray(np.tril(np.ones((TM_PROJ, TM_PROJ), np.float32)), BF16)
    params = pltpu.CompilerParams

    wx, wq, wkv = _mixer_weights(w_in, w_uq, w_ukv, q_lora, kv_lora)
    w_out_b, w_mq_b, w_mkv_b, w_mo_b = (w.astype(BF16) for w in (w_out, w_mq, w_mkv, w_mo))
    w_gate_b, w_up_b, w_down_b = (w.astype(BF16) for w in (w_gate, w_up, w_down))
    latent = jnp.pad(jnp.concatenate([cq_norm_g, ckv_norm_g] + [forget_bias] * N_SPLIT, axis=1),
                     ((0, 0), (0, d_model - q_lora - kv_lora - N_SPLIT * N_HEADS)))
    per_layer = lambda v: jnp.broadcast_to(v, (depth, d_model))
    rows = jnp.stack([mix_norm_g, latent, per_layer(jnp.asarray(decay_ones)),
                      jnp.concatenate([mla_out_g, fox_out_g], axis=1), mem_q_norm_g, ffn_norm_g,
                      per_layer(final_norm_g), mem_kv_norm_g], axis=1)
    rows_spec = lambda l: _layer_spec((8, d_model), l)

    h = x
    for l in range(depth):
        q_all, k_all, vt_all = pl.pallas_call(
            functools.partial(_proj_in_kernel, q_lora=q_lora, kv_lora=kv_lora),
            grid=(bsz, seq // TM_PROJ),
            in_specs=[
                pl.BlockSpec((1, TM_PROJ, d_model), lambda b, t: (b, t, 0)),
                rows_spec(l),
                pl.BlockSpec((TM_PROJ, 2 * LANE), lambda b, t: (t, 0)),
                _layer_spec(wx.shape[1:], l), _layer_spec(wq.shape[1:], l), _layer_spec(wkv.shape[1:], l),
                _const_spec(sel.shape), _const_spec(tri.shape),
            ],
            out_specs=[
                pl.BlockSpec((1, 1, 2 * width, TM_PROJ), lambda b, t: (b, t, 0, 0)),
                pl.BlockSpec((1, TM_PROJ, 2 * width), lambda b, t: (b, t, 0)),
                pl.BlockSpec((1, TM_PROJ // BK, n_v, BK), lambda b, t: (b, t, 0, 0)),
            ],
            out_shape=[
                jax.ShapeDtypeStruct((bsz, seq // TM_PROJ, 2 * width, TM_PROJ), BF16),
                jax.ShapeDtypeStruct((bsz, seq, 2 * width), BF16),
                jax.ShapeDtypeStruct((bsz, seq // BK, n_v, BK), BF16),
            ],
            scratch_shapes=[pltpu.VMEM((1, LANE), F32)],
            compiler_params=params(dimension_semantics=("arbitrary", "arbitrary"),
                                   vmem_limit_bytes=VMEM_LIMIT),
            name=f"proj_in_{l}",
        )(h, rows, tabs, wx, wq, wkv, sel, tri)

        o_all = pl.pallas_call(
            functools.partial(_attn_kernel, bq=BQ),
            grid=(bsz, N_HEADS),
            in_specs=[
                pl.BlockSpec((1, seq // TM_PROJ, 2 * HEAD_PAD, TM_PROJ), lambda b, p: (b, 0, p, 0)),
                pl.BlockSpec((1, seq, 2 * HEAD_PAD), lambda b, p: (b, 0, p)),
                pl.BlockSpec((1, seq // BK, 2 * HEAD_V, BK), lambda b, p: (b, 0, p, 0)),
            ],
            out_specs=pl.BlockSpec((1, seq, 2 * HEAD_V), lambda b, p: (b, 0, p)),
            out_shape=jax.ShapeDtypeStruct((bsz, seq, n_v), F32),
            scratch_shapes=[
                pltpu.VMEM((BQ // BK, 2, BK, BQ + 3 * LANE), F32),
                pltpu.VMEM((BQ // BK, 2, BK, BQ + 3 * LANE), BF16),
                pltpu.VMEM((BQ // BK, 2, 1, BQ), F32),
                pltpu.VMEM((2, 1, BQ), F32),
                pltpu.VMEM((2, 1, BQ), F32),
                pltpu.VMEM((2, HEAD_V + ONES_ROWS, BQ), F32),
                pltpu.VMEM((2 * HEAD_V, BQ), F32),
            ],
            compiler_params=params(dimension_semantics=("parallel", "parallel"),
                                   vmem_limit_bytes=VMEM_LIMIT),
            name=f"attn_{l}",
        )(q_all, k_all, vt_all)

        kv_mem = pl.pallas_call(
            _mem_kv_kernel,
            grid=(bsz,),
            in_specs=[
                pl.BlockSpec((1, mem_len, d_model), lambda b: (b, 0, 0)),
                rows_spec(l),
                _layer_spec(w_mkv.shape[1:], l),
            ],
            out_specs=pl.BlockSpec((1, mem_len, w_mkv.shape[2]), lambda b: (b, 0, 0)),
            out_shape=jax.ShapeDtypeStruct((bsz, mem_len, w_mkv.shape[2]), BF16),
            compiler_params=params(dimension_semantics=("parallel",), vmem_limit_bytes=VMEM_LIMIT),
            name=f"mem_kv_{l}",
        )(mem, rows, w_mkv_b)

        h = pl.pallas_call(
            _post_attn_kernel,
            grid=(bsz, seq // TM_POST),
            in_specs=[
                pl.BlockSpec((1, TM_POST, n_v), lambda b, t: (b, t, 0)),
                pl.BlockSpec((1, TM_POST, d_model), lambda b, t: (b, t, 0)),
                rows_spec(l),
                pl.BlockSpec((1, mem_len, w_mkv.shape[2]), lambda b, t: (b, 0, 0)),
                _layer_spec(w_out.shape[1:], l), _layer_spec(w_mq.shape[1:], l), _layer_spec(w_mo.shape[1:], l),
            ],
            out_specs=pl.BlockSpec((1, TM_POST, d_model), lambda b, t: (b, t, 0)),
            out_shape=jax.ShapeDtypeStruct((bsz, seq, d_model), F32),
            compiler_params=params(dimension_semantics=("parallel", "parallel"),
                                   vmem_limit_bytes=VMEM_LIMIT),
            name=f"post_attn_{l}",
        )(o_all, h, rows, kv_mem, w_out_b, w_mq_b, w_mo_b)

        last = l == depth - 1
        h = pl.pallas_call(
            functools.partial(_ffn_kernel, final_norm=last),
            grid=(bsz, seq // TM_FFN),
            in_specs=[
                pl.BlockSpec((1, TM_FFN, d_model), lambda b, t: (b, t, 0)),
                rows_spec(l),
                _layer_spec((d_model, d_ff), l), _layer_spec((d_model, d_ff), l), _layer_spec((d_ff, d_model), l),
            ],
            out_specs=pl.BlockSpec((1, TM_FFN, d_model), lambda b, t: (b, t, 0)),
            out_shape=jax.ShapeDtypeStruct((bsz, seq, d_model), F32),
            compiler_params=params(dimension_semantics=("parallel", "parallel"),
                                   vmem_limit_bytes=VMEM_LIMIT),
            name=f"ffn_{l}",
        )(h, rows, w_gate_b, w_up_b, w_down_b)
    return h
```

```python
import functools

import numpy as np
import jax
import jax.numpy as jnp
from jax import lax
from jax.experimental import pallas as pl
from jax.experimental.pallas import tpu as pltpu

F32 = jnp.float32
BF16 = jnp.bfloat16

EPS = 1e-6
ROPE_THETA = 10000.0
N_HEADS = 8
NOPE = 64
ROPE = 32
HEAD_V = 64
HEAD_PAD = 128
N_MEM_HEADS = 4
MEM_HD = 128
N_SPLIT = 3
ONES_ROWS = 16

LANE = 128
TM_PROJ = 512
TM_POST = 1024
TM_FFN = 512
BQ = 1024
BK = 256
MASKED = -2e30
M_INIT = -1e30
LOG2E = 1.4426950408889634
VMEM_LIMIT = 56 * 1024 * 1024

ROW_MIX_G, ROW_LATENT, ROW_DECAY_ONES, ROW_OUT_G, ROW_MEM_Q_G, ROW_FFN_G, ROW_FINAL_G, ROW_MEM_KV_G = range(8)


def _mm(a, b):
    return jnp.dot(a, b, preferred_element_type=F32)


def _mm_nt(a, b):
    return lax.dot_general(a, b, (((1,), (1,)), ((), ())), preferred_element_type=F32)


def _rms(x, g):
    return x * lax.rsqrt(jnp.mean(x * x, axis=-1, keepdims=True) + EPS) * g


def _split3(x):
    hi = x.astype(BF16).astype(F32)
    r = x - hi
    mid = r.astype(BF16).astype(F32)
    lo = (r - mid).astype(BF16).astype(F32)
    return hi, mid, lo


def _proj_in_kernel(h_ref, rows_ref, tabs_ref, wx_ref, wq_ref, wkv_ref, sel_ref, tri_ref,
                    q_ref, k_ref, vt_ref, carry_ref, *, q_lora, kv_lora):
    tm = h_ref.shape[1]
    n_sub = vt_ref.shape[1]
    bk = vt_ref.shape[3]
    width = N_HEADS * HEAD_PAD
    nv = N_HEADS * HEAD_V

    @pl.when(pl.program_id(1) == 0)
    def _():
        carry_ref[...] = jnp.zeros_like(carry_ref)

    xn = _rms(h_ref[0], rows_ref[ROW_MIX_G:ROW_MIX_G + 1, :]).astype(BF16)

    o_ckv = q_lora
    o_kf = o_ckv + kv_lora
    o_fq = o_kf + LANE
    o_fk = o_fq + nv
    o_fv = o_fk + nv

    cos_k = tabs_ref[:, 0:LANE]
    sin_k = tabs_ref[:, LANE:2 * LANE]
    q_scale = LOG2E * (NOPE + ROPE) ** -0.5
    lane = lax.broadcasted_iota(jnp.int32, (tm, LANE), 1)
    cos_q = jnp.where(lane < NOPE, q_scale, q_scale * cos_k)
    sin_q = q_scale * sin_k
    low = lane < HEAD_V

    def put_values(v, row0):
        vt = v.T.astype(BF16)
        for c in range(n_sub):
            vt_ref[0, c, row0:row0 + nv, :] = vt[:, c * bk:(c + 1) * bk]

    def put_head_pairs(out_ref, col0, narrow, fill):
        for pair in range(N_HEADS // 2):
            blk = narrow[:, pair * LANE:(pair + 1) * LANE]
            other = fill(pair)
            c = col0 + 2 * pair * HEAD_PAD
            out_ref[0, :, c:c + HEAD_PAD] = jnp.where(low, blk, other).astype(BF16)
            odd = pltpu.roll(jnp.where(low, other, blk), HEAD_V, axis=1)
            out_ref[0, :, c + HEAD_PAD:c + 2 * HEAD_PAD] = odd.astype(BF16)

    cq = _mm(xn, wx_ref[:, 0:q_lora])
    ckv = _mm(xn, wx_ref[:, o_ckv:o_kf])
    o_bias = q_lora + kv_lora
    kf = _mm(xn, wx_ref[:, o_kf:o_fq])
    f3 = kf + rows_ref[ROW_LATENT:ROW_LATENT + 1, o_bias:o_bias + LANE]
    put_values(_mm(xn, wx_ref[:, o_fv:o_fv + nv]), nv)

    cqn = _rms(cq, rows_ref[ROW_LATENT:ROW_LATENT + 1, 0:q_lora]).astype(BF16)
    qa = _mm(cqn, wq_ref[...])
    first_half = lane < NOPE + ROPE // 2

    def rotary(x, cos, sin):
        swapped = jnp.where(first_half, pltpu.roll(x, LANE - ROPE // 2, axis=1),
                            pltpu.roll(x, ROPE // 2, axis=1))
        return x * cos + swapped * sin

    for hd in range(N_HEADS):
        sl = slice(hd * HEAD_PAD, (hd + 1) * HEAD_PAD)
        q_ref[0, :, sl] = rotary(qa[:, sl], cos_q, sin_q).astype(BF16)

    ckvn = _rms(ckv, rows_ref[ROW_LATENT:ROW_LATENT + 1, q_lora:q_lora + kv_lora]).astype(BF16)
    kn = _mm(ckvn, wkv_ref[:, 0:nv])
    kr = rotary(kf, cos_k, sin_k)
    kr_both = kr + pltpu.roll(kr, HEAD_V, axis=1)
    put_head_pairs(k_ref, 0, kn, lambda pair: kr_both)
    put_values(_mm(ckvn, wkv_ref[:, nv:2 * nv]), 0)

    live = lane < N_SPLIT * N_HEADS
    log_f = jnp.minimum(f3, 0.0) - jnp.log1p(jnp.exp(-jnp.abs(f3)))
    log_f = jnp.where(live, log_f, 0.0)
    pieces = jnp.concatenate([p.astype(BF16) for p in _split3(log_f)], axis=1)
    csum = _mm(tri_ref[...], pieces)
    fq = _mm(xn, wx_ref[:, o_fq:o_fk])
    cum = csum[:, 0:LANE] + csum[:, LANE:2 * LANE] + csum[:, 2 * LANE:3 * LANE] + carry_ref[...]
    carry_ref[...] = cum[tm - 1:tm, :]
    c_hi, c_mid, c_lo = _split3(cum * LOG2E)
    c_sel = jnp.where(lane < N_HEADS, c_hi, jnp.where(lane < 2 * N_HEADS, c_mid, c_lo))
    c_sel = jnp.where(live, c_sel, 0.0).astype(BF16)
    fk = _mm(xn, wx_ref[:, o_fk:o_fv])
    aug = _mm(c_sel, sel_ref[...])
    aug = aug + rows_ref[ROW_DECAY_ONES:ROW_DECAY_ONES + 1, :]
    put_head_pairs(q_ref, width, fq, lambda pair: aug[:, pair * LANE:(pair + 1) * LANE])
    put_head_pairs(k_ref, width, fk, lambda pair: aug[:, nv + pair * LANE:nv + (pair + 1) * LANE])


def _attn_kernel(q_ref, k_ref, vt_ref, o_ref, s_scr, p_scr, mt_scr, al_scr, m_scr, acc_scr, out_scr, *, bq):
    bk = vt_ref.shape[3]
    n_sub = bq // bk
    nq = q_ref.shape[1] // bq
    heads = [slice(hh * HEAD_PAD, (hh + 1) * HEAD_PAD) for hh in range(2)]

    def k_tile(j):
        return k_ref[0, pl.ds(pl.multiple_of(j * bk, bk), bk), :]

    def q_rows(qi, first, hh):
        return q_ref[0, pl.ds(pl.multiple_of(qi * bq + first, bk), bq - first), heads[hh]]

    def put_scores(u, hh, s):
        s_scr[u, hh, :, 0:bq] = s
        mt_scr[u, hh] = jnp.max(s, axis=0, keepdims=True)

    def scores_group(g, qi):
        for u in range(n_sub):
            kb = k_tile(n_sub * g + u)
            for hh in range(2):
                put_scores(u, hh, _mm_nt(kb[:, heads[hh]], q_rows(qi, 0, hh)))

    def scores_diagonal(qi):
        tri = (lax.broadcasted_iota(jnp.int32, (bk, bk), 0)
               <= lax.broadcasted_iota(jnp.int32, (bk, bk), 1))
        for u in range(n_sub):
            kb = k_tile(n_sub * qi + u)
            for hh in range(2):
                s = _mm_nt(kb[:, heads[hh]], q_rows(qi, u * bk, hh))
                parts = [jnp.full((bk, u * bk), MASKED, F32)] if u > 0 else []
                parts.append(jnp.where(tri, s[:, :bk], MASKED))
                if u < n_sub - 1:
                    parts.append(s[:, bk:])
                put_scores(u, hh, jnp.concatenate(parts, axis=1) if len(parts) > 1 else parts[0])

    def softmax_group():
        for hh in range(2):
            mt = mt_scr[0, hh]
            for u in range(1, n_sub):
                mt = jnp.maximum(mt, mt_scr[u, hh])
            m_new = jnp.maximum(m_scr[hh], mt)
            al_scr[hh] = jnp.exp2(m_scr[hh] - m_new)
            m_scr[hh] = m_new
            for u in range(n_sub):
                p_scr[u, hh, :, 0:bq] = jnp.exp2(s_scr[u, hh, :, 0:bq] - m_new).astype(BF16)

    def values_at(j0):
        ones = jnp.ones((ONES_ROWS, bk), BF16)
        for hh in range(2):
            pv = None
            for u in range(n_sub):
                vb = vt_ref[0, j0 + u]
                v = jnp.concatenate([vb[hh * HEAD_V:(hh + 1) * HEAD_V, :], ones], axis=0)
                d = _mm(v, p_scr[u, hh, :, 0:bq])
                pv = d if pv is None else pv + d
            acc_scr[hh] = al_scr[hh] * acc_scr[hh] + pv

    def last_group_start(qi):
        return jnp.where(qi == 0, 0, n_sub * (qi - 1))

    def reset():
        m_scr[...] = jnp.full_like(m_scr, M_INIT)
        acc_scr[...] = jnp.zeros_like(acc_scr)

    def normalise():
        out_scr[...] = jnp.concatenate(
            [acc_scr[hh, 0:HEAD_V] * (1.0 / acc_scr[hh, HEAD_V:HEAD_V + 1]) for hh in range(2)], axis=0)
        reset()

    def store(qi):
        o_ref[0, pl.ds(pl.multiple_of(qi * bq, bq), bq), :] = out_scr[...].T.astype(o_ref.dtype)

    def below_diagonal(qi):
        def body(g, carry):
            softmax_group()
            scores_group(g, qi)
            values_at(jnp.where(g == 0, n_sub * qi, n_sub * (g - 1)))
            return carry

        lax.fori_loop(0, qi, body, 0)

    reset()
    scores_diagonal(0)

    def block(qi, store_previous):
        if store_previous:
            store(qi - 2)
        softmax_group()
        scores_diagonal(qi)
        values_at(last_group_start(qi - 1))
        normalise()
        below_diagonal(qi)

    if nq > 1:
        block(1, False)
        lax.fori_loop(2, nq, lambda qi, carry: (block(qi, True), carry)[1], 0)
        store(nq - 2)
    softmax_group()
    values_at(last_group_start(nq - 1))
    normalise()
    store(nq - 1)


def _post_attn_kernel(o_ref, h_ref, rows_ref, kv_ref, wo_ref, wmq_ref, wmo_ref, out_ref):
    half = o_ref.shape[2] // 2
    o = o_ref[0].astype(F32)
    merged = jnp.concatenate(
        [_rms(o[:, 0:half], rows_ref[ROW_OUT_G:ROW_OUT_G + 1, 0:half]),
         _rms(o[:, half:], rows_ref[ROW_OUT_G:ROW_OUT_G + 1, half:])],
        axis=1).astype(BF16)
    h1 = h_ref[0] + _mm(merged, wo_ref[...])
    xn = _rms(h1, rows_ref[ROW_MEM_Q_G:ROW_MEM_Q_G + 1, :]).astype(BF16)
    q = (_mm(xn, wmq_ref[...]) * (MEM_HD ** -0.5)).astype(BF16)
    kv = kv_ref[0]
    heads = []
    for hd in range(N_MEM_HEADS):
        kh = kv[:, 2 * hd * MEM_HD:(2 * hd + 1) * MEM_HD]
        vh = kv[:, (2 * hd + 1) * MEM_HD:(2 * hd + 2) * MEM_HD]
        s = _mm_nt(q[:, hd * MEM_HD:(hd + 1) * MEM_HD], kh)
        e = jnp.exp(s - jnp.max(s, axis=-1, keepdims=True))
        l = jnp.sum(e, axis=-1, keepdims=True)
        heads.append(_mm(e.astype(BF16), vh) * (1.0 / l))
    om = jnp.concatenate(heads, axis=1).astype(BF16)
    out_ref[0] = h1 + _mm(om, wmo_ref[...])


def _mem_kv_kernel(mem_ref, rows_ref, w_ref, kv_ref):
    g = rows_ref[ROW_MEM_KV_G:ROW_MEM_KV_G + 1, :]
    kv_ref[0] = _mm(_rms(mem_ref[0], g).astype(BF16), w_ref[...]).astype(BF16)


def _ffn_kernel(h_ref, rows_ref, wg_ref, wu_ref, wd_ref, out_ref, *, final_norm):
    half = h_ref.shape[1] // 2
    parts = [slice(0, half), slice(half, 2 * half)]
    hs = [h_ref[0, r, :] for r in parts]
    xn = [_rms(h, rows_ref[ROW_FFN_G:ROW_FFN_G + 1, :]).astype(BF16) for h in hs]
    gu = [(_mm(x, wg_ref[...]), _mm(x, wu_ref[...])) for x in xn]
    act = [(g * (1.0 / (1.0 + jnp.exp(-g))) * u).astype(BF16) for g, u in gu]
    for r, h, a in zip(parts, hs, act):
        y = h + _mm(a, wd_ref[...])
        if final_norm:
            y = _rms(y, rows_ref[ROW_FINAL_G:ROW_FINAL_G + 1, :])
        out_ref[0, r, :] = y


def _const_spec(shape):
    return pl.BlockSpec(shape, lambda *_: (0,) * len(shape), pipeline_mode=pl.Buffered(1))


def _layer_spec(shape, l):
    return pl.BlockSpec((None,) + tuple(shape), lambda *_: (l,) + (0,) * len(shape),
                        pipeline_mode=pl.Buffered(1))


def _decay_constants():
    n_half = N_HEADS * HEAD_V
    sel = np.zeros((LANE, 2 * n_half), np.float32)
    ones = np.zeros((2 * n_half,), np.float32)
    for hd in range(N_HEADS):
        base = (hd // 2) * LANE + (HEAD_V if hd % 2 == 0 else 0)
        for part in range(N_SPLIT):
            src = part * N_HEADS + hd
            sel[src, base + part] = 1.0
            ones[base + N_SPLIT + part] = 1.0
            ones[n_half + base + part] = 1.0
            sel[src, n_half + base + N_SPLIT + part] = -1.0
    return sel, ones


def _rope_tables(seq):
    f32 = np.float32
    inv = (f32(1.0) / (f32(ROPE_THETA) ** (np.arange(0, ROPE, 2, dtype=f32) / f32(ROPE)))).astype(f32)
    ang = np.arange(seq, dtype=f32)[:, None] * inv[None, :]
    cos, sin = np.cos(ang).astype(f32), np.sin(ang).astype(f32)
    tab = np.zeros((seq, 2 * HEAD_PAD), f32)
    tab[:, NOPE:NOPE + ROPE] = np.concatenate([cos, cos], axis=1)
    tab[:, HEAD_PAD + NOPE:HEAD_PAD + NOPE + ROPE] = np.concatenate([-sin, sin], axis=1)
    return jnp.asarray(tab)


def _mixer_weights(w_in, w_uq, w_ukv, q_lora, kv_lora):
    depth = w_in.shape[0]
    fox = N_HEADS * HEAD_V
    o = q_lora + kv_lora
    kr = w_in[:, :, o:o + ROPE]
    o += ROPE
    wfq, wfk, wfv = w_in[:, :, o:o + fox], w_in[:, :, o + fox:o + 2 * fox], w_in[:, :, o + 2 * fox:o + 3 * fox]
    wfl = w_in[:, :, o + 3 * fox:o + 3 * fox + N_HEADS]

    zeros = lambda n: jnp.zeros(w_in.shape[:2] + (n,), BF16)
    cast = lambda w: w.astype(BF16)
    wx = jnp.concatenate(
        [cast(w_in[:, :, 0:q_lora + kv_lora])] + [cast(wfl)] * N_SPLIT
        + [zeros(NOPE - N_SPLIT * N_HEADS), cast(kr), zeros(HEAD_PAD - NOPE - ROPE),
           cast(wfq * (LOG2E * HEAD_V ** -0.5)), cast(wfk), cast(wfv)], axis=2)

    uq = w_uq.reshape(depth, q_lora, N_HEADS, NOPE + ROPE)
    pad_hi = jnp.zeros((depth, q_lora, N_HEADS, HEAD_PAD - NOPE - ROPE), F32)
    wq = jnp.concatenate([uq, pad_hi], axis=3).reshape(depth, q_lora, -1).astype(BF16)

    ukv = w_ukv.reshape(depth, kv_lora, N_HEADS, NOPE + HEAD_V)
    wkv = jnp.concatenate([ukv[..., :NOPE].reshape(depth, kv_lora, -1),
                           ukv[..., NOPE:].reshape(depth, kv_lora, -1)], axis=2).astype(BF16)
    return wx, wq, wkv


def kernel(x, mem, mix_norm_g, w_in, cq_norm_g, ckv_norm_g, w_uq, w_ukv, forget_bias, mla_out_g, fox_out_g, w_out, mem_q_norm_g, mem_kv_norm_g, w_mq, w_mkv, w_mo, ffn_norm_g, w_gate, w_up, w_down, final_norm_g):
    bsz, seq, d_model = x.shape
    depth = w_in.shape[0]
    q_lora = cq_norm_g.shape[1]
    kv_lora = ckv_norm_g.shape[1]
    mem_len = mem.shape[1]
    d_ff = w_gate.shape[2]
    width = N_HEADS * HEAD_PAD
    n_v = 2 * N_HEADS * HEAD_V
    assert d_model == width == n_v, "layout assumes d_model = 8 heads * 128"
    assert seq % BQ == 0 and seq % TM_PROJ == 0 and TM_PROJ % BK == 0 and BQ % BK == 0
    assert seq % TM_POST == 0 and seq % TM_FFN == 0

    tabs = _rope_tables(seq)
    sel_np, decay_ones = _decay_constants()
    sel = jnp.asarray(sel_np, BF16)
    tri = jnp.asarray(np.tril(np.ones((TM_PROJ, TM_PROJ), np.float32)), BF16)
    params = pltpu.CompilerParams

    wx, wq, wkv = _mixer_weights(w_in, w_uq, w_ukv, q_lora, kv_lora)
    w_out_b, w_mq_b, w_mkv_b, w_mo_b = (w.astype(BF16) for w in (w_out, w_mq, w_mkv, w_mo))
    w_gate_b, w_up_b, w_down_b = (w.astype(BF16) for w in (w_gate, w_up, w_down))
    latent = jnp.pad(jnp.concatenate([cq_norm_g, ckv_norm_g] + [forget_bias] * N_SPLIT, axis=1),
                     ((0, 0), (0, d_model - q_lora - kv_lora - N_SPLIT * N_HEADS)))
    per_layer = lambda v: jnp.broadcast_to(v, (depth, d_model))
    rows = jnp.stack([mix_norm_g, latent, per_layer(jnp.asarray(decay_ones)),
                      jnp.concatenate([mla_out_g, fox_out_g], axis=1), mem_q_norm_g, ffn_norm_g,
                      per_layer(final_norm_g), mem_kv_norm_g], axis=1)
    rows_spec = lambda l: _layer_spec((8, d_model), l)

    h = x
    for l in range(depth):
        q_all, k_all, vt_all = pl.pallas_call(
            functools.partial(_proj_in_kernel, q_lora=q_lora, kv_lora=kv_lora),
            grid=(bsz, seq // TM_PROJ),
            in_specs=[
                pl.BlockSpec((1, TM_PROJ, d_model), lambda b, t: (b, t, 0)),
                rows_spec(l),
                pl.BlockSpec((TM_PROJ, 2 * LANE), lambda b, t: (t, 0)),
                _layer_spec(wx.shape[1:], l), _layer_spec(wq.shape[1:], l), _layer_spec(wkv.shape[1:], l),
                _const_spec(sel.shape), _const_spec(tri.shape),
            ],
            out_specs=[
                pl.BlockSpec((1, TM_PROJ, 2 * width), lambda b, t: (b, t, 0)),
                pl.BlockSpec((1, TM_PROJ, 2 * width), lambda b, t: (b, t, 0)),
                pl.BlockSpec((1, TM_PROJ // BK, n_v, BK), lambda b, t: (b, t, 0, 0)),
            ],
            out_shape=[
                jax.ShapeDtypeStruct((bsz, seq, 2 * width), BF16),
                jax.ShapeDtypeStruct((bsz, seq, 2 * width), BF16),
                jax.ShapeDtypeStruct((bsz, seq // BK, n_v, BK), BF16),
            ],
            scratch_shapes=[pltpu.VMEM((1, LANE), F32)],
            compiler_params=params(dimension_semantics=("arbitrary", "arbitrary"),
                                   vmem_limit_bytes=VMEM_LIMIT),
            name=f"proj_in_{l}",
        )(h, rows, tabs, wx, wq, wkv, sel, tri)

        o_all = pl.pallas_call(
            functools.partial(_attn_kernel, bq=BQ),
            grid=(bsz, N_HEADS),
            in_specs=[
                pl.BlockSpec((1, seq, 2 * HEAD_PAD), lambda b, p: (b, 0, p)),
                pl.BlockSpec((1, seq, 2 * HEAD_PAD), lambda b, p: (b, 0, p)),
                pl.BlockSpec((1, seq // BK, 2 * HEAD_V, BK), lambda b, p: (b, 0, p, 0)),
            ],
            out_specs=pl.BlockSpec((1, seq, 2 * HEAD_V), lambda b, p: (b, 0, p)),
            out_shape=jax.ShapeDtypeStruct((bsz, seq, n_v), BF16),
            scratch_shapes=[
                pltpu.VMEM((BQ // BK, 2, BK, BQ + 3 * LANE), F32),
                pltpu.VMEM((BQ // BK, 2, BK, BQ + 3 * LANE), BF16),
                pltpu.VMEM((BQ // BK, 2, 1, BQ), F32),
                pltpu.VMEM((2, 1, BQ), F32),
                pltpu.VMEM((2, 1, BQ), F32),
                pltpu.VMEM((2, HEAD_V + ONES_ROWS, BQ), F32),
                pltpu.VMEM((2 * HEAD_V, BQ), F32),
            ],
            compiler_params=params(dimension_semantics=("parallel", "parallel"),
                                   vmem_limit_bytes=VMEM_LIMIT),
            name=f"attn_{l}",
        )(q_all, k_all, vt_all)

        kv_mem = pl.pallas_call(
            _mem_kv_kernel,
            grid=(bsz,),
            in_specs=[
                pl.BlockSpec((1, mem_len, d_model), lambda b: (b, 0, 0)),
                rows_spec(l),
                _layer_spec(w_mkv.shape[1:], l),
            ],
            out_specs=pl.BlockSpec((1, mem_len, w_mkv.shape[2]), lambda b: (b, 0, 0)),
            out_shape=jax.ShapeDtypeStruct((bsz, mem_len, w_mkv.shape[2]), BF16),
            compiler_params=params(dimension_semantics=("parallel",), vmem_limit_bytes=VMEM_LIMIT),
            name=f"mem_kv_{l}",
        )(mem, rows, w_mkv_b)

        h = pl.pallas_call(
            _post_attn_kernel,
            grid=(bsz, seq // TM_POST),
            in_specs=[
                pl.BlockSpec((1, TM_POST, n_v), lambda b, t: (b, t, 0)),
                pl.BlockSpec((1, TM_POST, d_model), lambda b, t: (b, t, 0)),
                rows_spec(l),
                pl.BlockSpec((1, mem_len, w_mkv.shape[2]), lambda b, t: (b, 0, 0)),
                _layer_spec(w_out.shape[1:], l), _layer_spec(w_mq.shape[1:], l), _layer_spec(w_mo.shape[1:], l),
            ],
            out_specs=pl.BlockSpec((1, TM_POST, d_model), lambda b, t: (b, t, 0)),
            out_shape=jax.ShapeDtypeStruct((bsz, seq, d_model), F32),
            compiler_params=params(dimension_semantics=("parallel", "parallel"),
                                   vmem_limit_bytes=VMEM_LIMIT),
            name=f"post_attn_{l}",
        )(o_all, h, rows, kv_mem, w_out_b, w_mq_b, w_mo_b)

        last = l == depth - 1
        h = pl.pallas_call(
            functools.partial(_ffn_kernel, final_norm=last),
            grid=(bsz, seq // TM_FFN),
            in_specs=[
                pl.BlockSpec((1, TM_FFN, d_model), lambda b, t: (b, t, 0)),
                rows_spec(l),
                _layer_spec((d_model, d_ff), l), _layer_spec((d_model, d_ff), l), _layer_spec((d_ff, d_model), l),
            ],
            out_specs=pl.BlockSpec((1, TM_FFN, d_model), lambda b, t: (b, t, 0)),
            out_shape=jax.ShapeDtypeStruct((bsz, seq, d_model), F32),
            compiler_params=params(dimension_semantics=("parallel", "parallel"),
                                   vmem_limit_bytes=VMEM_LIMIT),
            name=f"ffn_{l}",
        )(h, rows, w_gate_b, w_up_b, w_down_b)
    return h
```
